```python
import jax
import jax.numpy as jnp
from jax import lax
import numpy as np

D_MODEL = 1024
BATCH = 8
SEQ = 2048
DEPTH = 2

MEM_LEN = 256
HEAD_DIM = 64
N_BRANCHES = 3
LN_EPS = 1e-5
BAND_BLOCK = 128

DIL_GROUPS = ((128, 1), (512, 4), (2048, 16))
DIL_HEADS = 4

GLA_HEADS = 4
GLA_DK = D_MODEL // 2 // GLA_HEADS
GLA_DV = D_MODEL // GLA_HEADS
GLA_RANK = 16
GLA_TAU = 16.0
GLA_CHUNK = 64

NSA_HEADS = 16
NSA_KV_GROUPS = 4
NSA_BRANCHES = 3
NSA_CMP_LEN = 32
NSA_CMP_STRIDE = 16
NSA_CMP_HIDDEN = 2 * HEAD_DIM
NSA_SEL_LEN = 64
NSA_N_SELECT = 16
NSA_WINDOW = 512
NSA_Q_BLOCK = 32

XATTN_HEADS = 4

N_EXPERTS = 32
TOP_K = 4
D_FF = D_MODEL
SWIGLU_LIMIT = 7.0
SWIGLU_ALPHA = 1.702
MOE_BLOCK = 128

DEEPNORM_ALPHA = (2 * DEPTH) ** 0.25
DEEPNORM_BETA = (8 * DEPTH) ** -0.25

A_W = len(DIL_GROUPS) * DIL_HEADS * HEAD_DIM
B_K = GLA_HEADS * GLA_DK
B_V = GLA_HEADS * GLA_DV
C_Q = NSA_HEADS * HEAD_DIM
C_KV = NSA_KV_GROUPS * HEAD_DIM
IN_WIDTHS = (A_W, A_W, A_W,
             B_K, B_K, B_V, B_V, GLA_RANK,
             C_Q, C_KV, C_KV, C_KV, C_KV, C_KV, C_KV, NSA_HEADS * NSA_BRANCHES,
             N_BRANCHES * D_MODEL)
N_IN = sum(IN_WIDTHS)

kernel_name = 'hybrid_dilated_gla_nsa_moe_deepnorm'


def layer_norm(x, g, b):
    xf = x.astype(jnp.float32)
    mu = jnp.mean(xf, axis=-1, keepdims=True)
    var = jnp.mean(jnp.square(xf - mu), axis=-1, keepdims=True)
    return ((xf - mu) * lax.rsqrt(var + LN_EPS)).astype(x.dtype) * g + b


def banded_attention(q, k, v, max_dist):
    n, seq_len, g, r, hd = q.shape
    blk = BAND_BLOCK
    n_prev = -(-max_dist // blk)
    nb = -(-seq_len // blk)
    pad = nb * blk - seq_len
    qp = jnp.pad(q, ((0, 0), (0, pad), (0, 0), (0, 0), (0, 0)))
    kv_pad = ((0, 0), (n_prev * blk, pad), (0, 0), (0, 0))
    kp = jnp.pad(k, kv_pad).reshape(n, nb + n_prev, blk, g, hd)
    vp = jnp.pad(v, kv_pad).reshape(n, nb + n_prev, blk, g, hd)
    kw = jnp.concatenate([kp[:, i:i + nb] for i in range(n_prev + 1)], axis=2)
    vw = jnp.concatenate([vp[:, i:i + nb] for i in range(n_prev + 1)], axis=2)
    width = (n_prev + 1) * blk
    qi = jnp.arange(blk)[:, None]
    kj = jnp.arange(width)[None, :]
    dist = qi + n_prev * blk - kj
    key_abs = jnp.arange(nb)[:, None, None] * blk - n_prev * blk + kj[None]
    mask = ((dist >= 0) & (dist <= max_dist))[None] & (key_abs >= 0)
    scale = hd ** -0.5

    def one_block(args):
        qb, kb, vb, mb = args
        s = jnp.einsum('nqgrd,nkgd->ngrqk', qb, kb).astype(jnp.float32) * scale
        s = jnp.where(mb, s, -jnp.inf)
        m = jnp.max(s, axis=-1, keepdims=True)
        p = jnp.exp(s - m)
        den = jnp.sum(p, axis=-1, keepdims=True)
        o = jnp.einsum('ngrqk,nkgd->nqgrd', (p / den).astype(vb.dtype), vb)
        lse = (m + jnp.log(den))[..., 0]
        return o, jnp.transpose(lse, (0, 3, 1, 2))

    qb = jnp.moveaxis(qp.reshape(n, nb, blk, g, r, hd), 1, 0)
    o, lse = lax.map(one_block, (qb, jnp.moveaxis(kw, 1, 0), jnp.moveaxis(vw, 1, 0), mask))
    o = jnp.moveaxis(o, 0, 1).reshape(n, nb * blk, g, r, hd)[:, :seq_len]
    lse = jnp.moveaxis(lse, 0, 1).reshape(n, nb * blk, g, r)[:, :seq_len]
    return o, lse


def dilated_attention(q, k, v):
    bsz, seq, _, nh, hd = q.shape
    outs, lses = [], []
    for gi, (window, dil) in enumerate(DIL_GROUPS):
        sub_len = seq // dil

        def to_sub(t):
            return t.reshape(bsz, sub_len, dil, nh, hd).transpose(0, 2, 1, 3, 4).reshape(bsz * dil, sub_len, nh, hd)

        o, lse = banded_attention(to_sub(q[:, :, gi])[:, :, :, None], to_sub(k[:, :, gi]), to_sub(v[:, :, gi]), window // dil)
        outs.append(o[:, :, :, 0].reshape(bsz, dil, sub_len, nh, hd).transpose(0, 2, 1, 3, 4).reshape(bsz, seq, nh, hd))
        lses.append(lse[..., 0].reshape(bsz, dil, sub_len, nh).transpose(0, 2, 1, 3).reshape(bsz, seq, nh))
    w = jax.nn.softmax(jnp.stack(lses), axis=0)
    return jnp.einsum('gbsh,gbshd->bshd', w.astype(q.dtype), jnp.stack(outs))


def gla_mixer(q, k, v, r, lr, w_alpha2, b_alpha, norm_g):
    bsz, seq, _ = q.shape
    nh, dk, dv, c = GLA_HEADS, GLA_DK, GLA_DV, GLA_CHUNK
    dtype = q.dtype
    log_a = jax.nn.log_sigmoid((lr @ w_alpha2 + b_alpha).astype(jnp.float32)) / GLA_TAU
    n_chunks = seq // c

    def chunks(t, width):
        return t.reshape(bsz, n_chunks, c, nh, width).transpose(1, 0, 3, 2, 4).astype(jnp.float32)

    qc = chunks(q, dk) * (dk ** -0.5)
    kc = chunks(k, dk)
    vc = chunks(v, dv)
    bc = jnp.cumsum(chunks(log_a, dk), axis=3)
    causal = jnp.tril(jnp.ones((c, c), dtype=bool))

    def step(state, inp):
        qi, ki, vi, bi = inp
        diff = bi[:, :, :, None, :] - bi[:, :, None, :, :]
        decay = jnp.exp(jnp.where(causal[:, :, None], diff, -jnp.inf))
        attn = jnp.einsum('bhic,bhjc,bhijc->bhij', qi, ki, decay)
        b_last = bi[:, :, -1:, :]
        o = attn @ vi + jnp.einsum('bhic,bhcv->bhiv', qi * jnp.exp(bi), state)
        state = jnp.exp(b_last[:, :, 0, :, None]) * state + jnp.einsum('bhjc,bhjv->bhcv', ki * jnp.exp(b_last - bi), vi)
        return state, o

    state0 = jnp.zeros((bsz, nh, dk, dv), jnp.float32)
    _, o = lax.scan(step, state0, (qc, kc, vc, bc))
    o = o.transpose(1, 0, 3, 2, 4).reshape(bsz, seq, nh, dv)
    o = o * lax.rsqrt(jnp.mean(jnp.square(o), axis=-1, keepdims=True) + LN_EPS)
    o = o.astype(dtype) * norm_g * jax.nn.silu(r).reshape(bsz, seq, nh, dv)
    return o.reshape(bsz, seq, nh * dv)


def compress_blocks(t, pe, w1, w2):
    bsz, seq, g, hd = t.shape
    n_cmp = (seq - NSA_CMP_LEN) // NSA_CMP_STRIDE + 1
    idx = jnp.arange(n_cmp)[:, None] * NSA_CMP_STRIDE + jnp.arange(NSA_CMP_LEN)[None, :]
    blocks = t[:, idx] + pe[:, None, :]
    flat = blocks.transpose(0, 1, 3, 2, 4).reshape(bsz, n_cmp, g, NSA_CMP_LEN * hd)
    return jax.nn.gelu(flat @ w1) @ w2


def nsa_selected_attention(q, k, v, sel_idx):
    bsz, seq, g, r, hd = q.shape
    n_blk = seq // NSA_SEL_LEN
    n_sel = sel_idx.shape[-1]
    kb = k.reshape(bsz, n_blk, NSA_SEL_LEN, g, hd).transpose(0, 3, 1, 2, 4)
    vb = v.reshape(bsz, n_blk, NSA_SEL_LEN, g, hd).transpose(0, 3, 1, 2, 4)
    nqb = seq // NSA_Q_BLOCK
    qx = jnp.moveaxis(q.reshape(bsz, nqb, NSA_Q_BLOCK, g, r, hd), 1, 0)
    ix = jnp.moveaxis(sel_idx.reshape(bsz, g, nqb, NSA_Q_BLOCK, n_sel), 2, 0)
    tp = jnp.arange(seq).reshape(nqb, NSA_Q_BLOCK)
    gather = jax.vmap(jax.vmap(lambda blocks, idx: blocks[idx]))
    offs = jnp.arange(NSA_SEL_LEN)
    scale = hd ** -0.5

    def one_block(args):
        qb, ib, tb = args
        kg = gather(kb, ib)
        vg = gather(vb, ib)
        s = jnp.einsum('bqgrd,bgqnld->bgqrnl', qb, kg).astype(jnp.float32) * scale
        kpos = ib[..., None] * NSA_SEL_LEN + offs
        valid = kpos <= tb[None, None, :, None, None]
        s = jnp.where(valid[:, :, :, None], s, -jnp.inf)
        p = jax.nn.softmax(s.reshape(bsz, g, NSA_Q_BLOCK, r, n_sel * NSA_SEL_LEN), axis=-1)
        p = p.reshape(bsz, g, NSA_Q_BLOCK, r, n_sel, NSA_SEL_LEN)
        return jnp.einsum('bgqrnl,bgqnld->bqgrd', p.astype(vg.dtype), vg)

    o = lax.map(one_block, (qx, ix, tp))
    return jnp.moveaxis(o, 0, 1).reshape(bsz, seq, g, r, hd)


def nsa_mixer(q, k_cmp_in, v_cmp_in, k_sel, v_sel, k_win, v_win, gate_logits,
              pe_k, w1_k, w2_k, pe_v, w1_v, w2_v):
    bsz, seq, g, r, hd = q.shape
    dtype = q.dtype
    pos = jnp.arange(seq)
    scale = hd ** -0.5
    kc = compress_blocks(k_cmp_in, pe_k, w1_k, w2_k)
    vc = compress_blocks(v_cmp_in, pe_v, w1_v, w2_v)
    n_cmp = kc.shape[1]
    cmp_start = jnp.arange(n_cmp) * NSA_CMP_STRIDE
    cvalid = (cmp_start + NSA_CMP_LEN - 1)[None, :] <= pos[:, None]
    s = jnp.einsum('bsgrd,bcgd->bgrsc', q, kc).astype(jnp.float32) * scale
    s = jnp.where(cvalid, s, -jnp.inf)
    m = jnp.max(s, axis=-1, keepdims=True)
    m = jnp.where(jnp.isfinite(m), m, 0.0)
    p = jnp.where(cvalid, jnp.exp(s - m), 0.0)
    p = p / jnp.maximum(jnp.sum(p, axis=-1, keepdims=True), 1e-30)
    o_cmp = jnp.einsum('bgrsc,bcgd->bsgrd', p.astype(dtype), vc)
    n_blk = seq // NSA_SEL_LEN
    sel_start = jnp.arange(n_blk) * NSA_SEL_LEN
    overlap = (cmp_start[:, None] < sel_start[None, :] + NSA_SEL_LEN) & (cmp_start[:, None] + NSA_CMP_LEN > sel_start[None, :])
    imp = jnp.einsum('bgrsc,cj->bgsj', p, overlap.astype(jnp.float32))
    q_blk = pos // NSA_SEL_LEN
    j = jnp.arange(n_blk)[None, :]
    forced = (j == 0) | (j == q_blk[:, None]) | (j == q_blk[:, None] - 1)
    future = j > q_blk[:, None]
    imp = jnp.where(forced, jnp.inf, jnp.where(future, -jnp.inf, imp))
    n_sel = min(NSA_N_SELECT, n_blk)
    _, sel_idx = lax.top_k(imp, n_sel)
    o_sel = nsa_selected_attention(q, k_sel, v_sel, sel_idx)
    o_win, _ = banded_attention(q, k_win, v_win, NSA_WINDOW - 1)
    gates = jax.nn.sigmoid(gate_logits).reshape(bsz, seq, g, r, NSA_BRANCHES)
    o = gates[..., 0:1] * o_cmp + gates[..., 1:2] * o_sel + gates[..., 2:3] * o_win
    return o.reshape(bsz, seq, g * r * hd)


def token_mixer(h, w_in, b_in, w_alpha2, b_alpha, gla_norm_g, cmp_pe_k, cmp_w1_k, cmp_w2_k,
                cmp_pe_v, cmp_w1_v, cmp_w2_v, w_br_a, w_br_b, w_br_c, w_o_mix):
    bsz, seq, d = h.shape
    u = h @ w_in + b_in
    split_points = np.cumsum(IN_WIDTHS)[:-1].tolist()
    (a_q, a_k, a_v, b_q, b_k, b_v, b_r, b_lr, c_q, c_kc, c_vc, c_ks, c_vs, c_kw, c_vw, c_g, m_g) = jnp.split(u, split_points, axis=-1)
    n_grp = len(DIL_GROUPS)

    def a_heads(t):
        return t.reshape(bsz, seq, n_grp, DIL_HEADS, HEAD_DIM)

    y_a = dilated_attention(a_heads(a_q), a_heads(a_k), a_heads(a_v)).reshape(bsz, seq, DIL_HEADS * HEAD_DIM)
    y_b = gla_mixer(b_q, b_k, b_v, b_r, b_lr, w_alpha2, b_alpha, gla_norm_g)

    def c_kv(t):
        return t.reshape(bsz, seq, NSA_KV_GROUPS, HEAD_DIM)

    y_c = nsa_mixer(c_q.reshape(bsz, seq, NSA_KV_GROUPS, NSA_HEADS // NSA_KV_GROUPS, HEAD_DIM),
                    c_kv(c_kc), c_kv(c_vc), c_kv(c_ks), c_kv(c_vs), c_kv(c_kw), c_kv(c_vw), c_g,
                    cmp_pe_k, cmp_w1_k, cmp_w2_k, cmp_pe_v, cmp_w1_v, cmp_w2_v)
    g = jax.nn.sigmoid(m_g).reshape(bsz, seq, N_BRANCHES, d)
    merged = g[:, :, 0] * (y_a @ w_br_a) + g[:, :, 1] * (y_b @ w_br_b) + g[:, :, 2] * (y_c @ w_br_c)
    return merged @ w_o_mix


def memory_cross_attention(x, mem, w_xq, w_xk, w_xv, w_xo):
    bsz, seq, d = x.shape
    hd = d // XATTN_HEADS
    q = (x @ w_xq).reshape(bsz, seq, XATTN_HEADS, hd)
    k = (mem @ w_xk).reshape(bsz, mem.shape[1], XATTN_HEADS, hd)
    v = (mem @ w_xv).reshape(bsz, mem.shape[1], XATTN_HEADS, hd)
    s = jnp.einsum('bshd,bmhd->bhsm', q, k).astype(jnp.float32) * (hd ** -0.5)
    p = jax.nn.softmax(s, axis=-1).astype(x.dtype)
    o = jnp.einsum('bhsm,bmhd->bshd', p, v).reshape(bsz, seq, d)
    return o @ w_xo


def clamped_swiglu(gate, up):
    gate = jnp.minimum(gate, SWIGLU_LIMIT)
    up = jnp.clip(up, -SWIGLU_LIMIT, SWIGLU_LIMIT)
    return (up + 1.0) * gate * jax.nn.sigmoid(SWIGLU_ALPHA * gate)


def moe_ffn(h, w_router, b_router, w_gu, b_gu, w_down, b_down):
    bsz, seq, d = h.shape
    t = h.reshape(-1, d)
    n_assign = t.shape[0] * TOP_K
    logits = t @ w_router + b_router
    top_logits, top_idx = lax.top_k(logits, TOP_K)
    gates = jax.nn.softmax(top_logits.astype(jnp.float32), axis=-1).astype(t.dtype)
    flat_e = top_idx.reshape(-1)
    order = jnp.argsort(flat_e)
    sorted_e = flat_e[order]
    tok_of = order // TOP_K
    counts = jnp.bincount(flat_e, length=N_EXPERTS)
    padded = (counts + MOE_BLOCK - 1) // MOE_BLOCK * MOE_BLOCK
    start = jnp.cumsum(counts) - counts
    pend = jnp.cumsum(padded)
    pstart = pend - padded
    dest = pstart[sorted_e] + (jnp.arange(n_assign) - start[sorted_e])
    n_rows = n_assign + N_EXPERTS * MOE_BLOCK
    n_blocks = n_rows // MOE_BLOCK
    buf = jnp.zeros((n_rows, d), t.dtype).at[dest].set(t[tok_of])
    block_e = jnp.minimum(jnp.searchsorted(pend, jnp.arange(n_blocks) * MOE_BLOCK, side='right'), N_EXPERTS - 1)

    def expert_block(args):
        xb, e = args
        gu = xb @ w_gu[e] + b_gu[e]
        return clamped_swiglu(gu[:, :D_FF], gu[:, D_FF:]) @ w_down[e] + b_down[e]

    out = lax.map(expert_block, (buf.reshape(n_blocks, MOE_BLOCK, d), block_e))
    y_assign = out.reshape(n_rows, d)[dest] * gates.reshape(-1)[order][:, None]
    y = jnp.zeros_like(t).at[tok_of].add(y_assign)
    return y.reshape(bsz, seq, d)


def setup_inputs(seed: int = 0) -> dict:
    key = jax.random.key(seed)
    keys = iter(jax.random.split(key, 40))

    def nrm(shape, scale):
        return jax.random.normal(next(keys), shape, jnp.float32) * scale

    dl, d, hd = DEPTH, D_MODEL, HEAD_DIM
    beta = DEEPNORM_BETA
    return {
        'x': nrm((BATCH, SEQ, d), 1.0),
        'mem': nrm((BATCH, MEM_LEN, d), 1.0),
        'ln0_g': 1.0 + nrm((d,), 0.02),
        'ln0_b': nrm((d,), 0.02),
        'w_in': nrm((dl, d, N_IN), d ** -0.5),
        'b_in': nrm((dl, N_IN), 0.02),
        'w_alpha2': nrm((dl, GLA_RANK, B_K), GLA_RANK ** -0.5),
        'b_alpha': nrm((dl, B_K), 0.1),
        'gla_norm_g': 1.0 + nrm((dl, GLA_DV), 0.02),
        'cmp_pe_k': nrm((dl, NSA_CMP_LEN, hd), 0.02),
        'cmp_w1_k': nrm((dl, NSA_CMP_LEN * hd, NSA_CMP_HIDDEN), (NSA_CMP_LEN * hd) ** -0.5),
        'cmp_w2_k': nrm((dl, NSA_CMP_HIDDEN, hd), NSA_CMP_HIDDEN ** -0.5),
        'cmp_pe_v': nrm((dl, NSA_CMP_LEN, hd), 0.02),
        'cmp_w1_v': nrm((dl, NSA_CMP_LEN * hd, NSA_CMP_HIDDEN), (NSA_CMP_LEN * hd) ** -0.5),
        'cmp_w2_v': nrm((dl, NSA_CMP_HIDDEN, hd), NSA_CMP_HIDDEN ** -0.5),
        'w_br_a': nrm((dl, DIL_HEADS * hd, d), (DIL_HEADS * hd) ** -0.5),
        'w_br_b': nrm((dl, B_V, d), B_V ** -0.5),
        'w_br_c': nrm((dl, C_Q, d), C_Q ** -0.5),
        'w_o_mix': nrm((dl, d, d), beta * d ** -0.5),
        'ln1_g': 1.0 + nrm((dl, d), 0.02),
        'ln1_b': nrm((dl, d), 0.02),
        'w_xq': nrm((dl, d, d), d ** -0.5),
        'w_xk': nrm((dl, d, d), d ** -0.5),
        'w_xv': nrm((dl, d, d), d ** -0.5),
        'w_xo': nrm((dl, d, d), beta * d ** -0.5),
        'ln2_g': 1.0 + nrm((dl, d), 0.02),
        'ln2_b': nrm((dl, d), 0.02),
        'w_router': nrm((dl, d, N_EXPERTS), d ** -0.5),
        'b_router': nrm((dl, N_EXPERTS), 0.01),
        'w_gu': nrm((dl, N_EXPERTS, d, 2 * D_FF), d ** -0.5),
        'b_gu': nrm((dl, N_EXPERTS, 2 * D_FF), 0.02),
        'w_down': nrm((dl, N_EXPERTS, D_FF, d), beta * D_FF ** -0.5),
        'b_down': nrm((dl, N_EXPERTS, d), 0.02),
        'ln3_g': 1.0 + nrm((dl, d), 0.02),
        'ln3_b': nrm((dl, d), 0.02),
    }


def reference(x, mem, ln0_g, ln0_b, w_in, b_in, w_alpha2, b_alpha, gla_norm_g,
              cmp_pe_k, cmp_w1_k, cmp_w2_k, cmp_pe_v, cmp_w1_v, cmp_w2_v,
              w_br_a, w_br_b, w_br_c, w_o_mix, ln1_g, ln1_b,
              w_xq, w_xk, w_xv, w_xo, ln2_g, ln2_b,
              w_router, b_router, w_gu, b_gu, w_down, b_down, ln3_g, ln3_b):
    alpha = DEEPNORM_ALPHA
    x = layer_norm(x, ln0_g, ln0_b)
    for li in range(DEPTH):
        mix = token_mixer(x, w_in[li], b_in[li], w_alpha2[li], b_alpha[li], gla_norm_g[li],
                          cmp_pe_k[li], cmp_w1_k[li], cmp_w2_k[li], cmp_pe_v[li], cmp_w1_v[li], cmp_w2_v[li],
                          w_br_a[li], w_br_b[li], w_br_c[li], w_o_mix[li])
        x = layer_norm(alpha * x + mix, ln1_g[li], ln1_b[li])
        xa = memory_cross_attention(x, mem, w_xq[li], w_xk[li], w_xv[li], w_xo[li])
        x = layer_norm(alpha * x + xa, ln2_g[li], ln2_b[li])
        ff = moe_ffn(x, w_router[li], b_router[li], w_gu[li], b_gu[li], w_down[li], b_down[li])
        x = layer_norm(alpha * x + ff, ln3_g[li], ln3_b[li])
    return x
```

```python
import functools

import jax
import jax.numpy as jnp
import numpy as np
from jax import lax
from jax.experimental import pallas as pl
from jax.experimental.pallas import tpu as pltpu

D_MODEL = 1024
DEPTH = 2
HEAD_DIM = 64
N_BRANCHES = 3
LN_EPS = 1e-5
BAND_BLOCK = 128

DIL_GROUPS = ((128, 1), (512, 4), (2048, 16))
DIL_HEADS = 4

GLA_HEADS = 4
GLA_DK = D_MODEL // 2 // GLA_HEADS
GLA_DV = D_MODEL // GLA_HEADS
GLA_RANK = 16
GLA_TAU = 16.0
GLA_CHUNK = 64

NSA_HEADS = 16
NSA_KV_GROUPS = 4
NSA_BRANCHES = 3
NSA_CMP_LEN = 32
NSA_CMP_STRIDE = 16
NSA_SEL_LEN = 64
NSA_N_SELECT = 16
NSA_WINDOW = 512
NSA_Q_BLOCK = 32

XATTN_HEADS = 4

N_EXPERTS = 32
TOP_K = 4
D_FF = D_MODEL
SWIGLU_LIMIT = 7.0
SWIGLU_ALPHA = 1.702
MOE_BLOCK = 128

DEEPNORM_ALPHA = (2 * DEPTH) ** 0.25

A_W = len(DIL_GROUPS) * DIL_HEADS * HEAD_DIM
B_K = GLA_HEADS * GLA_DK
B_V = GLA_HEADS * GLA_DV
C_Q = NSA_HEADS * HEAD_DIM
C_KV = NSA_KV_GROUPS * HEAD_DIM
IN_WIDTHS = (A_W, A_W, A_W,
             B_K, B_K, B_V, B_V, GLA_RANK,
             C_Q, C_KV, C_KV, C_KV, C_KV, C_KV, C_KV, NSA_HEADS * NSA_BRANCHES,
             N_BRANCHES * D_MODEL)

V7X_VMEM_LIMIT_BYTES = 48 * 1024 * 1024
V7X_LANES = 128


def _mm_kernel(x_ref, w_ref, b_ref, o_ref):
    acc = jnp.dot(x_ref[...].astype(jnp.bfloat16), w_ref[...].astype(jnp.bfloat16),
                  preferred_element_type=jnp.float32)
    o_ref[...] = (acc + b_ref[...]).astype(o_ref.dtype)


def _pick_tile(n, candidates):
    for c in candidates:
        if n % c == 0:
            return c
    return n


def _mm(x, w, b=None, out_dtype=jnp.float32):
    m, k = x.shape
    n = w.shape[1]
    if b is None:
        b = jnp.zeros((n,), jnp.float32)
    tm = _pick_tile(m, (1024, 512, 256, 128, 64, 32, 16, 8))
    tn = _pick_tile(n, (768, 512, 384, 256, 128))
    return pl.pallas_call(
        _mm_kernel,
        out_shape=jax.ShapeDtypeStruct((m, n), out_dtype),
        grid=(n // tn, m // tm),
        in_specs=[
            pl.BlockSpec((tm, k), lambda j, i: (i, 0)),
            pl.BlockSpec((k, tn), lambda j, i: (0, j)),
            pl.BlockSpec((1, tn), lambda j, i: (0, j)),
        ],
        out_specs=pl.BlockSpec((tm, tn), lambda j, i: (i, j)),
        compiler_params=pltpu.CompilerParams(
            dimension_semantics=("arbitrary", "arbitrary"),
            vmem_limit_bytes=V7X_VMEM_LIMIT_BYTES),
        name="dense_proj",
    )(x, w, b.reshape(1, n))


def _mm3(x, w, b=None):
    lead = x.shape[:-1]
    return _mm(x.reshape(-1, x.shape[-1]), w, b).reshape(*lead, w.shape[1])


def layer_norm(x, g, b):
    xf = x.astype(jnp.float32)
    mu = jnp.mean(xf, axis=-1, keepdims=True)
    var = jnp.mean(jnp.square(xf - mu), axis=-1, keepdims=True)
    return ((xf - mu) * lax.rsqrt(var + LN_EPS)).astype(x.dtype) * g + b


def banded_attention(q, k, v, max_dist):
    n, seq_len, g, r, hd = q.shape
    blk = BAND_BLOCK
    n_prev = -(-max_dist // blk)
    nb = -(-seq_len // blk)
    pad = nb * blk - seq_len
    qp = jnp.pad(q, ((0, 0), (0, pad), (0, 0), (0, 0), (0, 0)))
    kv_pad = ((0, 0), (n_prev * blk, pad), (0, 0), (0, 0))
    kp = jnp.pad(k, kv_pad).reshape(n, nb + n_prev, blk, g, hd)
    vp = jnp.pad(v, kv_pad).reshape(n, nb + n_prev, blk, g, hd)
    kw = jnp.concatenate([kp[:, i:i + nb] for i in range(n_prev + 1)], axis=2)
    vw = jnp.concatenate([vp[:, i:i + nb] for i in range(n_prev + 1)], axis=2)
    width = (n_prev + 1) * blk
    qi = jnp.arange(blk)[:, None]
    kj = jnp.arange(width)[None, :]
    dist = qi + n_prev * blk - kj
    key_abs = jnp.arange(nb)[:, None, None] * blk - n_prev * blk + kj[None]
    mask = ((dist >= 0) & (dist <= max_dist))[None] & (key_abs >= 0)
    scale = hd ** -0.5

    def one_block(args):
        qb, kb, vb, mb = args
        s = jnp.einsum('nqgrd,nkgd->ngrqk', qb, kb).astype(jnp.float32) * scale
        s = jnp.where(mb, s, -jnp.inf)
        m = jnp.max(s, axis=-1, keepdims=True)
        p = jnp.exp(s - m)
        den = jnp.sum(p, axis=-1, keepdims=True)
        o = jnp.einsum('ngrqk,nkgd->nqgrd', (p / den).astype(vb.dtype), vb)
        lse = (m + jnp.log(den))[..., 0]
        return o, jnp.transpose(lse, (0, 3, 1, 2))

    qb = jnp.moveaxis(qp.reshape(n, nb, blk, g, r, hd), 1, 0)
    o, lse = lax.map(one_block, (qb, jnp.moveaxis(kw, 1, 0), jnp.moveaxis(vw, 1, 0), mask))
    o = jnp.moveaxis(o, 0, 1).reshape(n, nb * blk, g, r, hd)[:, :seq_len]
    lse = jnp.moveaxis(lse, 0, 1).reshape(n, nb * blk, g, r)[:, :seq_len]
    return o, lse


def dilated_attention(q, k, v):
    bsz, seq, _, nh, hd = q.shape
    outs, lses = [], []
    for gi, (window, dil) in enumerate(DIL_GROUPS):
        sub_len = seq // dil

        def to_sub(t):
            return t.reshape(bsz, sub_len, dil, nh, hd).transpose(0, 2, 1, 3, 4).reshape(bsz * dil, sub_len, nh, hd)

        o, lse = banded_attention(to_sub(q[:, :, gi])[:, :, :, None], to_sub(k[:, :, gi]), to_sub(v[:, :, gi]), window // dil)
        outs.append(o[:, :, :, 0].reshape(bsz, dil, sub_len, nh, hd).transpose(0, 2, 1, 3, 4).reshape(bsz, seq, nh, hd))
        lses.append(lse[..., 0].reshape(bsz, dil, sub_len, nh).transpose(0, 2, 1, 3).reshape(bsz, seq, nh))
    w = jax.nn.softmax(jnp.stack(lses), axis=0)
    return jnp.einsum('gbsh,gbshd->bshd', w.astype(q.dtype), jnp.stack(outs))


def gla_mixer(q, k, v, r, lr, w_alpha2, b_alpha, norm_g):
    bsz, seq, _ = q.shape
    nh, dk, dv, c = GLA_HEADS, GLA_DK, GLA_DV, GLA_CHUNK
    dtype = q.dtype
    log_a = jax.nn.log_sigmoid((lr @ w_alpha2 + b_alpha).astype(jnp.float32)) / GLA_TAU
    n_chunks = seq // c

    def chunks(t, width):
        return t.reshape(bsz, n_chunks, c, nh, width).transpose(1, 0, 3, 2, 4).astype(jnp.float32)

    qc = chunks(q, dk) * (dk ** -0.5)
    kc = chunks(k, dk)
    vc = chunks(v, dv)
    bc = jnp.cumsum(chunks(log_a, dk), axis=3)
    causal = jnp.tril(jnp.ones((c, c), dtype=bool))

    def step(state, inp):
        qi, ki, vi, bi = inp
        diff = bi[:, :, :, None, :] - bi[:, :, None, :, :]
        decay = jnp.exp(jnp.where(causal[:, :, None], diff, -jnp.inf))
        attn = jnp.einsum('bhic,bhjc,bhijc->bhij', qi, ki, decay)
        b_last = bi[:, :, -1:, :]
        o = attn @ vi + jnp.einsum('bhic,bhcv->bhiv', qi * jnp.exp(bi), state)
        state = jnp.exp(b_last[:, :, 0, :, None]) * state + jnp.einsum('bhjc,bhjv->bhcv', ki * jnp.exp(b_last - bi), vi)
        return state, o

    state0 = jnp.zeros((bsz, nh, dk, dv), jnp.float32)
    _, o = lax.scan(step, state0, (qc, kc, vc, bc))
    o = o.transpose(1, 0, 3, 2, 4).reshape(bsz, seq, nh, dv)
    o = o * lax.rsqrt(jnp.mean(jnp.square(o), axis=-1, keepdims=True) + LN_EPS)
    o = o.astype(dtype) * norm_g * jax.nn.silu(r).reshape(bsz, seq, nh, dv)
    return o.reshape(bsz, seq, nh * dv)


def compress_blocks(t, pe, w1, w2):
    bsz, seq, g, hd = t.shape
    n_cmp = (seq - NSA_CMP_LEN) // NSA_CMP_STRIDE + 1
    idx = jnp.arange(n_cmp)[:, None] * NSA_CMP_STRIDE + jnp.arange(NSA_CMP_LEN)[None, :]
    blocks = t[:, idx] + pe[:, None, :]
    flat = blocks.transpose(0, 1, 3, 2, 4).reshape(bsz, n_cmp, g, NSA_CMP_LEN * hd)
    return jax.nn.gelu(flat @ w1) @ w2


def nsa_selected_attention(q, k, v, sel_idx):
    bsz, seq, g, r, hd = q.shape
    n_blk = seq // NSA_SEL_LEN
    n_sel = sel_idx.shape[-1]
    kb = k.reshape(bsz, n_blk, NSA_SEL_LEN, g, hd).transpose(0, 3, 1, 2, 4)
    vb = v.reshape(bsz, n_blk, NSA_SEL_LEN, g, hd).transpose(0, 3, 1, 2, 4)
    nqb = seq // NSA_Q_BLOCK
    qx = jnp.moveaxis(q.reshape(bsz, nqb, NSA_Q_BLOCK, g, r, hd), 1, 0)
    ix = jnp.moveaxis(sel_idx.reshape(bsz, g, nqb, NSA_Q_BLOCK, n_sel), 2, 0)
    tp = jnp.arange(seq).reshape(nqb, NSA_Q_BLOCK)
    gather = jax.vmap(jax.vmap(lambda blocks, idx: blocks[idx]))
    offs = jnp.arange(NSA_SEL_LEN)
    scale = hd ** -0.5

    def one_block(args):
        qb, ib, tb = args
        kg = gather(kb, ib)
        vg = gather(vb, ib)
        s = jnp.einsum('bqgrd,bgqnld->bgqrnl', qb, kg).astype(jnp.float32) * scale
        kpos = ib[..., None] * NSA_SEL_LEN + offs
        valid = kpos <= tb[None, None, :, None, None]
        s = jnp.where(valid[:, :, :, None], s, -jnp.inf)
        p = jax.nn.softmax(s.reshape(bsz, g, NSA_Q_BLOCK, r, n_sel * NSA_SEL_LEN), axis=-1)
        p = p.reshape(bsz, g, NSA_Q_BLOCK, r, n_sel, NSA_SEL_LEN)
        return jnp.einsum('bgqrnl,bgqnld->bqgrd', p.astype(vg.dtype), vg)

    o = lax.map(one_block, (qx, ix, tp))
    return jnp.moveaxis(o, 0, 1).reshape(bsz, seq, g, r, hd)


def nsa_mixer(q, k_cmp_in, v_cmp_in, k_sel, v_sel, k_win, v_win, gate_logits,
              pe_k, w1_k, w2_k, pe_v, w1_v, w2_v):
    bsz, seq, g, r, hd = q.shape
    dtype = q.dtype
    pos = jnp.arange(seq)
    scale = hd ** -0.5
    kc = compress_blocks(k_cmp_in, pe_k, w1_k, w2_k)
    vc = compress_blocks(v_cmp_in, pe_v, w1_v, w2_v)
    n_cmp = kc.shape[1]
    cmp_start = jnp.arange(n_cmp) * NSA_CMP_STRIDE
    cvalid = (cmp_start + NSA_CMP_LEN - 1)[None, :] <= pos[:, None]
    s = jnp.einsum('bsgrd,bcgd->bgrsc', q, kc).astype(jnp.float32) * scale
    s = jnp.where(cvalid, s, -jnp.inf)
    m = jnp.max(s, axis=-1, keepdims=True)
    m = jnp.where(jnp.isfinite(m), m, 0.0)
    p = jnp.where(cvalid, jnp.exp(s - m), 0.0)
    p = p / jnp.maximum(jnp.sum(p, axis=-1, keepdims=True), 1e-30)
    o_cmp = jnp.einsum('bgrsc,bcgd->bsgrd', p.astype(dtype), vc)
    n_blk = seq // NSA_SEL_LEN
    sel_start = jnp.arange(n_blk) * NSA_SEL_LEN
    overlap = (cmp_start[:, None] < sel_start[None, :] + NSA_SEL_LEN) & (cmp_start[:, None] + NSA_CMP_LEN > sel_start[None, :])
    imp = jnp.einsum('bgrsc,cj->bgsj', p, overlap.astype(jnp.float32))
    q_blk = pos // NSA_SEL_LEN
    j = jnp.arange(n_blk)[None, :]
    forced = (j == 0) | (j == q_blk[:, None]) | (j == q_blk[:, None] - 1)
    future = j > q_blk[:, None]
    imp = jnp.where(forced, jnp.inf, jnp.where(future, -jnp.inf, imp))
    n_sel = min(NSA_N_SELECT, n_blk)
    _, sel_idx = lax.top_k(imp, n_sel)
    o_sel = nsa_selected_attention(q, k_sel, v_sel, sel_idx)
    o_win, _ = banded_attention(q, k_win, v_win, NSA_WINDOW - 1)
    gates = jax.nn.sigmoid(gate_logits).reshape(bsz, seq, g, r, NSA_BRANCHES)
    o = gates[..., 0:1] * o_cmp + gates[..., 1:2] * o_sel + gates[..., 2:3] * o_win
    return o.reshape(bsz, seq, g * r * hd)


def token_mixer(h, w_in, b_in, w_alpha2, b_alpha, gla_norm_g, cmp_pe_k, cmp_w1_k, cmp_w2_k,
                cmp_pe_v, cmp_w1_v, cmp_w2_v, w_br_a, w_br_b, w_br_c, w_o_mix):
    bsz, seq, d = h.shape
    offs = np.concatenate([[0], np.cumsum(IN_WIDTHS)]).tolist()
    h2 = h.reshape(-1, d)

    def proj(lo, hi):
        return _mm(h2, w_in[:, offs[lo]:offs[hi]], b_in[offs[lo]:offs[hi]]).reshape(bsz, seq, -1)

    u_ab = proj(0, 7)
    a_q, a_k, a_v, b_q, b_k, b_v, b_r = jnp.split(u_ab, np.cumsum(IN_WIDTHS[:7])[:-1].tolist(), axis=-1)
    w_small = jnp.concatenate([w_in[:, offs[7]:offs[8]], w_in[:, offs[15]:offs[16]]], axis=1)
    b_small = jnp.concatenate([b_in[offs[7]:offs[8]], b_in[offs[15]:offs[16]]])
    n_small = w_small.shape[1]
    w_small = jnp.pad(w_small, ((0, 0), (0, V7X_LANES - n_small)))
    b_small = jnp.pad(b_small, (0, V7X_LANES - n_small))
    u_small = _mm(h2, w_small, b_small).reshape(bsz, seq, -1)
    b_lr = u_small[..., :GLA_RANK]
    c_g = u_small[..., GLA_RANK:n_small]
    u_c = proj(8, 15)
    c_q, c_kc, c_vc, c_ks, c_vs, c_kw, c_vw = jnp.split(u_c, np.cumsum(IN_WIDTHS[8:15])[:-1].tolist(), axis=-1)
    m_g = proj(16, 17)
    n_grp = len(DIL_GROUPS)

    def a_heads(t):
        return t.reshape(bsz, seq, n_grp, DIL_HEADS, HEAD_DIM)

    y_a = dilated_attention(a_heads(a_q), a_heads(a_k), a_heads(a_v)).reshape(bsz, seq, DIL_HEADS * HEAD_DIM)
    y_b = gla_mixer(b_q, b_k, b_v, b_r, b_lr, w_alpha2, b_alpha, gla_norm_g)

    def c_kv(t):
        return t.reshape(bsz, seq, NSA_KV_GROUPS, HEAD_DIM)

    y_c = nsa_mixer(c_q.reshape(bsz, seq, NSA_KV_GROUPS, NSA_HEADS // NSA_KV_GROUPS, HEAD_DIM),
                    c_kv(c_kc), c_kv(c_vc), c_kv(c_ks), c_kv(c_vs), c_kv(c_kw), c_kv(c_vw), c_g,
                    cmp_pe_k, cmp_w1_k, cmp_w2_k, cmp_pe_v, cmp_w1_v, cmp_w2_v)
    g = jax.nn.sigmoid(m_g).reshape(bsz, seq, N_BRANCHES, d)
    merged = g[:, :, 0] * _mm3(y_a, w_br_a) + g[:, :, 1] * _mm3(y_b, w_br_b) + g[:, :, 2] * _mm3(y_c, w_br_c)
    return _mm3(merged, w_o_mix)


def memory_cross_attention(x, mem, w_xq, w_xk, w_xv, w_xo):
    bsz, seq, d = x.shape
    hd = d // XATTN_HEADS
    q = _mm3(x, w_xq).reshape(bsz, seq, XATTN_HEADS, hd)
    k = _mm3(mem, w_xk).reshape(bsz, mem.shape[1], XATTN_HEADS, hd)
    v = _mm3(mem, w_xv).reshape(bsz, mem.shape[1], XATTN_HEADS, hd)
    s = jnp.einsum('bshd,bmhd->bhsm', q, k).astype(jnp.float32) * (hd ** -0.5)
    p = jax.nn.softmax(s, axis=-1).astype(x.dtype)
    o = jnp.einsum('bhsm,bmhd->bshd', p, v).reshape(bsz, seq, d)
    return _mm3(o, w_xo)


def clamped_swiglu(gate, up):
    gate = jnp.minimum(gate, SWIGLU_LIMIT)
    up = jnp.clip(up, -SWIGLU_LIMIT, SWIGLU_LIMIT)
    return (up + 1.0) * gate * jax.nn.sigmoid(SWIGLU_ALPHA * gate)


def moe_ffn(h, w_router, b_router, w_gu, b_gu, w_down, b_down):
    bsz, seq, d = h.shape
    t = h.reshape(-1, d)
    n_assign = t.shape[0] * TOP_K
    logits = t @ w_router + b_router
    top_logits, top_idx = lax.top_k(logits, TOP_K)
    gates = jax.nn.softmax(top_logits.astype(jnp.float32), axis=-1).astype(t.dtype)
    flat_e = top_idx.reshape(-1)
    order = jnp.argsort(flat_e)
    sorted_e = flat_e[order]
    tok_of = order // TOP_K
    counts = jnp.bincount(flat_e, length=N_EXPERTS)
    padded = (counts + MOE_BLOCK - 1) // MOE_BLOCK * MOE_BLOCK
    start = jnp.cumsum(counts) - counts
    pend = jnp.cumsum(padded)
    pstart = pend - padded
    dest = pstart[sorted_e] + (jnp.arange(n_assign) - start[sorted_e])
    n_rows = n_assign + N_EXPERTS * MOE_BLOCK
    n_blocks = n_rows // MOE_BLOCK
    buf = jnp.zeros((n_rows, d), t.dtype).at[dest].set(t[tok_of])
    block_e = jnp.minimum(jnp.searchsorted(pend, jnp.arange(n_blocks) * MOE_BLOCK, side='right'), N_EXPERTS - 1)

    def expert_block(args):
        xb, e = args
        gu = xb @ w_gu[e] + b_gu[e]
        return clamped_swiglu(gu[:, :D_FF], gu[:, D_FF:]) @ w_down[e] + b_down[e]

    out = lax.map(expert_block, (buf.reshape(n_blocks, MOE_BLOCK, d), block_e))
    y_assign = out.reshape(n_rows, d)[dest] * gates.reshape(-1)[order][:, None]
    y = jnp.zeros_like(t).at[tok_of].add(y_assign)
    return y.reshape(bsz, seq, d)


def kernel(x, mem, ln0_g, ln0_b, w_in, b_in, w_alpha2, b_alpha, gla_norm_g, cmp_pe_k, cmp_w1_k, cmp_w2_k, cmp_pe_v, cmp_w1_v, cmp_w2_v, w_br_a, w_br_b, w_br_c, w_o_mix, ln1_g, ln1_b, w_xq, w_xk, w_xv, w_xo, ln2_g, ln2_b, w_router, b_router, w_gu, b_gu, w_down, b_down, ln3_g, ln3_b):
    alpha = DEEPNORM_ALPHA
    x = layer_norm(x, ln0_g, ln0_b)
    for li in range(DEPTH):
        mix = token_mixer(x, w_in[li], b_in[li], w_alpha2[li], b_alpha[li], gla_norm_g[li],
                          cmp_pe_k[li], cmp_w1_k[li], cmp_w2_k[li], cmp_pe_v[li], cmp_w1_v[li], cmp_w2_v[li],
                          w_br_a[li], w_br_b[li], w_br_c[li], w_o_mix[li])
        x = layer_norm(alpha * x + mix, ln1_g[li], ln1_b[li])
        xa = memory_cross_attention(x, mem, w_xq[li], w_xk[li], w_xv[li], w_xo[li])
        x = layer_norm(alpha * x + xa, ln2_g[li], ln2_b[li])
        ff = moe_ffn(x, w_router[li], b_router[li], w_gu[li], b_gu[li], w_down[li], b_down[li])
        x = layer_norm(alpha * x + ff, ln3_g[li], ln3_b[li])
    return x
```

```python
import functools

import jax
import jax.numpy as jnp
import numpy as np
from jax import lax
from jax.experimental import pallas as pl
from jax.experimental.pallas import tpu as pltpu

D_MODEL = 1024
DEPTH = 2
HEAD_DIM = 64
N_BRANCHES = 3
LN_EPS = 1e-5
BAND_BLOCK = 128

DIL_GROUPS = ((128, 1), (512, 4), (2048, 16))
DIL_HEADS = 4

GLA_HEADS = 4
GLA_DK = D_MODEL // 2 // GLA_HEADS
GLA_DV = D_MODEL // GLA_HEADS
GLA_RANK = 16
GLA_TAU = 16.0
GLA_CHUNK = 64

NSA_HEADS = 16
NSA_KV_GROUPS = 4
NSA_BRANCHES = 3
NSA_CMP_LEN = 32
NSA_CMP_STRIDE = 16
NSA_SEL_LEN = 64
NSA_N_SELECT = 16
NSA_WINDOW = 512
NSA_Q_BLOCK = 32

XATTN_HEADS = 4

N_EXPERTS = 32
TOP_K = 4
D_FF = D_MODEL
SWIGLU_LIMIT = 7.0
SWIGLU_ALPHA = 1.702
MOE_BLOCK = 128

DEEPNORM_ALPHA = (2 * DEPTH) ** 0.25

A_W = len(DIL_GROUPS) * DIL_HEADS * HEAD_DIM
B_K = GLA_HEADS * GLA_DK
B_V = GLA_HEADS * GLA_DV
C_Q = NSA_HEADS * HEAD_DIM
C_KV = NSA_KV_GROUPS * HEAD_DIM
IN_WIDTHS = (A_W, A_W, A_W,
             B_K, B_K, B_V, B_V, GLA_RANK,
             C_Q, C_KV, C_KV, C_KV, C_KV, C_KV, C_KV, NSA_HEADS * NSA_BRANCHES,
             N_BRANCHES * D_MODEL)

V7X_VMEM_LIMIT_BYTES = 48 * 1024 * 1024
V7X_LANES = 128


def _mm_kernel(x_ref, w_ref, b_ref, o_ref):
    acc = jnp.dot(x_ref[...].astype(jnp.bfloat16), w_ref[...].astype(jnp.bfloat16),
                  preferred_element_type=jnp.float32)
    o_ref[...] = (acc + b_ref[...]).astype(o_ref.dtype)


def _pick_tile(n, candidates):
    for c in candidates:
        if n % c == 0:
            return c
    return n


def _mm(x, w, b=None, out_dtype=jnp.float32):
    m, k = x.shape
    n = w.shape[1]
    if b is None:
        b = jnp.zeros((n,), jnp.float32)
    tm = _pick_tile(m, (1024, 512, 256, 128, 64, 32, 16, 8))
    tn = _pick_tile(n, (768, 512, 384, 256, 128))
    return pl.pallas_call(
        _mm_kernel,
        out_shape=jax.ShapeDtypeStruct((m, n), out_dtype),
        grid=(n // tn, m // tm),
        in_specs=[
            pl.BlockSpec((tm, k), lambda j, i: (i, 0)),
            pl.BlockSpec((k, tn), lambda j, i: (0, j)),
            pl.BlockSpec((1, tn), lambda j, i: (0, j)),
        ],
        out_specs=pl.BlockSpec((tm, tn), lambda j, i: (i, j)),
        compiler_params=pltpu.CompilerParams(
            dimension_semantics=("arbitrary", "arbitrary"),
            vmem_limit_bytes=V7X_VMEM_LIMIT_BYTES),
        name="dense_proj",
    )(x, w, b.reshape(1, n))


def _mm3(x, w, b=None):
    lead = x.shape[:-1]
    return _mm(x.reshape(-1, x.shape[-1]), w, b).reshape(*lead, w.shape[1])


NEG_BIG = -1e30
SEL_TQ = 256
SEL_TK = 256
HEADS_PER_STEP = 8


def _half_select(x, want_half, have_half):
    if want_half != have_half:
        x = pltpu.roll(x, HEAD_DIM, axis=1)
    return x


def _sel_attn_kernel(q_ref, k_ref, v_ref, m_ref, e_ref, o_ref, mx_ref, l_ref, acc_ref):
    i = pl.program_id(2)
    tq, tk = SEL_TQ, SEL_TK
    lane = lax.broadcasted_iota(jnp.int32, (tq, V7X_LANES), 1)
    q_pos = i * tq + lax.broadcasted_iota(jnp.int32, (tq, tk), 0)
    k_off = lax.broadcasted_iota(jnp.int32, (tq, tk), 1)
    outs = []
    for hh in range(HEADS_PER_STEP):
        grp = hh // 4
        qc = q_ref[:, (hh // 2) * V7X_LANES:(hh // 2 + 1) * V7X_LANES] * (HEAD_DIM ** -0.5)
        qc = _half_select(qc, grp, hh % 2)
        qh = jnp.where((lane >= grp * HEAD_DIM) & (lane < (grp + 1) * HEAD_DIM), qc, 0.0).astype(jnp.bfloat16)
        selm = m_ref[grp]
        mx_ref[...] = jnp.full(mx_ref.shape, NEG_BIG, jnp.float32)
        l_ref[...] = jnp.zeros(l_ref.shape, jnp.float32)
        acc_ref[...] = jnp.zeros(acc_ref.shape, jnp.float32)

        def body(j, carry):
            k0 = pl.multiple_of(j * tk, tk)
            kt = k_ref[pl.ds(k0, tk), :].astype(jnp.bfloat16)
            vt = v_ref[pl.ds(k0, tk), :].astype(jnp.bfloat16)
            s = lax.dot_general(qh, kt, (((1,), (1,)), ((), ())), preferred_element_type=jnp.float32)
            blk = jnp.dot(selm, e_ref[j], preferred_element_type=jnp.float32)
            valid = (blk > 0.5) & (k0 + k_off <= q_pos)
            s = jnp.where(valid, s, NEG_BIG)
            m_old = mx_ref[...]
            m_new = jnp.maximum(m_old, jnp.max(s, axis=-1, keepdims=True))
            p = jnp.where(valid, jnp.exp(s - m_new), 0.0)
            corr = jnp.exp(m_old - m_new)
            l_ref[...] = corr * l_ref[...] + jnp.sum(p, axis=-1, keepdims=True)
            acc_ref[...] = corr * acc_ref[...] + jnp.dot(p.astype(jnp.bfloat16), vt, preferred_element_type=jnp.float32)
            mx_ref[...] = m_new
            return carry

        lax.fori_loop(0, i + 1, body, 0)
        o = acc_ref[...] / l_ref[...]
        outs.append(_half_select(o, hh % 2, grp))
    for c in range(HEADS_PER_STEP // 2):
        o_ref[:, c * V7X_LANES:(c + 1) * V7X_LANES] = jnp.where(lane < HEAD_DIM, outs[2 * c], outs[2 * c + 1])


def _sel_expand_matrix(seq):
    n_blk = seq // NSA_SEL_LEN
    e = (np.arange(seq)[None, :] // NSA_SEL_LEN == np.arange(n_blk)[:, None]).astype(np.float32)
    return e.reshape(n_blk, seq // SEL_TK, SEL_TK).transpose(1, 0, 2)


def nsa_selected_attention_pallas(q, k, v, selmask):
    bsz, seq, qw = q.shape
    n_blk = selmask.shape[-1]
    n_pairs = k.shape[-1] // V7X_LANES
    expand = jnp.asarray(_sel_expand_matrix(seq), jnp.bfloat16)
    return pl.pallas_call(
        _sel_attn_kernel,
        out_shape=jax.ShapeDtypeStruct((bsz, seq, qw), jnp.float32),
        grid=(bsz, n_pairs, seq // SEL_TQ),
        in_specs=[
            pl.BlockSpec((None, SEL_TQ, HEADS_PER_STEP * HEAD_DIM), lambda b, gp, i: (b, i, gp)),
            pl.BlockSpec((None, seq, V7X_LANES), lambda b, gp, i: (b, 0, gp)),
            pl.BlockSpec((None, seq, V7X_LANES), lambda b, gp, i: (b, 0, gp)),
            pl.BlockSpec((None, 2, SEL_TQ, n_blk), lambda b, gp, i: (b, gp, i, 0)),
            pl.BlockSpec((seq // SEL_TK, n_blk, SEL_TK), lambda b, gp, i: (0, 0, 0)),
        ],
        out_specs=pl.BlockSpec((None, SEL_TQ, HEADS_PER_STEP * HEAD_DIM), lambda b, gp, i: (b, i, gp)),
        scratch_shapes=[pltpu.VMEM((SEL_TQ, 1), jnp.float32),
                        pltpu.VMEM((SEL_TQ, 1), jnp.float32),
                        pltpu.VMEM((SEL_TQ, V7X_LANES), jnp.float32)],
        compiler_params=pltpu.CompilerParams(
            dimension_semantics=("arbitrary", "arbitrary", "arbitrary"),
            vmem_limit_bytes=V7X_VMEM_LIMIT_BYTES),
        name="nsa_selected_attention",
    )(q, k, v, selmask.astype(jnp.bfloat16), expand)


def layer_norm(x, g, b):
    xf = x.astype(jnp.float32)
    mu = jnp.mean(xf, axis=-1, keepdims=True)
    var = jnp.mean(jnp.square(xf - mu), axis=-1, keepdims=True)
    return ((xf - mu) * lax.rsqrt(var + LN_EPS)).astype(x.dtype) * g + b


def banded_attention(q, k, v, max_dist):
    n, seq_len, g, r, hd = q.shape
    blk = BAND_BLOCK
    n_prev = -(-max_dist // blk)
    nb = -(-seq_len // blk)
    pad = nb * blk - seq_len
    qp = jnp.pad(q, ((0, 0), (0, pad), (0, 0), (0, 0), (0, 0)))
    kv_pad = ((0, 0), (n_prev * blk, pad), (0, 0), (0, 0))
    kp = jnp.pad(k, kv_pad).reshape(n, nb + n_prev, blk, g, hd)
    vp = jnp.pad(v, kv_pad).reshape(n, nb + n_prev, blk, g, hd)
    kw = jnp.concatenate([kp[:, i:i + nb] for i in range(n_prev + 1)], axis=2)
    vw = jnp.concatenate([vp[:, i:i + nb] for i in range(n_prev + 1)], axis=2)
    width = (n_prev + 1) * blk
    qi = jnp.arange(blk)[:, None]
    kj = jnp.arange(width)[None, :]
    dist = qi + n_prev * blk - kj
    key_abs = jnp.arange(nb)[:, None, None] * blk - n_prev * blk + kj[None]
    mask = ((dist >= 0) & (dist <= max_dist))[None] & (key_abs >= 0)
    scale = hd ** -0.5

    def one_block(args):
        qb, kb, vb, mb = args
        s = jnp.einsum('nqgrd,nkgd->ngrqk', qb, kb).astype(jnp.float32) * scale
        s = jnp.where(mb, s, -jnp.inf)
        m = jnp.max(s, axis=-1, keepdims=True)
        p = jnp.exp(s - m)
        den = jnp.sum(p, axis=-1, keepdims=True)
        o = jnp.einsum('ngrqk,nkgd->nqgrd', (p / den).astype(vb.dtype), vb)
        lse = (m + jnp.log(den))[..., 0]
        return o, jnp.transpose(lse, (0, 3, 1, 2))

    qb = jnp.moveaxis(qp.reshape(n, nb, blk, g, r, hd), 1, 0)
    o, lse = lax.map(one_block, (qb, jnp.moveaxis(kw, 1, 0), jnp.moveaxis(vw, 1, 0), mask))
    o = jnp.moveaxis(o, 0, 1).reshape(n, nb * blk, g, r, hd)[:, :seq_len]
    lse = jnp.moveaxis(lse, 0, 1).reshape(n, nb * blk, g, r)[:, :seq_len]
    return o, lse


def dilated_attention(q, k, v):
    bsz, seq, _, nh, hd = q.shape
    outs, lses = [], []
    for gi, (window, dil) in enumerate(DIL_GROUPS):
        sub_len = seq // dil

        def to_sub(t):
            return t.reshape(bsz, sub_len, dil, nh, hd).transpose(0, 2, 1, 3, 4).reshape(bsz * dil, sub_len, nh, hd)

        o, lse = banded_attention(to_sub(q[:, :, gi])[:, :, :, None], to_sub(k[:, :, gi]), to_sub(v[:, :, gi]), window // dil)
        outs.append(o[:, :, :, 0].reshape(bsz, dil, sub_len, nh, hd).transpose(0, 2, 1, 3, 4).reshape(bsz, seq, nh, hd))
        lses.append(lse[..., 0].reshape(bsz, dil, sub_len, nh).transpose(0, 2, 1, 3).reshape(bsz, seq, nh))
    w = jax.nn.softmax(jnp.stack(lses), axis=0)
    return jnp.einsum('gbsh,gbshd->bshd', w.astype(q.dtype), jnp.stack(outs))


def gla_mixer(q, k, v, r, lr, w_alpha2, b_alpha, norm_g):
    bsz, seq, _ = q.shape
    nh, dk, dv, c = GLA_HEADS, GLA_DK, GLA_DV, GLA_CHUNK
    dtype = q.dtype
    log_a = jax.nn.log_sigmoid((lr @ w_alpha2 + b_alpha).astype(jnp.float32)) / GLA_TAU
    n_chunks = seq // c

    def chunks(t, width):
        return t.reshape(bsz, n_chunks, c, nh, width).transpose(1, 0, 3, 2, 4).astype(jnp.float32)

    qc = chunks(q, dk) * (dk ** -0.5)
    kc = chunks(k, dk)
    vc = chunks(v, dv)
    bc = jnp.cumsum(chunks(log_a, dk), axis=3)
    causal = jnp.tril(jnp.ones((c, c), dtype=bool))

    def step(state, inp):
        qi, ki, vi, bi = inp
        diff = bi[:, :, :, None, :] - bi[:, :, None, :, :]
        decay = jnp.exp(jnp.where(causal[:, :, None], diff, -jnp.inf))
        attn = jnp.einsum('bhic,bhjc,bhijc->bhij', qi, ki, decay)
        b_last = bi[:, :, -1:, :]
        o = attn @ vi + jnp.einsum('bhic,bhcv->bhiv', qi * jnp.exp(bi), state)
        state = jnp.exp(b_last[:, :, 0, :, None]) * state + jnp.einsum('bhjc,bhjv->bhcv', ki * jnp.exp(b_last - bi), vi)
        return state, o

    state0 = jnp.zeros((bsz, nh, dk, dv), jnp.float32)
    _, o = lax.scan(step, state0, (qc, kc, vc, bc))
    o = o.transpose(1, 0, 3, 2, 4).reshape(bsz, seq, nh, dv)
    o = o * lax.rsqrt(jnp.mean(jnp.square(o), axis=-1, keepdims=True) + LN_EPS)
    o = o.astype(dtype) * norm_g * jax.nn.silu(r).reshape(bsz, seq, nh, dv)
    return o.reshape(bsz, seq, nh * dv)


def compress_blocks(t, pe, w1, w2):
    bsz, seq, g, hd = t.shape
    n_cmp = (seq - NSA_CMP_LEN) // NSA_CMP_STRIDE + 1
    idx = jnp.arange(n_cmp)[:, None] * NSA_CMP_STRIDE + jnp.arange(NSA_CMP_LEN)[None, :]
    blocks = t[:, idx] + pe[:, None, :]
    flat = blocks.transpose(0, 1, 3, 2, 4).reshape(bsz, n_cmp, g, NSA_CMP_LEN * hd)
    return jax.nn.gelu(flat @ w1) @ w2


def nsa_selected_attention(q, k, v, sel_idx):
    bsz, seq, g, r, hd = q.shape
    n_blk = seq // NSA_SEL_LEN
    n_sel = sel_idx.shape[-1]
    kb = k.reshape(bsz, n_blk, NSA_SEL_LEN, g, hd).transpose(0, 3, 1, 2, 4)
    vb = v.reshape(bsz, n_blk, NSA_SEL_LEN, g, hd).transpose(0, 3, 1, 2, 4)
    nqb = seq // NSA_Q_BLOCK
    qx = jnp.moveaxis(q.reshape(bsz, nqb, NSA_Q_BLOCK, g, r, hd), 1, 0)
    ix = jnp.moveaxis(sel_idx.reshape(bsz, g, nqb, NSA_Q_BLOCK, n_sel), 2, 0)
    tp = jnp.arange(seq).reshape(nqb, NSA_Q_BLOCK)
    gather = jax.vmap(jax.vmap(lambda blocks, idx: blocks[idx]))
    offs = jnp.arange(NSA_SEL_LEN)
    scale = hd ** -0.5

    def one_block(args):
        qb, ib, tb = args
        kg = gather(kb, ib)
        vg = gather(vb, ib)
        s = jnp.einsum('bqgrd,bgqnld->bgqrnl', qb, kg).astype(jnp.float32) * scale
        kpos = ib[..., None] * NSA_SEL_LEN + offs
        valid = kpos <= tb[None, None, :, None, None]
        s = jnp.where(valid[:, :, :, None], s, -jnp.inf)
        p = jax.nn.softmax(s.reshape(bsz, g, NSA_Q_BLOCK, r, n_sel * NSA_SEL_LEN), axis=-1)
        p = p.reshape(bsz, g, NSA_Q_BLOCK, r, n_sel, NSA_SEL_LEN)
        return jnp.einsum('bgqrnl,bgqnld->bqgrd', p.astype(vg.dtype), vg)

    o = lax.map(one_block, (qx, ix, tp))
    return jnp.moveaxis(o, 0, 1).reshape(bsz, seq, g, r, hd)


def nsa_mixer(q, k_cmp_in, v_cmp_in, k_sel, v_sel, k_win, v_win, gate_logits,
              pe_k, w1_k, w2_k, pe_v, w1_v, w2_v):
    bsz, seq, g, r, hd = q.shape
    dtype = q.dtype
    pos = jnp.arange(seq)
    scale = hd ** -0.5
    kc = compress_blocks(k_cmp_in, pe_k, w1_k, w2_k)
    vc = compress_blocks(v_cmp_in, pe_v, w1_v, w2_v)
    n_cmp = kc.shape[1]
    cmp_start = jnp.arange(n_cmp) * NSA_CMP_STRIDE
    cvalid = (cmp_start + NSA_CMP_LEN - 1)[None, :] <= pos[:, None]
    s = jnp.einsum('bsgrd,bcgd->bgrsc', q, kc).astype(jnp.float32) * scale
    s = jnp.where(cvalid, s, -jnp.inf)
    m = jnp.max(s, axis=-1, keepdims=True)
    m = jnp.where(jnp.isfinite(m), m, 0.0)
    p = jnp.where(cvalid, jnp.exp(s - m), 0.0)
    p = p / jnp.maximum(jnp.sum(p, axis=-1, keepdims=True), 1e-30)
    o_cmp = jnp.einsum('bgrsc,bcgd->bsgrd', p.astype(dtype), vc)
    n_blk = seq // NSA_SEL_LEN
    sel_start = jnp.arange(n_blk) * NSA_SEL_LEN
    overlap = (cmp_start[:, None] < sel_start[None, :] + NSA_SEL_LEN) & (cmp_start[:, None] + NSA_CMP_LEN > sel_start[None, :])
    imp = jnp.einsum('bgrsc,cj->bgsj', p, overlap.astype(jnp.float32))
    q_blk = pos // NSA_SEL_LEN
    j = jnp.arange(n_blk)[None, :]
    forced = (j == 0) | (j == q_blk[:, None]) | (j == q_blk[:, None] - 1)
    future = j > q_blk[:, None]
    imp = jnp.where(forced, jnp.inf, jnp.where(future, -jnp.inf, imp))
    n_sel = min(NSA_N_SELECT, n_blk)
    _, sel_idx = lax.top_k(imp, n_sel)
    selmask = (sel_idx[..., None] == jnp.arange(n_blk)).any(-2)
    o_sel = nsa_selected_attention_pallas(q.reshape(bsz, seq, -1), k_sel.reshape(bsz, seq, -1),
                                          v_sel.reshape(bsz, seq, -1), selmask).reshape(q.shape)
    o_win, _ = banded_attention(q, k_win, v_win, NSA_WINDOW - 1)
    gates = jax.nn.sigmoid(gate_logits).reshape(bsz, seq, g, r, NSA_BRANCHES)
    o = gates[..., 0:1] * o_cmp + gates[..., 1:2] * o_sel + gates[..., 2:3] * o_win
    return o.reshape(bsz, seq, g * r * hd)


def token_mixer(h, w_in, b_in, w_alpha2, b_alpha, gla_norm_g, cmp_pe_k, cmp_w1_k, cmp_w2_k,
                cmp_pe_v, cmp_w1_v, cmp_w2_v, w_br_a, w_br_b, w_br_c, w_o_mix):
    bsz, seq, d = h.shape
    offs = np.concatenate([[0], np.cumsum(IN_WIDTHS)]).tolist()
    h2 = h.reshape(-1, d)

    def proj(lo, hi):
        return _mm(h2, w_in[:, offs[lo]:offs[hi]], b_in[offs[lo]:offs[hi]]).reshape(bsz, seq, -1)

    u_ab = proj(0, 7)
    a_q, a_k, a_v, b_q, b_k, b_v, b_r = jnp.split(u_ab, np.cumsum(IN_WIDTHS[:7])[:-1].tolist(), axis=-1)
    w_small = jnp.concatenate([w_in[:, offs[7]:offs[8]], w_in[:, offs[15]:offs[16]]], axis=1)
    b_small = jnp.concatenate([b_in[offs[7]:offs[8]], b_in[offs[15]:offs[16]]])
    n_small = w_small.shape[1]
    w_small = jnp.pad(w_small, ((0, 0), (0, V7X_LANES - n_small)))
    b_small = jnp.pad(b_small, (0, V7X_LANES - n_small))
    u_small = _mm(h2, w_small, b_small).reshape(bsz, seq, -1)
    b_lr = u_small[..., :GLA_RANK]
    c_g = u_small[..., GLA_RANK:n_small]
    u_c = proj(8, 15)
    c_q, c_kc, c_vc, c_ks, c_vs, c_kw, c_vw = jnp.split(u_c, np.cumsum(IN_WIDTHS[8:15])[:-1].tolist(), axis=-1)
    m_g = proj(16, 17)
    n_grp = len(DIL_GROUPS)

    def a_heads(t):
        return t.reshape(bsz, seq, n_grp, DIL_HEADS, HEAD_DIM)

    y_a = dilated_attention(a_heads(a_q), a_heads(a_k), a_heads(a_v)).reshape(bsz, seq, DIL_HEADS * HEAD_DIM)
    y_b = gla_mixer(b_q, b_k, b_v, b_r, b_lr, w_alpha2, b_alpha, gla_norm_g)

    def c_kv(t):
        return t.reshape(bsz, seq, NSA_KV_GROUPS, HEAD_DIM)

    y_c = nsa_mixer(c_q.reshape(bsz, seq, NSA_KV_GROUPS, NSA_HEADS // NSA_KV_GROUPS, HEAD_DIM),
                    c_kv(c_kc), c_kv(c_vc), c_kv(c_ks), c_kv(c_vs), c_kv(c_kw), c_kv(c_vw), c_g,
                    cmp_pe_k, cmp_w1_k, cmp_w2_k, cmp_pe_v, cmp_w1_v, cmp_w2_v)
    g = jax.nn.sigmoid(m_g).reshape(bsz, seq, N_BRANCHES, d)
    merged = g[:, :, 0] * _mm3(y_a, w_br_a) + g[:, :, 1] * _mm3(y_b, w_br_b) + g[:, :, 2] * _mm3(y_c, w_br_c)
    return _mm3(merged, w_o_mix)


def memory_cross_attention(x, mem, w_xq, w_xk, w_xv, w_xo):
    bsz, seq, d = x.shape
    hd = d // XATTN_HEADS
    q = _mm3(x, w_xq).reshape(bsz, seq, XATTN_HEADS, hd)
    k = _mm3(mem, w_xk).reshape(bsz, mem.shape[1], XATTN_HEADS, hd)
    v = _mm3(mem, w_xv).reshape(bsz, mem.shape[1], XATTN_HEADS, hd)
    s = jnp.einsum('bshd,bmhd->bhsm', q, k).astype(jnp.float32) * (hd ** -0.5)
    p = jax.nn.softmax(s, axis=-1).astype(x.dtype)
    o = jnp.einsum('bhsm,bmhd->bshd', p, v).reshape(bsz, seq, d)
    return _mm3(o, w_xo)


def clamped_swiglu(gate, up):
    gate = jnp.minimum(gate, SWIGLU_LIMIT)
    up = jnp.clip(up, -SWIGLU_LIMIT, SWIGLU_LIMIT)
    return (up + 1.0) * gate * jax.nn.sigmoid(SWIGLU_ALPHA * gate)


def moe_ffn(h, w_router, b_router, w_gu, b_gu, w_down, b_down):
    bsz, seq, d = h.shape
    t = h.reshape(-1, d)
    n_assign = t.shape[0] * TOP_K
    logits = t @ w_router + b_router
    top_logits, top_idx = lax.top_k(logits, TOP_K)
    gates = jax.nn.softmax(top_logits.astype(jnp.float32), axis=-1).astype(t.dtype)
    flat_e = top_idx.reshape(-1)
    order = jnp.argsort(flat_e)
    sorted_e = flat_e[order]
    tok_of = order // TOP_K
    counts = jnp.bincount(flat_e, length=N_EXPERTS)
    padded = (counts + MOE_BLOCK - 1) // MOE_BLOCK * MOE_BLOCK
    start = jnp.cumsum(counts) - counts
    pend = jnp.cumsum(padded)
    pstart = pend - padded
    dest = pstart[sorted_e] + (jnp.arange(n_assign) - start[sorted_e])
    n_rows = n_assign + N_EXPERTS * MOE_BLOCK
    n_blocks = n_rows // MOE_BLOCK
    buf = jnp.zeros((n_rows, d), t.dtype).at[dest].set(t[tok_of])
    block_e = jnp.minimum(jnp.searchsorted(pend, jnp.arange(n_blocks) * MOE_BLOCK, side='right'), N_EXPERTS - 1)

    def expert_block(args):
        xb, e = args
        gu = xb @ w_gu[e] + b_gu[e]
        return clamped_swiglu(gu[:, :D_FF], gu[:, D_FF:]) @ w_down[e] + b_down[e]

    out = lax.map(expert_block, (buf.reshape(n_blocks, MOE_BLOCK, d), block_e))
    y_assign = out.reshape(n_rows, d)[dest] * gates.reshape(-1)[order][:, None]
    y = jnp.zeros_like(t).at[tok_of].add(y_assign)
    return y.reshape(bsz, seq, d)


def kernel(x, mem, ln0_g, ln0_b, w_in, b_in, w_alpha2, b_alpha, gla_norm_g, cmp_pe_k, cmp_w1_k, cmp_w2_k, cmp_pe_v, cmp_w1_v, cmp_w2_v, w_br_a, w_br_b, w_br_c, w_o_mix, ln1_g, ln1_b, w_xq, w_xk, w_xv, w_xo, ln2_g, ln2_b, w_router, b_router, w_gu, b_gu, w_down, b_down, ln3_g, ln3_b):
    alpha = DEEPNORM_ALPHA
    x = layer_norm(x, ln0_g, ln0_b)
    for li in range(DEPTH):
        mix = token_mixer(x, w_in[li], b_in[li], w_alpha2[li], b_alpha[li], gla_norm_g[li],
                          cmp_pe_k[li], cmp_w1_k[li], cmp_w2_k[li], cmp_pe_v[li], cmp_w1_v[li], cmp_w2_v[li],
                          w_br_a[li], w_br_b[li], w_br_c[li], w_o_mix[li])
        x = layer_norm(alpha * x + mix, ln1_g[li], ln1_b[li])
        xa = memory_cross_attention(x, mem, w_xq[li], w_xk[li], w_xv[li], w_xo[li])
        x = layer_norm(alpha * x + xa, ln2_g[li], ln2_b[li])
        ff = moe_ffn(x, w_router[li], b_router[li], w_gu[li], b_gu[li], w_down[li], b_down[li])
        x = layer_norm(alpha * x + ff, ln3_g[li], ln3_b[li])
    return x
```

```python
import functools

import jax
import jax.numpy as jnp
import numpy as np
from jax import lax
from jax.experimental import pallas as pl
from jax.experimental.pallas import tpu as pltpu

D_MODEL = 1024
DEPTH = 2
HEAD_DIM = 64
N_BRANCHES = 3
LN_EPS = 1e-5
BAND_BLOCK = 128

DIL_GROUPS = ((128, 1), (512, 4), (2048, 16))
DIL_HEADS = 4

GLA_HEADS = 4
GLA_DK = D_MODEL // 2 // GLA_HEADS
GLA_DV = D_MODEL // GLA_HEADS
GLA_RANK = 16
GLA_TAU = 16.0
GLA_CHUNK = 64

NSA_HEADS = 16
NSA_KV_GROUPS = 4
NSA_BRANCHES = 3
NSA_CMP_LEN = 32
NSA_CMP_STRIDE = 16
NSA_SEL_LEN = 64
NSA_N_SELECT = 16
NSA_WINDOW = 512
NSA_Q_BLOCK = 32

XATTN_HEADS = 4

N_EXPERTS = 32
TOP_K = 4
D_FF = D_MODEL
SWIGLU_LIMIT = 7.0
SWIGLU_ALPHA = 1.702
MOE_BLOCK = 128

DEEPNORM_ALPHA = (2 * DEPTH) ** 0.25

A_W = len(DIL_GROUPS) * DIL_HEADS * HEAD_DIM
B_K = GLA_HEADS * GLA_DK
B_V = GLA_HEADS * GLA_DV
C_Q = NSA_HEADS * HEAD_DIM
C_KV = NSA_KV_GROUPS * HEAD_DIM
IN_WIDTHS = (A_W, A_W, A_W,
             B_K, B_K, B_V, B_V, GLA_RANK,
             C_Q, C_KV, C_KV, C_KV, C_KV, C_KV, C_KV, NSA_HEADS * NSA_BRANCHES,
             N_BRANCHES * D_MODEL)

V7X_VMEM_LIMIT_BYTES = 48 * 1024 * 1024
V7X_LANES = 128


def _mm_kernel(x_ref, w_ref, b_ref, o_ref):
    acc = jnp.dot(x_ref[...].astype(jnp.bfloat16), w_ref[...].astype(jnp.bfloat16),
                  preferred_element_type=jnp.float32)
    o_ref[...] = (acc + b_ref[...]).astype(o_ref.dtype)


def _pick_tile(n, candidates):
    for c in candidates:
        if n % c == 0:
            return c
    return n


def _mm(x, w, b=None, out_dtype=jnp.float32):
    m, k = x.shape
    n = w.shape[1]
    if b is None:
        b = jnp.zeros((n,), jnp.float32)
    tm = _pick_tile(m, (1024, 512, 256, 128, 64, 32, 16, 8))
    tn = _pick_tile(n, (768, 512, 384, 256, 128))
    return pl.pallas_call(
        _mm_kernel,
        out_shape=jax.ShapeDtypeStruct((m, n), out_dtype),
        grid=(n // tn, m // tm),
        in_specs=[
            pl.BlockSpec((tm, k), lambda j, i: (i, 0)),
            pl.BlockSpec((k, tn), lambda j, i: (0, j)),
            pl.BlockSpec((1, tn), lambda j, i: (0, j)),
        ],
        out_specs=pl.BlockSpec((tm, tn), lambda j, i: (i, j)),
        compiler_params=pltpu.CompilerParams(
            dimension_semantics=("arbitrary", "arbitrary"),
            vmem_limit_bytes=V7X_VMEM_LIMIT_BYTES),
        name="dense_proj",
    )(x, w, b.reshape(1, n))


def _mm3(x, w, b=None):
    lead = x.shape[:-1]
    return _mm(x.reshape(-1, x.shape[-1]), w, b).reshape(*lead, w.shape[1])


NEG_BIG = -1e30
SEL_TQ = 256
SEL_TK = 256
HEADS_PER_STEP = 8


def _half_select(x, want_half, have_half):
    if want_half != have_half:
        x = pltpu.roll(x, HEAD_DIM, axis=1)
    return x


def _sel_attn_kernel(q_ref, k_ref, v_ref, m_ref, e_ref, o_ref, mx_ref, l_ref, acc_ref):
    i = pl.program_id(2)
    tq, tk = SEL_TQ, SEL_TK
    lane = lax.broadcasted_iota(jnp.int32, (tq, V7X_LANES), 1)
    q_pos = i * tq + lax.broadcasted_iota(jnp.int32, (tq, tk), 0)
    k_off = lax.broadcasted_iota(jnp.int32, (tq, tk), 1)
    outs = []
    for hh in range(HEADS_PER_STEP):
        grp = hh // 4
        qc = q_ref[:, (hh // 2) * V7X_LANES:(hh // 2 + 1) * V7X_LANES] * (HEAD_DIM ** -0.5)
        qc = _half_select(qc, grp, hh % 2)
        qh = jnp.where((lane >= grp * HEAD_DIM) & (lane < (grp + 1) * HEAD_DIM), qc, 0.0).astype(jnp.bfloat16)
        selm = m_ref[grp]
        mx_ref[...] = jnp.full(mx_ref.shape, NEG_BIG, jnp.float32)
        l_ref[...] = jnp.zeros(l_ref.shape, jnp.float32)
        acc_ref[...] = jnp.zeros(acc_ref.shape, jnp.float32)

        def body(j, carry):
            k0 = pl.multiple_of(j * tk, tk)
            kt = k_ref[pl.ds(k0, tk), :].astype(jnp.bfloat16)
            vt = v_ref[pl.ds(k0, tk), :].astype(jnp.bfloat16)
            s = lax.dot_general(qh, kt, (((1,), (1,)), ((), ())), preferred_element_type=jnp.float32)
            blk = jnp.dot(selm, e_ref[j], preferred_element_type=jnp.float32)
            valid = (blk > 0.5) & (k0 + k_off <= q_pos)
            s = jnp.where(valid, s, NEG_BIG)
            m_old = mx_ref[...]
            m_new = jnp.maximum(m_old, jnp.max(s, axis=-1, keepdims=True))
            p = jnp.where(valid, jnp.exp(s - m_new), 0.0)
            corr = jnp.exp(m_old - m_new)
            l_ref[...] = corr * l_ref[...] + jnp.sum(p, axis=-1, keepdims=True)
            acc_ref[...] = corr * acc_ref[...] + jnp.dot(p.astype(jnp.bfloat16), vt, preferred_element_type=jnp.float32)
            mx_ref[...] = m_new
            return carry

        lax.fori_loop(0, i + 1, body, 0)
        o = acc_ref[...] / l_ref[...]
        outs.append(_half_select(o, hh % 2, grp))
    for c in range(HEADS_PER_STEP // 2):
        o_ref[:, c * V7X_LANES:(c + 1) * V7X_LANES] = jnp.where(lane < HEAD_DIM, outs[2 * c], outs[2 * c + 1])


def _sel_expand_matrix(seq):
    n_blk = seq // NSA_SEL_LEN
    e = (np.arange(seq)[None, :] // NSA_SEL_LEN == np.arange(n_blk)[:, None]).astype(np.float32)
    return e.reshape(n_blk, seq // SEL_TK, SEL_TK).transpose(1, 0, 2)


def nsa_selected_attention_pallas(u, q_col, k_col, v_col, selmask):
    bsz, seq, _ = u.shape
    n_blk = selmask.shape[-1]
    n_pairs = NSA_KV_GROUPS // 2
    qw = HEADS_PER_STEP * HEAD_DIM
    qb, kb, vb = q_col // qw, k_col // V7X_LANES, v_col // V7X_LANES
    expand = jnp.asarray(_sel_expand_matrix(seq), jnp.bfloat16)
    return pl.pallas_call(
        _sel_attn_kernel,
        out_shape=jax.ShapeDtypeStruct((bsz, seq, NSA_HEADS * HEAD_DIM), jnp.float32),
        grid=(bsz, n_pairs, seq // SEL_TQ),
        in_specs=[
            pl.BlockSpec((None, SEL_TQ, qw), lambda b, gp, i: (b, i, qb + gp)),
            pl.BlockSpec((None, seq, V7X_LANES), lambda b, gp, i: (b, 0, kb + gp)),
            pl.BlockSpec((None, seq, V7X_LANES), lambda b, gp, i: (b, 0, vb + gp)),
            pl.BlockSpec((None, 2, SEL_TQ, n_blk), lambda b, gp, i: (b, gp, i, 0)),
            pl.BlockSpec((seq // SEL_TK, n_blk, SEL_TK), lambda b, gp, i: (0, 0, 0)),
        ],
        out_specs=pl.BlockSpec((None, SEL_TQ, qw), lambda b, gp, i: (b, i, gp)),
        scratch_shapes=[pltpu.VMEM((SEL_TQ, 1), jnp.float32),
                        pltpu.VMEM((SEL_TQ, 1), jnp.float32),
                        pltpu.VMEM((SEL_TQ, V7X_LANES), jnp.float32)],
        compiler_params=pltpu.CompilerParams(
            dimension_semantics=("arbitrary", "arbitrary", "arbitrary"),
            vmem_limit_bytes=V7X_VMEM_LIMIT_BYTES),
        name="nsa_selected_attention",
    )(u, u, u, selmask.astype(jnp.bfloat16), expand)


def _make_band_kernel(r, t, max_dist, n_outer, with_lse):
    heads = 2 * r
    n_prev = -(-max_dist // t)

    def kern(q_ref, k_ref, v_ref, *rest):
        if with_lse:
            o_ref, lse_ref, mx_ref, l_ref, acc_ref = rest
        else:
            o_ref, mx_ref, l_ref, acc_ref = rest
        i = pl.program_id(n_outer)
        lane = lax.broadcasted_iota(jnp.int32, (t, V7X_LANES), 1)
        dist0 = i * t + lax.broadcasted_iota(jnp.int32, (t, t), 0) - lax.broadcasted_iota(jnp.int32, (t, t), 1)
        outs, lses = [], []
        for hh in range(heads):
            half = hh // r
            qc = q_ref[:, (hh // 2) * V7X_LANES:(hh // 2 + 1) * V7X_LANES] * (HEAD_DIM ** -0.5)
            qc = _half_select(qc, half, hh % 2)
            qh = jnp.where((lane >= half * HEAD_DIM) & (lane < (half + 1) * HEAD_DIM), qc, 0.0).astype(jnp.bfloat16)
            mx_ref[...] = jnp.full(mx_ref.shape, NEG_BIG, jnp.float32)
            l_ref[...] = jnp.zeros(l_ref.shape, jnp.float32)
            acc_ref[...] = jnp.zeros(acc_ref.shape, jnp.float32)

            def body(j, carry):
                k0 = pl.multiple_of(j * t, t)
                kt = k_ref[pl.ds(k0, t), :].astype(jnp.bfloat16)
                vt = v_ref[pl.ds(k0, t), :].astype(jnp.bfloat16)
                s = lax.dot_general(qh, kt, (((1,), (1,)), ((), ())), preferred_element_type=jnp.float32)
                dist = dist0 - k0
                valid = (dist >= 0) & (dist <= max_dist)
                s = jnp.where(valid, s, NEG_BIG)
                m_old = mx_ref[...]
                m_new = jnp.maximum(m_old, jnp.max(s, axis=-1, keepdims=True))
                p = jnp.where(valid, jnp.exp(s - m_new), 0.0)
                corr = jnp.exp(m_old - m_new)
                l_ref[...] = corr * l_ref[...] + jnp.sum(p, axis=-1, keepdims=True)
                acc_ref[...] = corr * acc_ref[...] + jnp.dot(p.astype(jnp.bfloat16), vt,
                                                             preferred_element_type=jnp.float32)
                mx_ref[...] = m_new
                return carry

            lax.fori_loop(jnp.maximum(i - n_prev, 0), i + 1, body, 0)
            outs.append(_half_select(acc_ref[...] / l_ref[...], hh % 2, half))
            if with_lse:
                lses.append(jnp.broadcast_to(mx_ref[...] + jnp.log(l_ref[...]), (t, V7X_LANES)))
        for c in range(heads // 2):
            sl = slice(c * V7X_LANES, (c + 1) * V7X_LANES)
            o_ref[:, sl] = jnp.where(lane < HEAD_DIM, outs[2 * c], outs[2 * c + 1])
            if with_lse:
                lse_ref[:, sl] = jnp.where(lane < HEAD_DIM, lses[2 * c], lses[2 * c + 1])

    return kern


def _band_attention_call(u, *, r, t, max_dist, seq_len, outer, q_map, k_map, v_map, out_width, out_map, with_lse, name):
    qw = 2 * r * HEAD_DIM
    out_sds = jax.ShapeDtypeStruct((u.shape[0], seq_len, out_width), jnp.float32)
    out_spec = pl.BlockSpec((None, t, qw), out_map)
    return pl.pallas_call(
        _make_band_kernel(r, t, max_dist, len(outer), with_lse),
        out_shape=(out_sds, out_sds) if with_lse else out_sds,
        grid=(*outer, seq_len // t),
        in_specs=[pl.BlockSpec((None, t, qw), q_map),
                  pl.BlockSpec((None, seq_len, V7X_LANES), k_map),
                  pl.BlockSpec((None, seq_len, V7X_LANES), v_map)],
        out_specs=(out_spec, out_spec) if with_lse else out_spec,
        scratch_shapes=[pltpu.VMEM((t, 1), jnp.float32),
                        pltpu.VMEM((t, 1), jnp.float32),
                        pltpu.VMEM((t, V7X_LANES), jnp.float32)],
        compiler_params=pltpu.CompilerParams(
            dimension_semantics=("arbitrary",) * (len(outer) + 1),
            vmem_limit_bytes=V7X_VMEM_LIMIT_BYTES),
        name=name,
    )(u, u, u)


DIL_TILE = 128


def dilated_attention_pallas(u, q_col, k_col, v_col):
    bsz, seq, w = u.shape
    gw = DIL_HEADS * HEAD_DIM
    outs, lses = [], []
    for gi, (window, dil) in enumerate(DIL_GROUPS):
        sub_len = seq // dil
        cw = w // V7X_LANES

        def cmap(col, whole):
            base = (col + gi * gw) // V7X_LANES
            if whole:
                return lambda b, rr, c, i: (b, 0, rr * cw + base + c)
            return lambda b, rr, c, i: (b, i, rr * cw + base + c)

        o, lse = _band_attention_call(
            u.reshape(bsz, sub_len, dil * w), r=1, t=min(DIL_TILE, sub_len), max_dist=window // dil, seq_len=sub_len,
            outer=(bsz, dil, gw // V7X_LANES),
            q_map=cmap(q_col, False), k_map=cmap(k_col, True), v_map=cmap(v_col, True),
            out_width=dil * gw, out_map=lambda b, rr, c, i: (b, i, rr * (gw // V7X_LANES) + c),
            with_lse=True, name="dilated_attention")
        outs.append(o.reshape(bsz, seq, gw))
        lses.append(lse.reshape(bsz, seq, gw))
    return outs, lses


WIN_TILE = 256


def nsa_window_attention_pallas(u, q_col, k_col, v_col):
    bsz, seq, _ = u.shape
    qw = HEADS_PER_STEP * HEAD_DIM
    qb, kb, vb = q_col // qw, k_col // V7X_LANES, v_col // V7X_LANES
    return _band_attention_call(
        u, r=NSA_HEADS // NSA_KV_GROUPS, t=WIN_TILE, max_dist=NSA_WINDOW - 1, seq_len=seq,
        outer=(bsz, NSA_KV_GROUPS // 2),
        q_map=lambda b, c, i: (b, i, qb + c), k_map=lambda b, c, i: (b, 0, kb + c), v_map=lambda b, c, i: (b, 0, vb + c),
        out_width=NSA_HEADS * HEAD_DIM, out_map=lambda b, c, i: (b, i, c),
        with_lse=False, name="nsa_window_attention")


GLA_SUB = 16


def _split_bf16(x):
    hi = x.astype(jnp.bfloat16)
    lo = (x - hi.astype(jnp.float32)).astype(jnp.bfloat16)
    return hi, lo


def _dot_split(a, b):
    a_hi, a_lo = _split_bf16(a)
    b_hi, b_lo = _split_bf16(b)
    f = functools.partial(jnp.dot, preferred_element_type=jnp.float32)
    return f(a_hi, b_hi) + (f(a_hi, b_lo) + f(a_lo, b_hi))


def _gla_kernel(q_ref, k_ref, v_ref, r_ref, lr_ref, wa_ref, ba_ref, g_ref, o_ref, state_ref):
    c, dk, dv, sub = GLA_CHUNK, GLA_DK, GLA_DV, GLA_SUB

    @pl.when(pl.program_id(1) == 0)
    def _():
        state_ref[...] = jnp.zeros(state_ref.shape, jnp.float32)

    x = _dot_split(lr_ref[...], wa_ref[...]) + ba_ref[...]
    log_a = (jnp.minimum(x, 0.0) - jnp.log1p(jnp.exp(-jnp.abs(x)))) * (1.0 / GLA_TAU)
    tri = (lax.broadcasted_iota(jnp.int32, (c, c), 0) >= lax.broadcasted_iota(jnp.int32, (c, c), 1))
    tri = tri.astype(jnp.bfloat16)
    la_hi, la_lo = _split_bf16(log_a)
    b_all = (jnp.dot(tri, la_hi, preferred_element_type=jnp.float32)
             + jnp.dot(tri, la_lo, preferred_element_type=jnp.float32))
    col = lax.broadcasted_iota(jnp.int32, (sub, c), 1)
    row = lax.broadcasted_iota(jnp.int32, (sub, c), 0)
    bf = jnp.bfloat16
    for h in range(GLA_HEADS):
        q = q_ref[:, h * dk:(h + 1) * dk] * (dk ** -0.5)
        k = k_ref[:, h * dk:(h + 1) * dk]
        v = v_ref[:, h * dv:(h + 1) * dv]
        b = b_all[:, h * dk:(h + 1) * dk]
        a_rows = []
        for blk in range(c // sub):
            lo = blk * sub
            q_i, b_i, k_i = q[lo:lo + sub], b[lo:lo + sub], k[lo:lo + sub]
            if blk > 0:
                b_prev = b[lo - 1:lo]
                q_t = q_i * jnp.exp(b_i - b_prev)
                k_t = k * jnp.exp(jnp.minimum(b_prev - b, 0.0))
                off = lax.dot_general(q_t.astype(bf), k_t.astype(bf), (((1,), (1,)), ((), ())),
                                      preferred_element_type=jnp.float32)
                acc = jnp.where(col < lo, off, 0.0)
            else:
                acc = jnp.zeros((sub, c), jnp.float32)
            for j in range(sub):
                t = q_i * k_i[j:j + 1] * jnp.exp(jnp.minimum(b_i - b_i[j:j + 1], 0.0))
                cs = jnp.sum(t, axis=-1, keepdims=True)
                acc = acc + jnp.where((col == lo + j) & (row >= j), cs, 0.0)
            a_rows.append(acc)
        attn = jnp.concatenate(a_rows, axis=0)
        st = state_ref[h]
        o = jnp.dot(attn.astype(bf), v.astype(bf), preferred_element_type=jnp.float32)
        o = o + lax.dot_general((q * jnp.exp(b)).astype(bf), st.astype(bf), (((1,), (1,)), ((), ())),
                                preferred_element_type=jnp.float32)
        b_last = b[c - 1:c]
        k_d = k * jnp.exp(b_last - b)
        upd = lax.dot_general(v.astype(bf), k_d.astype(bf), (((0,), (0,)), ((), ())),
                              preferred_element_type=jnp.float32)
        state_ref[h] = st * jnp.exp(b_last) + upd
        o = o * lax.rsqrt(jnp.mean(o * o, axis=-1, keepdims=True) + LN_EPS)
        r = r_ref[:, h * dv:(h + 1) * dv]
        o_ref[:, h * dv:(h + 1) * dv] = o * g_ref[...] * (r / (1.0 + jnp.exp(-r)))


def gla_mixer_pallas(q, k, v, r, lr, w_alpha2, b_alpha, norm_g):
    bsz, seq, _ = q.shape
    c = GLA_CHUNK

    def tok(width):
        return pl.BlockSpec((None, c, width), lambda b, i: (b, i, 0))

    def whole(shape):
        return pl.BlockSpec(shape, lambda b, i: (0,) * len(shape))

    return pl.pallas_call(
        _gla_kernel,
        out_shape=jax.ShapeDtypeStruct((bsz, seq, B_V), jnp.float32),
        grid=(bsz, seq // c),
        in_specs=[tok(B_K), tok(B_K), tok(B_V), tok(B_V), tok(GLA_RANK),
                  whole((GLA_RANK, B_K)), whole((1, B_K)), whole((1, GLA_DV))],
        out_specs=tok(B_V),
        scratch_shapes=[pltpu.VMEM((GLA_HEADS, GLA_DV, GLA_DK), jnp.float32)],
        compiler_params=pltpu.CompilerParams(
            dimension_semantics=("arbitrary", "arbitrary"),
            vmem_limit_bytes=V7X_VMEM_LIMIT_BYTES),
        name="gla_mixer",
    )(q, k, v, r, lr, w_alpha2, b_alpha.reshape(1, -1), norm_g.reshape(1, -1))


MOE_TB = 256


def _moe_ffn_kernel(be_ref, nused_ref, x_ref, wgu_ref, bgu_ref, wd_ref, bd_ref, o_ref, wgu_bf, wd_bf):
    i = pl.program_id(0)
    changed = jnp.logical_or(i == 0, be_ref[i] != be_ref[jnp.maximum(i - 1, 0)])

    @pl.when(changed)
    def _():
        wgu_bf[...] = wgu_ref[...].astype(jnp.bfloat16)
        wd_bf[...] = wd_ref[...].astype(jnp.bfloat16)

    @pl.when(i < nused_ref[0])
    def _():
        gu = jnp.dot(x_ref[...].astype(jnp.bfloat16), wgu_bf[...], preferred_element_type=jnp.float32) + bgu_ref[...]
        gate = jnp.minimum(gu[:, :D_FF], SWIGLU_LIMIT)
        up = jnp.clip(gu[:, D_FF:], -SWIGLU_LIMIT, SWIGLU_LIMIT)
        act = (up + 1.0) * gate * (1.0 / (1.0 + jnp.exp(-SWIGLU_ALPHA * gate)))
        o_ref[...] = jnp.dot(act.astype(jnp.bfloat16), wd_bf[...], preferred_element_type=jnp.float32) + bd_ref[...]

    @pl.when(i >= nused_ref[0])
    def _():
        o_ref[...] = jnp.zeros(o_ref.shape, o_ref.dtype)


def moe_expert_ffn_pallas(buf, block_e, n_used, w_gu, b_gu, w_down, b_down):
    n_rows, d = buf.shape
    n_e = w_gu.shape[0]
    grid_spec = pltpu.PrefetchScalarGridSpec(
        num_scalar_prefetch=2,
        grid=(n_rows // MOE_TB,),
        in_specs=[
            pl.BlockSpec((MOE_TB, d), lambda i, be, nu: (i, 0)),
            pl.BlockSpec((None, d, 2 * D_FF), lambda i, be, nu: (be[i], 0, 0)),
            pl.BlockSpec((None, 1, 2 * D_FF), lambda i, be, nu: (be[i], 0, 0)),
            pl.BlockSpec((None, D_FF, d), lambda i, be, nu: (be[i], 0, 0)),
            pl.BlockSpec((None, 1, d), lambda i, be, nu: (be[i], 0, 0)),
        ],
        out_specs=pl.BlockSpec((MOE_TB, d), lambda i, be, nu: (i, 0)),
        scratch_shapes=[pltpu.VMEM((d, 2 * D_FF), jnp.bfloat16), pltpu.VMEM((D_FF, d), jnp.bfloat16)],
    )
    return pl.pallas_call(
        _moe_ffn_kernel,
        out_shape=jax.ShapeDtypeStruct((n_rows, d), jnp.float32),
        grid_spec=grid_spec,
        compiler_params=pltpu.CompilerParams(
            dimension_semantics=("arbitrary",),
            vmem_limit_bytes=V7X_VMEM_LIMIT_BYTES),
        name="moe_expert_ffn",
    )(block_e, n_used, buf, w_gu, b_gu.reshape(n_e, 1, -1), w_down, b_down.reshape(n_e, 1, -1))


def layer_norm(x, g, b):
    xf = x.astype(jnp.float32)
    mu = jnp.mean(xf, axis=-1, keepdims=True)
    var = jnp.mean(jnp.square(xf - mu), axis=-1, keepdims=True)
    return ((xf - mu) * lax.rsqrt(var + LN_EPS)).astype(x.dtype) * g + b


def banded_attention(q, k, v, max_dist):
    n, seq_len, g, r, hd = q.shape
    blk = BAND_BLOCK
    n_prev = -(-max_dist // blk)
    nb = -(-seq_len // blk)
    pad = nb * blk - seq_len
    qp = jnp.pad(q, ((0, 0), (0, pad), (0, 0), (0, 0), (0, 0)))
    kv_pad = ((0, 0), (n_prev * blk, pad), (0, 0), (0, 0))
    kp = jnp.pad(k, kv_pad).reshape(n, nb + n_prev, blk, g, hd)
    vp = jnp.pad(v, kv_pad).reshape(n, nb + n_prev, blk, g, hd)
    kw = jnp.concatenate([kp[:, i:i + nb] for i in range(n_prev + 1)], axis=2)
    vw = jnp.concatenate([vp[:, i:i + nb] for i in range(n_prev + 1)], axis=2)
    width = (n_prev + 1) * blk
    qi = jnp.arange(blk)[:, None]
    kj = jnp.arange(width)[None, :]
    dist = qi + n_prev * blk - kj
    key_abs = jnp.arange(nb)[:, None, None] * blk - n_prev * blk + kj[None]
    mask = ((dist >= 0) & (dist <= max_dist))[None] & (key_abs >= 0)
    scale = hd ** -0.5

    def one_block(args):
        qb, kb, vb, mb = args
        s = jnp.einsum('nqgrd,nkgd->ngrqk', qb, kb).astype(jnp.float32) * scale
        s = jnp.where(mb, s, -jnp.inf)
        m = jnp.max(s, axis=-1, keepdims=True)
        p = jnp.exp(s - m)
        den = jnp.sum(p, axis=-1, keepdims=True)
        o = jnp.einsum('ngrqk,nkgd->nqgrd', (p / den).astype(vb.dtype), vb)
        lse = (m + jnp.log(den))[..., 0]
        return o, jnp.transpose(lse, (0, 3, 1, 2))

    qb = jnp.moveaxis(qp.reshape(n, nb, blk, g, r, hd), 1, 0)
    o, lse = lax.map(one_block, (qb, jnp.moveaxis(kw, 1, 0), jnp.moveaxis(vw, 1, 0), mask))
    o = jnp.moveaxis(o, 0, 1).reshape(n, nb * blk, g, r, hd)[:, :seq_len]
    lse = jnp.moveaxis(lse, 0, 1).reshape(n, nb * blk, g, r)[:, :seq_len]
    return o, lse


def dilated_attention(q, k, v):
    bsz, seq, _, nh, hd = q.shape
    outs, lses = [], []
    for gi, (window, dil) in enumerate(DIL_GROUPS):
        sub_len = seq // dil

        def to_sub(t):
            return t.reshape(bsz, sub_len, dil, nh, hd).transpose(0, 2, 1, 3, 4).reshape(bsz * dil, sub_len, nh, hd)

        o, lse = banded_attention(to_sub(q[:, :, gi])[:, :, :, None], to_sub(k[:, :, gi]), to_sub(v[:, :, gi]), window // dil)
        outs.append(o[:, :, :, 0].reshape(bsz, dil, sub_len, nh, hd).transpose(0, 2, 1, 3, 4).reshape(bsz, seq, nh, hd))
        lses.append(lse[..., 0].reshape(bsz, dil, sub_len, nh).transpose(0, 2, 1, 3).reshape(bsz, seq, nh))
    w = jax.nn.softmax(jnp.stack(lses), axis=0)
    return jnp.einsum('gbsh,gbshd->bshd', w.astype(q.dtype), jnp.stack(outs))


def gla_mixer(q, k, v, r, lr, w_alpha2, b_alpha, norm_g):
    bsz, seq, _ = q.shape
    nh, dk, dv, c = GLA_HEADS, GLA_DK, GLA_DV, GLA_CHUNK
    dtype = q.dtype
    log_a = jax.nn.log_sigmoid((lr @ w_alpha2 + b_alpha).astype(jnp.float32)) / GLA_TAU
    n_chunks = seq // c

    def chunks(t, width):
        return t.reshape(bsz, n_chunks, c, nh, width).transpose(1, 0, 3, 2, 4).astype(jnp.float32)

    qc = chunks(q, dk) * (dk ** -0.5)
    kc = chunks(k, dk)
    vc = chunks(v, dv)
    bc = jnp.cumsum(chunks(log_a, dk), axis=3)
    causal = jnp.tril(jnp.ones((c, c), dtype=bool))

    def step(state, inp):
        qi, ki, vi, bi = inp
        diff = bi[:, :, :, None, :] - bi[:, :, None, :, :]
        decay = jnp.exp(jnp.where(causal[:, :, None], diff, -jnp.inf))
        attn = jnp.einsum('bhic,bhjc,bhijc->bhij', qi, ki, decay)
        b_last = bi[:, :, -1:, :]
        o = attn @ vi + jnp.einsum('bhic,bhcv->bhiv', qi * jnp.exp(bi), state)
        state = jnp.exp(b_last[:, :, 0, :, None]) * state + jnp.einsum('bhjc,bhjv->bhcv', ki * jnp.exp(b_last - bi), vi)
        return state, o

    state0 = jnp.zeros((bsz, nh, dk, dv), jnp.float32)
    _, o = lax.scan(step, state0, (qc, kc, vc, bc))
    o = o.transpose(1, 0, 3, 2, 4).reshape(bsz, seq, nh, dv)
    o = o * lax.rsqrt(jnp.mean(jnp.square(o), axis=-1, keepdims=True) + LN_EPS)
    o = o.astype(dtype) * norm_g * jax.nn.silu(r).reshape(bsz, seq, nh, dv)
    return o.reshape(bsz, seq, nh * dv)


def compress_blocks(t, pe, w1, w2):
    bsz, seq, g, hd = t.shape
    n_cmp = (seq - NSA_CMP_LEN) // NSA_CMP_STRIDE + 1
    idx = jnp.arange(n_cmp)[:, None] * NSA_CMP_STRIDE + jnp.arange(NSA_CMP_LEN)[None, :]
    blocks = t[:, idx] + pe[:, None, :]
    flat = blocks.transpose(0, 1, 3, 2, 4).reshape(bsz, n_cmp, g, NSA_CMP_LEN * hd)
    return jax.nn.gelu(flat @ w1) @ w2


def nsa_selected_attention(q, k, v, sel_idx):
    bsz, seq, g, r, hd = q.shape
    n_blk = seq // NSA_SEL_LEN
    n_sel = sel_idx.shape[-1]
    kb = k.reshape(bsz, n_blk, NSA_SEL_LEN, g, hd).transpose(0, 3, 1, 2, 4)
    vb = v.reshape(bsz, n_blk, NSA_SEL_LEN, g, hd).transpose(0, 3, 1, 2, 4)
    nqb = seq // NSA_Q_BLOCK
    qx = jnp.moveaxis(q.reshape(bsz, nqb, NSA_Q_BLOCK, g, r, hd), 1, 0)
    ix = jnp.moveaxis(sel_idx.reshape(bsz, g, nqb, NSA_Q_BLOCK, n_sel), 2, 0)
    tp = jnp.arange(seq).reshape(nqb, NSA_Q_BLOCK)
    gather = jax.vmap(jax.vmap(lambda blocks, idx: blocks[idx]))
    offs = jnp.arange(NSA_SEL_LEN)
    scale = hd ** -0.5

    def one_block(args):
        qb, ib, tb = args
        kg = gather(kb, ib)
        vg = gather(vb, ib)
        s = jnp.einsum('bqgrd,bgqnld->bgqrnl', qb, kg).astype(jnp.float32) * scale
        kpos = ib[..., None] * NSA_SEL_LEN + offs
        valid = kpos <= tb[None, None, :, None, None]
        s = jnp.where(valid[:, :, :, None], s, -jnp.inf)
        p = jax.nn.softmax(s.reshape(bsz, g, NSA_Q_BLOCK, r, n_sel * NSA_SEL_LEN), axis=-1)
        p = p.reshape(bsz, g, NSA_Q_BLOCK, r, n_sel, NSA_SEL_LEN)
        return jnp.einsum('bgqrnl,bgqnld->bqgrd', p.astype(vg.dtype), vg)

    o = lax.map(one_block, (qx, ix, tp))
    return jnp.moveaxis(o, 0, 1).reshape(bsz, seq, g, r, hd)


def nsa_mixer(u_c, gate_logits, pe_k, w1_k, w2_k, pe_v, w1_v, w2_v):
    bsz, seq, _ = u_c.shape
    g, r, hd = NSA_KV_GROUPS, NSA_HEADS // NSA_KV_GROUPS, HEAD_DIM
    q = u_c[..., :C_Q].reshape(bsz, seq, g, r, hd)
    k_cmp_in = u_c[..., C_Q:C_Q + C_KV].reshape(bsz, seq, g, hd)
    v_cmp_in = u_c[..., C_Q + C_KV:C_Q + 2 * C_KV].reshape(bsz, seq, g, hd)
    dtype = q.dtype
    pos = jnp.arange(seq)
    scale = hd ** -0.5
    kc = compress_blocks(k_cmp_in, pe_k, w1_k, w2_k)
    vc = compress_blocks(v_cmp_in, pe_v, w1_v, w2_v)
    n_cmp = kc.shape[1]
    cmp_start = jnp.arange(n_cmp) * NSA_CMP_STRIDE
    cvalid = (cmp_start + NSA_CMP_LEN - 1)[None, :] <= pos[:, None]
    s = jnp.einsum('bsgrd,bcgd->bgrsc', q, kc).astype(jnp.float32) * scale
    s = jnp.where(cvalid, s, -jnp.inf)
    m = jnp.max(s, axis=-1, keepdims=True)
    m = jnp.where(jnp.isfinite(m), m, 0.0)
    p = jnp.where(cvalid, jnp.exp(s - m), 0.0)
    p = p / jnp.maximum(jnp.sum(p, axis=-1, keepdims=True), 1e-30)
    o_cmp = jnp.einsum('bgrsc,bcgd->bsgrd', p.astype(dtype), vc)
    n_blk = seq // NSA_SEL_LEN
    sel_start = jnp.arange(n_blk) * NSA_SEL_LEN
    overlap = (cmp_start[:, None] < sel_start[None, :] + NSA_SEL_LEN) & (cmp_start[:, None] + NSA_CMP_LEN > sel_start[None, :])
    imp = jnp.einsum('bgrsc,cj->bgsj', p, overlap.astype(jnp.float32))
    q_blk = pos // NSA_SEL_LEN
    j = jnp.arange(n_blk)[None, :]
    forced = (j == 0) | (j == q_blk[:, None]) | (j == q_blk[:, None] - 1)
    future = j > q_blk[:, None]
    imp = jnp.where(forced, jnp.inf, jnp.where(future, -jnp.inf, imp))
    n_sel = min(NSA_N_SELECT, n_blk)
    _, sel_idx = lax.top_k(imp, n_sel)
    selmask = (sel_idx[..., None] == jnp.arange(n_blk)).any(-2)
    o_sel = nsa_selected_attention_pallas(u_c, 0, C_Q + 2 * C_KV, C_Q + 3 * C_KV, selmask).reshape(q.shape)
    o_win = nsa_window_attention_pallas(u_c, 0, C_Q + 4 * C_KV, C_Q + 5 * C_KV).reshape(q.shape)
    gates = jax.nn.sigmoid(gate_logits).reshape(bsz, seq, g, r, NSA_BRANCHES)
    o = gates[..., 0:1] * o_cmp + gates[..., 1:2] * o_sel + gates[..., 2:3] * o_win
    return o.reshape(bsz, seq, g * r * hd)


def token_mixer(h, w_in, b_in, w_alpha2, b_alpha, gla_norm_g, cmp_pe_k, cmp_w1_k, cmp_w2_k,
                cmp_pe_v, cmp_w1_v, cmp_w2_v, w_br_a, w_br_b, w_br_c, w_o_mix):
    bsz, seq, d = h.shape
    offs = np.concatenate([[0], np.cumsum(IN_WIDTHS)]).tolist()
    h2 = h.reshape(-1, d)

    def proj(lo, hi):
        return _mm(h2, w_in[:, offs[lo]:offs[hi]], b_in[offs[lo]:offs[hi]]).reshape(bsz, seq, -1)

    u_ab = proj(0, 7)
    a_q, a_k, a_v, b_q, b_k, b_v, b_r = jnp.split(u_ab, np.cumsum(IN_WIDTHS[:7])[:-1].tolist(), axis=-1)
    w_small = jnp.concatenate([w_in[:, offs[7]:offs[8]], w_in[:, offs[15]:offs[16]]], axis=1)
    b_small = jnp.concatenate([b_in[offs[7]:offs[8]], b_in[offs[15]:offs[16]]])
    n_small = w_small.shape[1]
    w_small = jnp.pad(w_small, ((0, 0), (0, V7X_LANES - n_small)))
    b_small = jnp.pad(b_small, (0, V7X_LANES - n_small))
    u_small = _mm(h2, w_small, b_small).reshape(bsz, seq, -1)
    b_lr = u_small[..., :GLA_RANK]
    c_g = u_small[..., GLA_RANK:n_small]
    u_c = proj(8, 15)
    m_g = proj(16, 17)
    dil_o, dil_lse = dilated_attention_pallas(u_ab, 0, A_W, 2 * A_W)
    dil_w = jax.nn.softmax(jnp.stack(dil_lse), axis=0)
    y_a = jnp.sum(dil_w * jnp.stack(dil_o), axis=0)
    y_b = gla_mixer_pallas(b_q, b_k, b_v, b_r, b_lr, w_alpha2, b_alpha, gla_norm_g)
    y_c = nsa_mixer(u_c, c_g, cmp_pe_k, cmp_w1_k, cmp_w2_k, cmp_pe_v, cmp_w1_v, cmp_w2_v)
    g = jax.nn.sigmoid(m_g).reshape(bsz, seq, N_BRANCHES, d)
    merged = g[:, :, 0] * _mm3(y_a, w_br_a) + g[:, :, 1] * _mm3(y_b, w_br_b) + g[:, :, 2] * _mm3(y_c, w_br_c)
    return _mm3(merged, w_o_mix)


def memory_cross_attention(x, mem, w_xq, w_xk, w_xv, w_xo):
    bsz, seq, d = x.shape
    hd = d // XATTN_HEADS
    q = _mm3(x, w_xq).reshape(bsz, seq, XATTN_HEADS, hd)
    k = _mm3(mem, w_xk).reshape(bsz, mem.shape[1], XATTN_HEADS, hd)
    v = _mm3(mem, w_xv).reshape(bsz, mem.shape[1], XATTN_HEADS, hd)
    s = jnp.einsum('bshd,bmhd->bhsm', q, k).astype(jnp.float32) * (hd ** -0.5)
    p = jax.nn.softmax(s, axis=-1).astype(x.dtype)
    o = jnp.einsum('bhsm,bmhd->bshd', p, v).reshape(bsz, seq, d)
    return _mm3(o, w_xo)


def clamped_swiglu(gate, up):
    gate = jnp.minimum(gate, SWIGLU_LIMIT)
    up = jnp.clip(up, -SWIGLU_LIMIT, SWIGLU_LIMIT)
    return (up + 1.0) * gate * jax.nn.sigmoid(SWIGLU_ALPHA * gate)


def moe_ffn(h, w_router, b_router, w_gu, b_gu, w_down, b_down):
    bsz, seq, d = h.shape
    t = h.reshape(-1, d)
    n_assign = t.shape[0] * TOP_K
    logits = t @ w_router + b_router
    top_logits, top_idx = lax.top_k(logits, TOP_K)
    gates = jax.nn.softmax(top_logits.astype(jnp.float32), axis=-1).astype(t.dtype)
    flat_e = top_idx.reshape(-1)
    order = jnp.argsort(flat_e)
    sorted_e = flat_e[order]
    tok_of = order // TOP_K
    counts = jnp.bincount(flat_e, length=N_EXPERTS)
    padded = (counts + MOE_TB - 1) // MOE_TB * MOE_TB
    start = jnp.cumsum(counts) - counts
    pend = jnp.cumsum(padded)
    pstart = pend - padded
    dest = pstart[sorted_e] + (jnp.arange(n_assign) - start[sorted_e])
    n_rows = n_assign + N_EXPERTS * MOE_TB
    n_blocks = n_rows // MOE_TB
    buf = jnp.zeros((n_rows, d), t.dtype).at[dest].set(t[tok_of])
    block_e = jnp.minimum(jnp.searchsorted(pend, jnp.arange(n_blocks) * MOE_TB, side='right'), N_EXPERTS - 1)
    n_used = (pend[-1:] // MOE_TB).astype(jnp.int32)
    out = moe_expert_ffn_pallas(buf, block_e.astype(jnp.int32), n_used, w_gu, b_gu, w_down, b_down)
    y_assign = out[dest] * gates.reshape(-1)[order][:, None]
    y = jnp.zeros_like(t).at[tok_of].add(y_assign)
    return y.reshape(bsz, seq, d)


def kernel(x, mem, ln0_g, ln0_b, w_in, b_in, w_alpha2, b_alpha, gla_norm_g, cmp_pe_k, cmp_w1_k, cmp_w2_k, cmp_pe_v, cmp_w1_v, cmp_w2_v, w_br_a, w_br_b, w_br_c, w_o_mix, ln1_g, ln1_b, w_xq, w_xk, w_xv, w_xo, ln2_g, ln2_b, w_router, b_router, w_gu, b_gu, w_down, b_down, ln3_g, ln3_b):
    alpha = DEEPNORM_ALPHA
    x = layer_norm(x, ln0_g, ln0_b)
    for li in range(DEPTH):
        mix = token_mixer(x, w_in[li], b_in[li], w_alpha2[li], b_alpha[li], gla_norm_g[li],
                          cmp_pe_k[li], cmp_w1_k[li], cmp_w2_k[li], cmp_pe_v[li], cmp_w1_v[li], cmp_w2_v[li],
                          w_br_a[li], w_br_b[li], w_br_c[li], w_o_mix[li])
        x = layer_norm(alpha * x + mix, ln1_g[li], ln1_b[li])
        xa = memory_cross_attention(x, mem, w_xq[li], w_xk[li], w_xv[li], w_xo[li])
        x = layer_norm(alpha * x + xa, ln2_g[li], ln2_b[li])
        ff = moe_ffn(x, w_router[li], b_router[li], w_gu[li], b_gu[li], w_down[li], b_down[li])
        x = layer_norm(alpha * x + ff, ln3_g[li], ln3_b[li])
    return x
```

```python
import functools

import jax
import jax.numpy as jnp
import numpy as np
from jax import lax
from jax.experimental import pallas as pl
from jax.experimental.pallas import tpu as pltpu

D_MODEL = 1024
DEPTH = 2
HEAD_DIM = 64
N_BRANCHES = 3
LN_EPS = 1e-5

DIL_GROUPS = ((128, 1), (512, 4), (2048, 16))
DIL_HEADS = 4

GLA_HEADS = 4
GLA_DK = D_MODEL // 2 // GLA_HEADS
GLA_DV = D_MODEL // GLA_HEADS
GLA_RANK = 16
GLA_TAU = 16.0
GLA_CHUNK = 64

NSA_HEADS = 16
NSA_KV_GROUPS = 4
NSA_GROUP_HEADS = NSA_HEADS // NSA_KV_GROUPS
NSA_BRANCHES = 3
NSA_CMP_LEN = 32
NSA_CMP_STRIDE = 16
NSA_CMP_HIDDEN = 2 * HEAD_DIM
NSA_SEL_LEN = 64
NSA_N_SELECT = 16
NSA_WINDOW = 512

XATTN_HEADS = 4

N_EXPERTS = 32
TOP_K = 4
D_FF = D_MODEL
SWIGLU_LIMIT = 7.0
SWIGLU_ALPHA = 1.702

DEEPNORM_ALPHA = (2 * DEPTH) ** 0.25

A_W = len(DIL_GROUPS) * DIL_HEADS * HEAD_DIM
B_K = GLA_HEADS * GLA_DK
B_V = GLA_HEADS * GLA_DV
C_Q = NSA_HEADS * HEAD_DIM
C_KV = NSA_KV_GROUPS * HEAD_DIM
IN_WIDTHS = (A_W, A_W, A_W,
             B_K, B_K, B_V, B_V, GLA_RANK,
             C_Q, C_KV, C_KV, C_KV, C_KV, C_KV, C_KV, NSA_HEADS * NSA_BRANCHES,
             N_BRANCHES * D_MODEL)

V7X_VMEM_LIMIT_BYTES = 48 * 1024 * 1024
V7X_LANES = 128
NEG_BIG = -1e30
BF = jnp.bfloat16


def _cparams(n_axes):
    return pltpu.CompilerParams(dimension_semantics=("arbitrary",) * n_axes,
                                vmem_limit_bytes=V7X_VMEM_LIMIT_BYTES)


def _split_bf16(x):
    hi = x.astype(BF)
    lo = (x - hi.astype(jnp.float32)).astype(BF)
    return hi, lo


def _dot_split(a, b):
    a_hi, a_lo = _split_bf16(a)
    b_hi, b_lo = _split_bf16(b)
    f = functools.partial(jnp.dot, preferred_element_type=jnp.float32)
    return f(a_hi, b_hi) + (f(a_hi, b_lo) + f(a_lo, b_hi))


def _dot_split_lhs(a, b_exact):
    a_hi, a_lo = _split_bf16(a)
    f = functools.partial(jnp.dot, preferred_element_type=jnp.float32)
    return f(a_hi, b_exact) + f(a_lo, b_exact)


def _dot_nt(a, b):
    return lax.dot_general(a, b, (((1,), (1,)), ((), ())), preferred_element_type=jnp.float32)


def _mm_kernel(x_ref, w_ref, b_ref, o_ref):
    acc = jnp.dot(x_ref[...].astype(BF), w_ref[...].astype(BF), preferred_element_type=jnp.float32)
    o_ref[...] = (acc + b_ref[...]).astype(o_ref.dtype)


def _pick_tile(n, candidates):
    for c in candidates:
        if n % c == 0:
            return c
    return n


def _mm(x, w, b=None, out_dtype=jnp.float32):
    m, k = x.shape
    n = w.shape[1]
    if b is None:
        b = jnp.zeros((n,), jnp.float32)
    tm = _pick_tile(m, (1024, 512, 256, 128, 64, 32, 16, 8))
    tn = _pick_tile(n, (768, 512, 384, 256, 128))
    return pl.pallas_call(
        _mm_kernel,
        out_shape=jax.ShapeDtypeStruct((m, n), out_dtype),
        grid=(n // tn, m // tm),
        in_specs=[
            pl.BlockSpec((tm, k), lambda j, i: (i, 0)),
            pl.BlockSpec((k, tn), lambda j, i: (0, j)),
            pl.BlockSpec((1, tn), lambda j, i: (0, j)),
        ],
        out_specs=pl.BlockSpec((tm, tn), lambda j, i: (i, j)),
        compiler_params=_cparams(2),
        name="dense_proj",
    )(x, w, b.reshape(1, n))


def _mm3(x, w, b=None):
    lead = x.shape[:-1]
    return _mm(x.reshape(-1, x.shape[-1]), w, b).reshape(*lead, w.shape[1])


GQA_TQ = 128
GQA_TK = 256


def _half_select(x, want_half, have_half):
    if want_half != have_half:
        x = pltpu.roll(x, HEAD_DIM, axis=1)
    return x


def _lane_half_mask(shape, half):
    lane = lax.broadcasted_iota(jnp.int32, shape, 1)
    return (lane >= half * HEAD_DIM) & (lane < (half + 1) * HEAD_DIM)


def _make_gqa_kernel(select, max_dist, n_outer):
    r, tq, tk = NSA_GROUP_HEADS, GQA_TQ, GQA_TK

    def kern(q_ref, k_ref, v_ref, *rest):
        if select:
            m_ref, e_ref, o_ref, mx_ref, l_ref, acc_ref = rest
        else:
            o_ref, mx_ref, l_ref, acc_ref = rest
        i = pl.program_id(n_outer)
        dist0 = i * tq + lax.broadcasted_iota(jnp.int32, (tk, tq), 1) - lax.broadcasted_iota(jnp.int32, (tk, tq), 0)
        qs = []
        for half in range(2):
            parts = []
            for rr in range(r):
                hh = half * r + rr
                qc = q_ref[:, (hh // 2) * V7X_LANES:(hh // 2 + 1) * V7X_LANES] * (HEAD_DIM ** -0.5)
                qc = _half_select(qc, half, hh % 2)
                parts.append(jnp.where(_lane_half_mask(qc.shape, half), qc, 0.0).astype(BF))
            qs.append(jnp.concatenate(parts, axis=0))
        mx_ref[...] = jnp.full(mx_ref.shape, NEG_BIG, jnp.float32)
        l_ref[...] = jnp.zeros(l_ref.shape, jnp.float32)
        acc_ref[...] = jnp.zeros(acc_ref.shape, jnp.float32)

        def body(j, carry):
            k0 = pl.multiple_of(j * tk, tk)
            kt = k_ref[pl.ds(k0, tk), :].astype(BF)
            vt = v_ref[pl.ds(k0, tk), :].astype(BF)
            dist = dist0 - k0
            in_band = (dist >= 0) if select else ((dist >= 0) & (dist <= max_dist))
            for half in range(2):
                if select:
                    blk = jnp.dot(e_ref[j], m_ref[half], preferred_element_type=jnp.float32)
                    valid = (blk > 0.5) & in_band
                else:
                    valid = in_band
                bias = jnp.where(valid, 0.0, NEG_BIG)
                s = _dot_nt(kt, qs[half]) + jnp.concatenate([bias] * r, axis=1)
                m_old = mx_ref[half]
                m_new = jnp.maximum(m_old, jnp.max(s, axis=0, keepdims=True))
                p = jnp.exp(s - m_new)
                corr = jnp.exp(m_old - m_new)
                l_ref[half] = corr * l_ref[half] + jnp.sum(p, axis=0, keepdims=True)
                pv = lax.dot_general(vt, p.astype(BF), (((0,), (0,)), ((), ())), preferred_element_type=jnp.float32)
                acc_ref[half] = corr * acc_ref[half] + pv
                mx_ref[half] = m_new
            return carry

        lo = 0 if select else jnp.maximum(i * tq - max_dist, 0) // tk
        lax.fori_loop(lo, (i * tq + tq - 1) // tk + 1, body, 0)
        outs = []
        for half in range(2):
            o = (acc_ref[half] / l_ref[half]).T
            for rr in range(r):
                outs.append(_half_select(o[rr * tq:(rr + 1) * tq], (half * r + rr) % 2, half))
        lane = lax.broadcasted_iota(jnp.int32, (tq, V7X_LANES), 1)
        for c in range(r):
            o_ref[:, c * V7X_LANES:(c + 1) * V7X_LANES] = jnp.where(lane < HEAD_DIM, outs[2 * c], outs[2 * c + 1])

    return kern


def _sel_expand_matrix(seq):
    n_blk = seq // NSA_SEL_LEN
    e = (np.arange(seq)[:, None] // NSA_SEL_LEN == np.arange(n_blk)[None, :]).astype(np.float32)
    return e.reshape(seq // GQA_TK, GQA_TK, n_blk)


def nsa_gqa_attention_pallas(u, q_col, k_col, v_col, selmask=None):
    bsz, seq, _ = u.shape
    select = selmask is not None
    n_pairs = NSA_KV_GROUPS // 2
    qw = 2 * NSA_GROUP_HEADS * HEAD_DIM
    qb, kb, vb = q_col // qw, k_col // V7X_LANES, v_col // V7X_LANES
    rows = NSA_GROUP_HEADS * GQA_TQ
    in_specs = [
        pl.BlockSpec((None, GQA_TQ, qw), lambda b, gp, i: (b, i, qb + gp)),
        pl.BlockSpec((None, seq, V7X_LANES), lambda b, gp, i: (b, 0, kb + gp)),
        pl.BlockSpec((None, seq, V7X_LANES), lambda b, gp, i: (b, 0, vb + gp)),
    ]
    args = [u, u, u]
    if select:
        n_blk = selmask.shape[2]
        in_specs += [pl.BlockSpec((None, 2, n_blk, GQA_TQ), lambda b, gp, i: (b, gp, 0, i)),
                     pl.BlockSpec((seq // GQA_TK, GQA_TK, n_blk), lambda b, gp, i: (0, 0, 0))]
        args += [selmask, jnp.asarray(_sel_expand_matrix(seq), BF)]
    return pl.pallas_call(
        _make_gqa_kernel(select, NSA_WINDOW - 1, 2),
        out_shape=jax.ShapeDtypeStruct((bsz, seq, NSA_HEADS * HEAD_DIM), jnp.float32),
        grid=(bsz, n_pairs, seq // GQA_TQ),
        in_specs=in_specs,
        out_specs=pl.BlockSpec((None, GQA_TQ, qw), lambda b, gp, i: (b, i, gp)),
        scratch_shapes=[pltpu.VMEM((2, 1, rows), jnp.float32),
                        pltpu.VMEM((2, 1, rows), jnp.float32),
                        pltpu.VMEM((2, V7X_LANES, rows), jnp.float32)],
        compiler_params=_cparams(3),
        name="nsa_selected_attention" if select else "nsa_window_attention",
    )(*args)


DIL_TILE = 128


def _make_dil_kernel(max_dist, n_kv):
    t = DIL_TILE

    def kern(q_ref, k_ref, v_ref, o_ref, lse_ref):
        i = pl.program_id(2)
        start = pl.multiple_of(jnp.maximum(i + 1 - n_kv, 0) * t, t)
        kw = n_kv * t
        dist = (i * t + lax.broadcasted_iota(jnp.int32, (t, kw), 0)) - (start + lax.broadcasted_iota(jnp.int32, (t, kw), 1))
        bias = jnp.where((dist >= 0) & (dist <= max_dist), 0.0, NEG_BIG)
        lane = lax.broadcasted_iota(jnp.int32, (t, V7X_LANES), 1)
        for c in range(DIL_HEADS // 2):
            sl = slice(c * V7X_LANES, (c + 1) * V7X_LANES)
            qc = q_ref[:, sl] * (HEAD_DIM ** -0.5)
            kt = k_ref[pl.ds(start, kw), sl].astype(BF)
            vt = v_ref[pl.ds(start, kw), sl].astype(BF)
            os_, ls_ = [], []
            for half in range(2):
                qh = jnp.where(_lane_half_mask(qc.shape, half), qc, 0.0).astype(BF)
                s = _dot_nt(qh, kt) + bias
                m = jnp.max(s, axis=-1, keepdims=True)
                p = jnp.exp(s - m)
                den = jnp.sum(p, axis=-1, keepdims=True)
                o = jnp.dot(p.astype(BF), vt, preferred_element_type=jnp.float32) / den
                os_.append(o)
                ls_.append(jnp.broadcast_to(m + jnp.log(den), (t, V7X_LANES)))
            o_ref[:, sl] = jnp.where(lane < HEAD_DIM, os_[0], os_[1])
            lse_ref[:, sl] = jnp.where(lane < HEAD_DIM, ls_[0], ls_[1])

    return kern


def dilated_attention_pallas(u, q_col, k_col, v_col):
    bsz, seq, w = u.shape
    gw = DIL_HEADS * HEAD_DIM
    outs, lses = [], []
    for gi, (window, dil) in enumerate(DIL_GROUPS):
        sub_len = seq // dil
        max_dist = window // dil
        n_kv = min(-(-max_dist // DIL_TILE) + 1, sub_len // DIL_TILE)
        wb = w // gw

        def cmap(col, whole):
            base = (col + gi * gw) // gw
            if whole:
                return lambda b, rr, i: (b, 0, rr * wb + base)
            return lambda b, rr, i: (b, i, rr * wb + base)

        sds = jax.ShapeDtypeStruct((bsz, sub_len, dil * gw), jnp.float32)
        out_spec = pl.BlockSpec((None, DIL_TILE, gw), lambda b, rr, i: (b, i, rr))
        uv = u.reshape(bsz, sub_len, dil * w)
        o, lse = pl.pallas_call(
            _make_dil_kernel(max_dist, n_kv),
            out_shape=(sds, sds),
            grid=(bsz, dil, sub_len // DIL_TILE),
            in_specs=[pl.BlockSpec((None, DIL_TILE, gw), cmap(q_col, False)),
                      pl.BlockSpec((None, sub_len, gw), cmap(k_col, True)),
                      pl.BlockSpec((None, sub_len, gw), cmap(v_col, True))],
            out_specs=(out_spec, out_spec),
            compiler_params=_cparams(3),
            name="dilated_attention",
        )(uv, uv, uv)
        outs.append(o.reshape(bsz, seq, gw))
        lses.append(lse.reshape(bsz, seq, gw))
    return outs, lses


GLA_SUB = 16


def _gla_kernel(q_ref, k_ref, v_ref, r_ref, lr_ref, wa_ref, ba_ref, g_ref, o_ref, state_ref):
    c, dk, dv, sub = GLA_CHUNK, GLA_DK, GLA_DV, GLA_SUB

    @pl.when(pl.program_id(1) == 0)
    def _():
        state_ref[...] = jnp.zeros(state_ref.shape, jnp.float32)

    x = _dot_split(lr_ref[...], wa_ref[...]) + ba_ref[...]
    log_a = (jnp.minimum(x, 0.0) - jnp.log1p(jnp.exp(-jnp.abs(x)))) * (1.0 / GLA_TAU)
    tri = (lax.broadcasted_iota(jnp.int32, (c, c), 0) >= lax.broadcasted_iota(jnp.int32, (c, c), 1))
    la_hi, la_lo = _split_bf16(log_a)
    tri = tri.astype(BF)
    b_all = (jnp.dot(tri, la_hi, preferred_element_type=jnp.float32)
             + jnp.dot(tri, la_lo, preferred_element_type=jnp.float32))
    col = lax.broadcasted_iota(jnp.int32, (sub, c), 1)
    row = lax.broadcasted_iota(jnp.int32, (sub, c), 0)
    for h in range(GLA_HEADS):
        q = q_ref[:, h * dk:(h + 1) * dk] * (dk ** -0.5)
        k = k_ref[:, h * dk:(h + 1) * dk]
        v = v_ref[:, h * dv:(h + 1) * dv]
        b = b_all[:, h * dk:(h + 1) * dk]
        a_rows = []
        for blk in range(c // sub):
            lo = blk * sub
            q_i, b_i, k_i = q[lo:lo + sub], b[lo:lo + sub], k[lo:lo + sub]
            if blk > 0:
                b_prev = b[lo - 1:lo]
                q_t = q_i * jnp.exp(b_i - b_prev)
                k_t = k * jnp.exp(jnp.minimum(b_prev - b, 0.0))
                acc = jnp.where(col < lo, _dot_nt(q_t.astype(BF), k_t.astype(BF)), 0.0)
            else:
                acc = jnp.zeros((sub, c), jnp.float32)
            for j in range(sub):
                t = q_i * k_i[j:j + 1] * jnp.exp(jnp.minimum(b_i - b_i[j:j + 1], 0.0))
                cs = jnp.sum(t, axis=-1, keepdims=True)
                acc = acc + jnp.where((col == lo + j) & (row >= j), cs, 0.0)
            a_rows.append(acc)
        attn = jnp.concatenate(a_rows, axis=0)
        st = state_ref[h]
        o = jnp.dot(attn.astype(BF), v.astype(BF), preferred_element_type=jnp.float32)
        o = o + _dot_nt((q * jnp.exp(b)).astype(BF), st.astype(BF))
        b_last = b[c - 1:c]
        k_d = k * jnp.exp(b_last - b)
        upd = lax.dot_general(v.astype(BF), k_d.astype(BF), (((0,), (0,)), ((), ())),
                              preferred_element_type=jnp.float32)
        state_ref[h] = st * jnp.exp(b_last) + upd
        o = o * lax.rsqrt(jnp.mean(o * o, axis=-1, keepdims=True) + LN_EPS)
        r = r_ref[:, h * dv:(h + 1) * dv]
        o_ref[:, h * dv:(h + 1) * dv] = o * g_ref[...] * (r / (1.0 + jnp.exp(-r)))


def gla_mixer_pallas(q, k, v, r, lr, w_alpha2, b_alpha, norm_g):
    bsz, seq, _ = q.shape
    c = GLA_CHUNK

    def tok(width):
        return pl.BlockSpec((None, c, width), lambda b, i: (b, i, 0))

    def whole(shape):
        return pl.BlockSpec(shape, lambda b, i: (0,) * len(shape))

    return pl.pallas_call(
        _gla_kernel,
        out_shape=jax.ShapeDtypeStruct((bsz, seq, B_V), jnp.float32),
        grid=(bsz, seq // c),
        in_specs=[tok(B_K), tok(B_K), tok(B_V), tok(B_V), tok(GLA_RANK),
                  whole((GLA_RANK, B_K)), whole((1, B_K)), whole((1, GLA_DV))],
        out_specs=tok(B_V),
        scratch_shapes=[pltpu.VMEM((GLA_HEADS, GLA_DV, GLA_DK), jnp.float32)],
        compiler_params=_cparams(2),
        name="gla_mixer",
    )(q, k, v, r, lr, w_alpha2, b_alpha.reshape(1, -1), norm_g.reshape(1, -1))


CMP_HALF = NSA_CMP_STRIDE * HEAD_DIM


def _gelu_tanh(x):
    return 0.5 * x * (1.0 + jnp.tanh(np.sqrt(2.0 / np.pi) * (x + 0.044715 * (x * x * x))))


def _compress_kernel(hk_ref, hv_ref, pek_ref, w1k_ref, w2k_ref, pev_ref, w1v_ref, w2v_ref, kc_ref, vc_ref):
    for h_ref, pe_ref, w1_ref, w2_ref, o_ref in ((hk_ref, pek_ref, w1k_ref, w2k_ref, kc_ref),
                                                 (hv_ref, pev_ref, w1v_ref, w2v_ref, vc_ref)):
        w1a = w1_ref[0].astype(BF)
        w1b = w1_ref[1].astype(BF)
        pe = pe_ref[...].astype(BF)
        pe_term = (jnp.dot(pe[:, :CMP_HALF], w1a, preferred_element_type=jnp.float32)
                   + jnp.dot(pe[:, CMP_HALF:], w1b, preferred_element_type=jnp.float32))
        w2 = w2_ref[...].astype(BF)
        for g in range(NSA_KV_GROUPS):
            hb = h_ref[g].astype(BF)
            y1 = jnp.dot(hb, w1a, preferred_element_type=jnp.float32)
            y2 = jnp.dot(hb, w1b, preferred_element_type=jnp.float32)
            n_half = y2.shape[0]
            pre = y1 + pltpu.roll(y2, n_half - 1, axis=0) + pe_term
            out = jnp.dot(_gelu_tanh(pre).astype(BF), w2, preferred_element_type=jnp.float32)
            o_ref[g] = jnp.concatenate([out, out], axis=1)


def nsa_compress_pallas(u, k_col, v_col, pe_k, w1_k, w2_k, pe_v, w1_v, w2_v):
    bsz, seq, _ = u.shape
    n_half = seq // NSA_CMP_STRIDE

    def halves(col):
        t = u[..., col:col + C_KV].reshape(bsz, n_half, NSA_CMP_STRIDE, NSA_KV_GROUPS, HEAD_DIM)
        return t.transpose(0, 3, 1, 2, 4).reshape(bsz, NSA_KV_GROUPS, n_half, CMP_HALF)

    def whole(shape):
        return pl.BlockSpec(shape, lambda b: (0,) * len(shape))

    hspec = pl.BlockSpec((None, NSA_KV_GROUPS, n_half, CMP_HALF), lambda b: (b, 0, 0, 0))
    ospec = pl.BlockSpec((None, NSA_KV_GROUPS, n_half, V7X_LANES), lambda b: (b, 0, 0, 0))
    wspecs = [whole((1, 2 * CMP_HALF)), whole((2, CMP_HALF, NSA_CMP_HIDDEN)), whole((NSA_CMP_HIDDEN, HEAD_DIM))]
    sds = jax.ShapeDtypeStruct((bsz, NSA_KV_GROUPS, n_half, V7X_LANES), jnp.float32)
    return pl.pallas_call(
        _compress_kernel,
        out_shape=(sds, sds),
        grid=(bsz,),
        in_specs=[hspec, hspec] + wspecs + wspecs,
        out_specs=(ospec, ospec),
        compiler_params=_cparams(1),
        name="nsa_compress",
    )(halves(k_col), halves(v_col),
      pe_k.reshape(1, -1), w1_k.reshape(2, CMP_HALF, NSA_CMP_HIDDEN), w2_k,
      pe_v.reshape(1, -1), w1_v.reshape(2, CMP_HALF, NSA_CMP_HIDDEN), w2_v)


CMP_TQ = 256


def _overlap_matrix(n_cmp_pad, n_blk):
    c0 = np.arange(n_cmp_pad)[:, None] * NSA_CMP_STRIDE
    s0 = np.arange(n_blk)[None, :] * NSA_SEL_LEN
    return ((c0 < s0 + NSA_SEL_LEN) & (c0 + NSA_CMP_LEN > s0)).astype(np.float32)


def _cmp_attn_kernel(q_ref, kc_ref, vc_ref, ov_ref, o_ref, m_ref):
    i = pl.program_id(2)
    tq = CMP_TQ
    n_c = kc_ref.shape[0]
    n_blk = ov_ref.shape[0]
    kc = kc_ref[...].astype(BF)
    vc = vc_ref[...].astype(BF)
    pos = i * tq + lax.broadcasted_iota(jnp.int32, (tq, n_c), 0)
    c_end = lax.broadcasted_iota(jnp.int32, (tq, n_c), 1) * NSA_CMP_STRIDE + (NSA_CMP_LEN - 1)
    cvalid = c_end <= pos
    lane = lax.broadcasted_iota(jnp.int32, (tq, V7X_LANES), 1)
    p_sum = jnp.zeros((tq, n_c), jnp.float32)
    outs = []
    for rr in range(NSA_GROUP_HEADS):
        qc = q_ref[:, (rr // 2) * V7X_LANES:(rr // 2 + 1) * V7X_LANES] * (HEAD_DIM ** -0.5)
        qh = jnp.where(_lane_half_mask(qc.shape, rr % 2), qc, 0.0).astype(BF)
        s = jnp.where(cvalid, _dot_nt(qh, kc), NEG_BIG)
        m = jnp.max(s, axis=-1, keepdims=True)
        p = jnp.where(cvalid, jnp.exp(s - m), 0.0)
        p = p / jnp.maximum(jnp.sum(p, axis=-1, keepdims=True), 1e-30)
        p_sum = p_sum + p
        outs.append(jnp.dot(p.astype(BF), vc, preferred_element_type=jnp.float32))
    for c in range(NSA_GROUP_HEADS // 2):
        o_ref[:, c * V7X_LANES:(c + 1) * V7X_LANES] = jnp.where(lane < HEAD_DIM, outs[2 * c], outs[2 * c + 1])
    ps_hi, ps_lo = _split_bf16(p_sum)
    imp = _dot_nt(ov_ref[...], ps_hi) + _dot_nt(ov_ref[...], ps_lo)
    j = lax.broadcasted_iota(jnp.int32, (n_blk, tq), 0)
    q_blk = (i * tq + lax.broadcasted_iota(jnp.int32, (n_blk, tq), 1)) // NSA_SEL_LEN
    forced = (j == 0) | (j == q_blk) | (j == q_blk - 1)
    imp = jnp.where(forced, jnp.inf, jnp.where(j > q_blk, -jnp.inf, imp))
    rank = jnp.zeros((n_blk, tq), jnp.float32)
    for jj in range(n_blk):
        other = imp[jj:jj + 1, :]
        beats = (other > imp) | ((other == imp) & (jj < j))
        rank = rank + jnp.where(beats, 1.0, 0.0)
    m_ref[...] = jnp.where(rank < float(min(NSA_N_SELECT, n_blk)), 1.0, 0.0).astype(m_ref.dtype)


def nsa_cmp_attention_pallas(u, q_col, kc, vc):
    bsz, seq, _ = u.shape
    n_c = kc.shape[2]
    n_blk = seq // NSA_SEL_LEN
    gw = NSA_GROUP_HEADS * HEAD_DIM
    qb = q_col // gw
    cspec = pl.BlockSpec((None, None, n_c, V7X_LANES), lambda b, g, i: (b, g, 0, 0))
    return pl.pallas_call(
        _cmp_attn_kernel,
        out_shape=(jax.ShapeDtypeStruct((bsz, seq, NSA_HEADS * HEAD_DIM), jnp.float32),
                   jax.ShapeDtypeStruct((bsz, NSA_KV_GROUPS, n_blk, seq), BF)),
        grid=(bsz, NSA_KV_GROUPS, seq // CMP_TQ),
        in_specs=[pl.BlockSpec((None, CMP_TQ, gw), lambda b, g, i: (b, i, qb + g)),
                  cspec, cspec,
                  pl.BlockSpec((n_blk, n_c), lambda b, g, i: (0, 0))],
        out_specs=(pl.BlockSpec((None, CMP_TQ, gw), lambda b, g, i: (b, i, g)),
                   pl.BlockSpec((None, None, n_blk, CMP_TQ), lambda b, g, i: (b, g, 0, i))),
        compiler_params=_cparams(3),
        name="nsa_cmp_attention",
    )(u, kc, vc, jnp.asarray(_overlap_matrix(n_c, n_blk).T, BF))


MERGE_TM = 256


def _layer_norm_rows(x, g, b):
    mu = jnp.mean(x, axis=-1, keepdims=True)
    xc = x - mu
    var = jnp.mean(xc * xc, axis=-1, keepdims=True)
    return xc * lax.rsqrt(var + LN_EPS) * g + b


def _gate_expand_matrices():
    e = np.zeros((NSA_BRANCHES, V7X_LANES, NSA_HEADS * HEAD_DIM), np.float32)
    for br in range(NSA_BRANCHES):
        for h in range(NSA_HEADS):
            e[br, GLA_RANK + NSA_BRANCHES * h + br, h * HEAD_DIM:(h + 1) * HEAD_DIM] = 1.0
    return e


def _sigmoid(x):
    return 1.0 / (1.0 + jnp.exp(-x))


def _merge_kernel(x_ref, o0_ref, o1_ref, o2_ref, l0_ref, l1_ref, l2_ref, yb_ref, oc_ref, os_ref, ow_ref,
                  us_ref, mg_ref, ex_ref, wa_ref, wb_ref, wc_ref, wo_ref, g_ref, b_ref, out_ref):
    l0, l1, l2 = l0_ref[...], l1_ref[...], l2_ref[...]
    lm = jnp.maximum(jnp.maximum(l0, l1), l2)
    e0, e1, e2 = jnp.exp(l0 - lm), jnp.exp(l1 - lm), jnp.exp(l2 - lm)
    y_a = (e0 * o0_ref[...] + e1 * o1_ref[...] + e2 * o2_ref[...]) / (e0 + e1 + e2)
    sg = _sigmoid(us_ref[...])
    y_c = (_dot_split_lhs(sg, ex_ref[0]) * oc_ref[...] + _dot_split_lhs(sg, ex_ref[1]) * os_ref[...]
           + _dot_split_lhs(sg, ex_ref[2]) * ow_ref[...])
    d = D_MODEL
    f = functools.partial(jnp.dot, preferred_element_type=jnp.float32)
    merged = (_sigmoid(mg_ref[:, :d]) * f(y_a.astype(BF), wa_ref[...])
              + _sigmoid(mg_ref[:, d:2 * d]) * f(yb_ref[...].astype(BF), wb_ref[...])
              + _sigmoid(mg_ref[:, 2 * d:]) * f(y_c.astype(BF), wc_ref[...]))
    mix = f(merged.astype(BF), wo_ref[...])
    out_ref[...] = _layer_norm_rows(DEEPNORM_ALPHA * x_ref[...] + mix, g_ref[...], b_ref[...])


def mixer_merge_pallas(x, dil_o, dil_lse, y_b, o_cmp, o_sel, o_win, u_small, m_g, w_br_a, w_br_b, w_br_c, w_o_mix, ln_g, ln_b):
    n, d = x.shape
    tm = MERGE_TM
    aw = DIL_HEADS * HEAD_DIM

    def tok(width):
        return pl.BlockSpec((tm, width), lambda i: (i, 0))

    def whole(shape):
        return pl.BlockSpec(shape, lambda i: (0,) * len(shape))

    ex = jnp.asarray(_gate_expand_matrices(), BF)
    return pl.pallas_call(
        _merge_kernel,
        out_shape=jax.ShapeDtypeStruct((n, d), jnp.float32),
        grid=(n // tm,),
        in_specs=[tok(d)] + [tok(aw)] * 6 + [tok(B_V)] + [tok(C_Q)] * 3 + [tok(V7X_LANES), tok(N_BRANCHES * d),
                  whole(ex.shape), whole((aw, d)), whole((B_V, d)), whole((C_Q, d)), whole((d, d)),
                  whole((1, d)), whole((1, d))],
        out_specs=tok(d),
        compiler_params=_cparams(1),
        name="mixer_merge",
    )(x, *dil_o, *dil_lse, y_b, o_cmp, o_sel, o_win, u_small, m_g, ex,
      w_br_a.astype(BF), w_br_b.astype(BF), w_br_c.astype(BF), w_o_mix.astype(BF),
      ln_g.reshape(1, d), ln_b.reshape(1, d))


XATTN_TM = 256


def _xattn_kernel(x_ref, k_ref, v_ref, wq_ref, wo_ref, g_ref, b_ref, o_ref):
    x = x_ref[...]
    d = x.shape[-1]
    hd = d // XATTN_HEADS
    q = jnp.dot(x.astype(BF), wq_ref[...], preferred_element_type=jnp.float32)
    outs = []
    for h in range(XATTN_HEADS):
        sl = slice(h * hd, (h + 1) * hd)
        s = _dot_nt((q[:, sl] * (hd ** -0.5)).astype(BF), k_ref[:, sl].astype(BF))
        p = jnp.exp(s - jnp.max(s, axis=-1, keepdims=True))
        den = jnp.sum(p, axis=-1, keepdims=True)
        outs.append(jnp.dot(p.astype(BF), v_ref[:, sl].astype(BF), preferred_element_type=jnp.float32) / den)
    o = jnp.concatenate(outs, axis=1)
    xa = jnp.dot(o.astype(BF), wo_ref[...], preferred_element_type=jnp.float32)
    o_ref[...] = _layer_norm_rows(DEEPNORM_ALPHA * x + xa, g_ref[...], b_ref[...])


def cross_attention_block_pallas(x, mem, w_xq, w_xk, w_xv, w_xo, ln_g, ln_b):
    bsz, seq, d = x.shape
    mem_len = mem.shape[1]
    mem2 = mem.reshape(-1, d)
    k = _mm(mem2, w_xk)
    v = _mm(mem2, w_xv)
    n_t = seq // XATTN_TM

    def whole(shape):
        return pl.BlockSpec(shape, lambda b, i: (0,) * len(shape))

    out = pl.pallas_call(
        _xattn_kernel,
        out_shape=jax.ShapeDtypeStruct((bsz * seq, d), jnp.float32),
        grid=(bsz, n_t),
        in_specs=[pl.BlockSpec((XATTN_TM, d), lambda b, i: (b * n_t + i, 0)),
                  pl.BlockSpec((mem_len, d), lambda b, i: (b, 0)),
                  pl.BlockSpec((mem_len, d), lambda b, i: (b, 0)),
                  whole((d, d)), whole((d, d)), whole((1, d)), whole((1, d))],
        out_specs=pl.BlockSpec((XATTN_TM, d), lambda b, i: (b * n_t + i, 0)),
        compiler_params=_cparams(2),
        name="cross_attention_block",
    )(x.reshape(-1, d), k, v, w_xq.astype(BF), w_xo.astype(BF), ln_g.reshape(1, d), ln_b.reshape(1, d))
    return out.reshape(bsz, seq, d)


def _ln_kernel(x_ref, g_ref, b_ref, o_ref):
    o_ref[...] = _layer_norm_rows(x_ref[...], g_ref[...], b_ref[...])


def layer_norm_pallas(x, g, b):
    shape = x.shape
    d = shape[-1]
    x2 = x.reshape(-1, d)
    tm = _pick_tile(x2.shape[0], (1024, 512, 256, 128, 64, 32, 16, 8))
    out = pl.pallas_call(
        _ln_kernel,
        out_shape=jax.ShapeDtypeStruct(x2.shape, jnp.float32),
        grid=(x2.shape[0] // tm,),
        in_specs=[pl.BlockSpec((tm, d), lambda i: (i, 0)),
                  pl.BlockSpec((1, d), lambda i: (0, 0)), pl.BlockSpec((1, d), lambda i: (0, 0))],
        out_specs=pl.BlockSpec((tm, d), lambda i: (i, 0)),
        compiler_params=_cparams(1),
        name="layer_norm",
    )(x2, g.reshape(1, d), b.reshape(1, d))
    return out.reshape(shape)


MOE_TB = 256


def _moe_ffn_kernel(be_ref, nused_ref, x_ref, wgu_ref, bgu_ref, wd_ref, bd_ref, o_ref, wgu_bf, wd_bf):
    i = pl.program_id(0)
    changed = jnp.logical_or(i == 0, be_ref[i] != be_ref[jnp.maximum(i - 1, 0)])

    @pl.when(changed)
    def _():
        wgu_bf[...] = wgu_ref[...].astype(BF)
        wd_bf[...] = wd_ref[...].astype(BF)

    @pl.when(i < nused_ref[0])
    def _():
        gu = jnp.dot(x_ref[...].astype(BF), wgu_bf[...], preferred_element_type=jnp.float32) + bgu_ref[...]
        gate = jnp.minimum(gu[:, :D_FF], SWIGLU_LIMIT)
        up = jnp.clip(gu[:, D_FF:], -SWIGLU_LIMIT, SWIGLU_LIMIT)
        act = (up + 1.0) * gate * (1.0 / (1.0 + jnp.exp(-SWIGLU_ALPHA * gate)))
        o_ref[...] = jnp.dot(act.astype(BF), wd_bf[...], preferred_element_type=jnp.float32) + bd_ref[...]

    @pl.when(i >= nused_ref[0])
    def _():
        o_ref[...] = jnp.zeros(o_ref.shape, o_ref.dtype)


def moe_expert_ffn_pallas(buf, block_e, n_used, w_gu, b_gu, w_down, b_down):
    n_rows, d = buf.shape
    n_e = w_gu.shape[0]
    grid_spec = pltpu.PrefetchScalarGridSpec(
        num_scalar_prefetch=2,
        grid=(n_rows // MOE_TB,),
        in_specs=[
            pl.BlockSpec((MOE_TB, d), lambda i, be, nu: (i, 0)),
            pl.BlockSpec((None, d, 2 * D_FF), lambda i, be, nu: (be[i], 0, 0)),
            pl.BlockSpec((None, 1, 2 * D_FF), lambda i, be, nu: (be[i], 0, 0)),
            pl.BlockSpec((None, D_FF, d), lambda i, be, nu: (be[i], 0, 0)),
            pl.BlockSpec((None, 1, d), lambda i, be, nu: (be[i], 0, 0)),
        ],
        out_specs=pl.BlockSpec((MOE_TB, d), lambda i, be, nu: (i, 0)),
        scratch_shapes=[pltpu.VMEM((d, 2 * D_FF), BF), pltpu.VMEM((D_FF, d), BF)],
    )
    return pl.pallas_call(
        _moe_ffn_kernel,
        out_shape=jax.ShapeDtypeStruct((n_rows, d), jnp.float32),
        grid_spec=grid_spec,
        compiler_params=_cparams(1),
        name="moe_expert_ffn",
    )(block_e, n_used, buf, w_gu, b_gu.reshape(n_e, 1, -1), w_down, b_down.reshape(n_e, 1, -1))


def token_mixer_block(h, w_in, b_in, w_alpha2, b_alpha, gla_norm_g, cmp_pe_k, cmp_w1_k, cmp_w2_k,
                      cmp_pe_v, cmp_w1_v, cmp_w2_v, w_br_a, w_br_b, w_br_c, w_o_mix, ln_g, ln_b):
    bsz, seq, d = h.shape
    offs = np.concatenate([[0], np.cumsum(IN_WIDTHS)]).tolist()
    h2 = h.reshape(-1, d)

    def proj(lo, hi):
        return _mm(h2, w_in[:, offs[lo]:offs[hi]], b_in[offs[lo]:offs[hi]]).reshape(bsz, seq, -1)

    u_ab = proj(0, 7)
    _, _, _, b_q, b_k, b_v, b_r = jnp.split(u_ab, np.cumsum(IN_WIDTHS[:7])[:-1].tolist(), axis=-1)
    w_small = jnp.concatenate([w_in[:, offs[7]:offs[8]], w_in[:, offs[15]:offs[16]]], axis=1)
    b_small = jnp.concatenate([b_in[offs[7]:offs[8]], b_in[offs[15]:offs[16]]])
    n_small = w_small.shape[1]
    w_small = jnp.pad(w_small, ((0, 0), (0, V7X_LANES - n_small)))
    b_small = jnp.pad(b_small, (0, V7X_LANES - n_small))
    u_small = _mm(h2, w_small, b_small)
    b_lr = u_small[:, :GLA_RANK].reshape(bsz, seq, GLA_RANK)
    u_c = proj(8, 15)
    m_g = _mm(h2, w_in[:, offs[16]:offs[17]], b_in[offs[16]:offs[17]])
    dil_o, dil_lse = dilated_attention_pallas(u_ab, 0, A_W, 2 * A_W)
    y_b = gla_mixer_pallas(b_q, b_k, b_v, b_r, b_lr, w_alpha2, b_alpha, gla_norm_g)
    kc, vc = nsa_compress_pallas(u_c, C_Q, C_Q + C_KV, cmp_pe_k, cmp_w1_k, cmp_w2_k, cmp_pe_v, cmp_w1_v, cmp_w2_v)
    o_cmp, selmask = nsa_cmp_attention_pallas(u_c, 0, kc, vc)
    o_sel = nsa_gqa_attention_pallas(u_c, 0, C_Q + 2 * C_KV, C_Q + 3 * C_KV, selmask)
    o_win = nsa_gqa_attention_pallas(u_c, 0, C_Q + 4 * C_KV, C_Q + 5 * C_KV)

    def flat(t):
        return t.reshape(bsz * seq, -1)

    out = mixer_merge_pallas(h2, [flat(t) for t in dil_o], [flat(t) for t in dil_lse], flat(y_b), flat(o_cmp),
                             flat(o_sel), flat(o_win), u_small, m_g, w_br_a, w_br_b, w_br_c, w_o_mix, ln_g, ln_b)
    return out.reshape(bsz, seq, d)


def moe_ffn(h, w_router, b_router, w_gu, b_gu, w_down, b_down):
    bsz, seq, d = h.shape
    t = h.reshape(-1, d)
    n_assign = t.shape[0] * TOP_K
    logits = t @ w_router + b_router
    top_logits, top_idx = lax.top_k(logits, TOP_K)
    gates = jax.nn.softmax(top_logits.astype(jnp.float32), axis=-1).astype(t.dtype)
    flat_e = top_idx.reshape(-1)
    order = jnp.argsort(flat_e)
    sorted_e = flat_e[order]
    tok_of = order // TOP_K
    counts = jnp.bincount(flat_e, length=N_EXPERTS)
    padded = (counts + MOE_TB - 1) // MOE_TB * MOE_TB
    start = jnp.cumsum(counts) - counts
    pend = jnp.cumsum(padded)
    pstart = pend - padded
    dest = pstart[sorted_e] + (jnp.arange(n_assign) - start[sorted_e])
    n_rows = n_assign + N_EXPERTS * MOE_TB
    n_blocks = n_rows // MOE_TB
    row_tok = jnp.zeros((n_rows,), jnp.int32).at[dest].set(tok_of.astype(jnp.int32))
    dest_tok = jnp.zeros((n_assign,), jnp.int32).at[order].set(dest.astype(jnp.int32))
    buf = t[row_tok]
    block_e = jnp.minimum(jnp.searchsorted(pend, jnp.arange(n_blocks) * MOE_TB, side='right'), N_EXPERTS - 1)
    n_used = (pend[-1:] // MOE_TB).astype(jnp.int32)
    out = moe_expert_ffn_pallas(buf, block_e.astype(jnp.int32), n_used, w_gu, b_gu, w_down, b_down)
    y = jnp.sum(out[dest_tok].reshape(-1, TOP_K, d) * gates[:, :, None], axis=1)
    return y.reshape(bsz, seq, d)


def kernel(x, mem, ln0_g, ln0_b, w_in, b_in, w_alpha2, b_alpha, gla_norm_g, cmp_pe_k, cmp_w1_k, cmp_w2_k, cmp_pe_v, cmp_w1_v, cmp_w2_v, w_br_a, w_br_b, w_br_c, w_o_mix, ln1_g, ln1_b, w_xq, w_xk, w_xv, w_xo, ln2_g, ln2_b, w_router, b_router, w_gu, b_gu, w_down, b_down, ln3_g, ln3_b):
    alpha = DEEPNORM_ALPHA
    x = layer_norm_pallas(x, ln0_g, ln0_b)
    for li in range(DEPTH):
        x = token_mixer_block(x, w_in[li], b_in[li], w_alpha2[li], b_alpha[li], gla_norm_g[li],
                              cmp_pe_k[li], cmp_w1_k[li], cmp_w2_k[li], cmp_pe_v[li], cmp_w1_v[li], cmp_w2_v[li],
                              w_br_a[li], w_br_b[li], w_br_c[li], w_o_mix[li], ln1_g[li], ln1_b[li])
        x = cross_attention_block_pallas(x, mem, w_xq[li], w_xk[li], w_xv[li], w_xo[li], ln2_g[li], ln2_b[li])
        ff = moe_ffn(x, w_router[li], b_router[li], w_gu[li], b_gu[li], w_down[li], b_down[li])
        x = layer_norm_pallas(alpha * x + ff, ln3_g[li], ln3_b[li])
    return x
```

```python
import functools

import jax
import jax.numpy as jnp
import numpy as np
from jax import lax
from jax.experimental import pallas as pl
from jax.experimental.pallas import tpu as pltpu

D_MODEL = 1024
DEPTH = 2
HEAD_DIM = 64
N_BRANCHES = 3
LN_EPS = 1e-5

DIL_GROUPS = ((128, 1), (512, 4), (2048, 16))
DIL_HEADS = 4

GLA_HEADS = 4
GLA_DK = D_MODEL // 2 // GLA_HEADS
GLA_DV = D_MODEL // GLA_HEADS
GLA_RANK = 16
GLA_TAU = 16.0
GLA_CHUNK = 64

NSA_HEADS = 16
NSA_KV_GROUPS = 4
NSA_GROUP_HEADS = NSA_HEADS // NSA_KV_GROUPS
NSA_BRANCHES = 3
NSA_CMP_LEN = 32
NSA_CMP_STRIDE = 16
NSA_CMP_HIDDEN = 2 * HEAD_DIM
NSA_SEL_LEN = 64
NSA_N_SELECT = 16
NSA_WINDOW = 512

XATTN_HEADS = 4

N_EXPERTS = 32
TOP_K = 4
D_FF = D_MODEL
SWIGLU_LIMIT = 7.0
SWIGLU_ALPHA = 1.702

DEEPNORM_ALPHA = (2 * DEPTH) ** 0.25

A_W = len(DIL_GROUPS) * DIL_HEADS * HEAD_DIM
B_K = GLA_HEADS * GLA_DK
B_V = GLA_HEADS * GLA_DV
C_Q = NSA_HEADS * HEAD_DIM
C_KV = NSA_KV_GROUPS * HEAD_DIM
IN_WIDTHS = (A_W, A_W, A_W,
             B_K, B_K, B_V, B_V, GLA_RANK,
             C_Q, C_KV, C_KV, C_KV, C_KV, C_KV, C_KV, NSA_HEADS * NSA_BRANCHES,
             N_BRANCHES * D_MODEL)

V7X_VMEM_LIMIT_BYTES = 48 * 1024 * 1024
V7X_LANES = 128
NEG_BIG = -1e30
BF = jnp.bfloat16


def _cparams(n_axes):
    return pltpu.CompilerParams(dimension_semantics=("arbitrary",) * n_axes,
                                vmem_limit_bytes=V7X_VMEM_LIMIT_BYTES)


def _split_bf16(x):
    hi = x.astype(BF)
    lo = (x - hi.astype(jnp.float32)).astype(BF)
    return hi, lo


def _dot_split(a, b):
    a_hi, a_lo = _split_bf16(a)
    b_hi, b_lo = _split_bf16(b)
    f = functools.partial(jnp.dot, preferred_element_type=jnp.float32)
    return f(a_hi, b_hi) + (f(a_hi, b_lo) + f(a_lo, b_hi))


def _dot_split_lhs(a, b_exact):
    a_hi, a_lo = _split_bf16(a)
    f = functools.partial(jnp.dot, preferred_element_type=jnp.float32)
    return f(a_hi, b_exact) + f(a_lo, b_exact)


def _dot_nt(a, b):
    return lax.dot_general(a, b, (((1,), (1,)), ((), ())), preferred_element_type=jnp.float32)


def _mm_kernel(x_ref, w_ref, b_ref, o_ref):
    acc = jnp.dot(x_ref[...].astype(BF), w_ref[...].astype(BF), preferred_element_type=jnp.float32)
    o_ref[...] = (acc + b_ref[...]).astype(o_ref.dtype)


def _pick_tile(n, candidates):
    for c in candidates:
        if n % c == 0:
            return c
    return n


def _mm(x, w, b=None, out_dtype=jnp.float32):
    m, k = x.shape
    n = w.shape[1]
    if b is None:
        b = jnp.zeros((n,), jnp.float32)
    tm = _pick_tile(m, (1024, 512, 256, 128, 64, 32, 16, 8))
    tn = _pick_tile(n, (768, 512, 384, 256, 128))
    return pl.pallas_call(
        _mm_kernel,
        out_shape=jax.ShapeDtypeStruct((m, n), out_dtype),
        grid=(n // tn, m // tm),
        in_specs=[
            pl.BlockSpec((tm, k), lambda j, i: (i, 0)),
            pl.BlockSpec((k, tn), lambda j, i: (0, j)),
            pl.BlockSpec((1, tn), lambda j, i: (0, j)),
        ],
        out_specs=pl.BlockSpec((tm, tn), lambda j, i: (i, j)),
        compiler_params=_cparams(2),
        name="dense_proj",
    )(x, w, b.reshape(1, n))


def _mm3(x, w, b=None):
    lead = x.shape[:-1]
    return _mm(x.reshape(-1, x.shape[-1]), w, b).reshape(*lead, w.shape[1])


GQA_TQ = 128
GQA_TK = 256


def _half_select(x, want_half, have_half):
    if want_half != have_half:
        x = pltpu.roll(x, HEAD_DIM, axis=1)
    return x


def _lane_half_mask(shape, half):
    lane = lax.broadcasted_iota(jnp.int32, shape, 1)
    return (lane >= half * HEAD_DIM) & (lane < (half + 1) * HEAD_DIM)


def _make_gqa_kernel(select, max_dist, n_outer):
    r, tq, tk = NSA_GROUP_HEADS, GQA_TQ, GQA_TK

    def kern(q_ref, k_ref, v_ref, *rest):
        if select:
            m_ref, e_ref, o_ref, mx_ref, l_ref, acc_ref = rest
        else:
            o_ref, mx_ref, l_ref, acc_ref = rest
        i = pl.program_id(n_outer)
        dist0 = i * tq + lax.broadcasted_iota(jnp.int32, (tk, tq), 1) - lax.broadcasted_iota(jnp.int32, (tk, tq), 0)
        qs = []
        for half in range(2):
            parts = []
            for rr in range(r):
                hh = half * r + rr
                qc = q_ref[:, (hh // 2) * V7X_LANES:(hh // 2 + 1) * V7X_LANES] * (HEAD_DIM ** -0.5)
                qc = _half_select(qc, half, hh % 2)
                parts.append(jnp.where(_lane_half_mask(qc.shape, half), qc, 0.0).astype(BF))
            qs.append(jnp.concatenate(parts, axis=0))
        mx_ref[...] = jnp.full(mx_ref.shape, NEG_BIG, jnp.float32)
        l_ref[...] = jnp.zeros(l_ref.shape, jnp.float32)
        acc_ref[...] = jnp.zeros(acc_ref.shape, jnp.float32)

        def body(j, carry):
            k0 = pl.multiple_of(j * tk, tk)
            kt = k_ref[pl.ds(k0, tk), :].astype(BF)
            vt = v_ref[pl.ds(k0, tk), :].astype(BF)
            dist = dist0 - k0
            in_band = (dist >= 0) if select else ((dist >= 0) & (dist <= max_dist))
            for half in range(2):
                if select:
                    blk = jnp.dot(e_ref[j], m_ref[half], preferred_element_type=jnp.float32)
                    valid = (blk > 0.5) & in_band
                else:
                    valid = in_band
                bias = jnp.where(valid, 0.0, NEG_BIG)
                s = _dot_nt(kt, qs[half]) + jnp.concatenate([bias] * r, axis=1)
                m_old = mx_ref[half]
                m_new = jnp.maximum(m_old, jnp.max(s, axis=0, keepdims=True))
                p = jnp.exp(s - m_new)
                corr = jnp.exp(m_old - m_new)
                l_ref[half] = corr * l_ref[half] + jnp.sum(p, axis=0, keepdims=True)
                pv = lax.dot_general(vt, p.astype(BF), (((0,), (0,)), ((), ())), preferred_element_type=jnp.float32)
                acc_ref[half] = corr * acc_ref[half] + pv
                mx_ref[half] = m_new
            return carry

        lo = 0 if select else jnp.maximum(i * tq - max_dist, 0) // tk
        lax.fori_loop(lo, (i * tq + tq - 1) // tk + 1, body, 0)
        outs = []
        for half in range(2):
            o = (acc_ref[half] / l_ref[half]).T
            for rr in range(r):
                outs.append(_half_select(o[rr * tq:(rr + 1) * tq], (half * r + rr) % 2, half))
        lane = lax.broadcasted_iota(jnp.int32, (tq, V7X_LANES), 1)
        for c in range(r):
            o_ref[:, c * V7X_LANES:(c + 1) * V7X_LANES] = jnp.where(lane < HEAD_DIM, outs[2 * c], outs[2 * c + 1])

    return kern


def _sel_expand_matrix(seq):
    n_blk = seq // NSA_SEL_LEN
    e = (np.arange(seq)[:, None] // NSA_SEL_LEN == np.arange(n_blk)[None, :]).astype(np.float32)
    return e.reshape(seq // GQA_TK, GQA_TK, n_blk)


def nsa_gqa_attention_pallas(u, q_col, k_col, v_col, selmask=None):
    bsz, seq, _ = u.shape
    select = selmask is not None
    n_pairs = NSA_KV_GROUPS // 2
    qw = 2 * NSA_GROUP_HEADS * HEAD_DIM
    qb, kb, vb = q_col // qw, k_col // V7X_LANES, v_col // V7X_LANES
    rows = NSA_GROUP_HEADS * GQA_TQ
    in_specs = [
        pl.BlockSpec((None, GQA_TQ, qw), lambda b, gp, i: (b, i, qb + gp)),
        pl.BlockSpec((None, seq, V7X_LANES), lambda b, gp, i: (b, 0, kb + gp)),
        pl.BlockSpec((None, seq, V7X_LANES), lambda b, gp, i: (b, 0, vb + gp)),
    ]
    args = [u, u, u]
    if select:
        n_blk = selmask.shape[2]
        in_specs += [pl.BlockSpec((None, 2, n_blk, GQA_TQ), lambda b, gp, i: (b, gp, 0, i)),
                     pl.BlockSpec((seq // GQA_TK, GQA_TK, n_blk), lambda b, gp, i: (0, 0, 0))]
        args += [selmask, jnp.asarray(_sel_expand_matrix(seq), BF)]
    return pl.pallas_call(
        _make_gqa_kernel(select, NSA_WINDOW - 1, 2),
        out_shape=jax.ShapeDtypeStruct((bsz, seq, NSA_HEADS * HEAD_DIM), jnp.float32),
        grid=(bsz, n_pairs, seq // GQA_TQ),
        in_specs=in_specs,
        out_specs=pl.BlockSpec((None, GQA_TQ, qw), lambda b, gp, i: (b, i, gp)),
        scratch_shapes=[pltpu.VMEM((2, 1, rows), jnp.float32),
                        pltpu.VMEM((2, 1, rows), jnp.float32),
                        pltpu.VMEM((2, V7X_LANES, rows), jnp.float32)],
        compiler_params=_cparams(3),
        name="nsa_selected_attention" if select else "nsa_window_attention",
    )(*args)


DIL_TILE = 128


def _make_dil_kernel(dil, max_dist, seq):
    t = DIL_TILE
    n_sub = seq // dil // t
    n_kv = min(-(-max_dist // t) + 1, n_sub)

    def rows(first, size):
        return pl.ds(first, size, stride=dil) if dil > 1 else pl.ds(first, size)

    def kern(q_ref, k_ref, v_ref, o_ref, lse_ref):
        kw = n_kv * t
        lane = lax.broadcasted_iota(jnp.int32, (t, V7X_LANES), 1)
        for rr in range(dil):
            for i in range(n_sub):
                k_tile0 = max(i + 1 - n_kv, 0)
                dist = ((i - k_tile0) * t + lax.broadcasted_iota(jnp.int32, (t, kw), 0)
                        - lax.broadcasted_iota(jnp.int32, (t, kw), 1))
                bias = jnp.where((dist >= 0) & (dist <= max_dist), 0.0, NEG_BIG)
                q_rows = rows(rr + i * t * dil, t)
                k_rows = rows(rr + k_tile0 * t * dil, kw)
                qc = q_ref[q_rows, :] * (HEAD_DIM ** -0.5)
                kt = k_ref[k_rows, :].astype(BF)
                vt = v_ref[k_rows, :].astype(BF)
                os_, ls_ = [], []
                for half in range(2):
                    qh = jnp.where(_lane_half_mask(qc.shape, half), qc, 0.0).astype(BF)
                    s = _dot_nt(qh, kt) + bias
                    m = jnp.max(s, axis=-1, keepdims=True)
                    p = jnp.exp(s - m)
                    den = jnp.sum(p, axis=-1, keepdims=True)
                    os_.append(jnp.dot(p.astype(BF), vt, preferred_element_type=jnp.float32) / den)
                    ls_.append(jnp.broadcast_to(m + jnp.log(den), (t, V7X_LANES)))
                o_ref[q_rows, :] = jnp.where(lane < HEAD_DIM, os_[0], os_[1])
                lse_ref[q_rows, :] = jnp.where(lane < HEAD_DIM, ls_[0], ls_[1])

    return kern


def dilated_attention_pallas(u, q_col, k_col, v_col):
    bsz, seq, _ = u.shape
    gw = DIL_HEADS * HEAD_DIM
    outs, lses = [], []
    sds = jax.ShapeDtypeStruct((bsz, seq, gw), jnp.float32)
    out_spec = pl.BlockSpec((None, seq, V7X_LANES), lambda b, c: (b, 0, c))
    for gi, (window, dil) in enumerate(DIL_GROUPS):

        def cmap(col):
            blk = (col + gi * gw) // V7X_LANES
            return lambda b, c: (b, 0, blk + c)

        o, lse = pl.pallas_call(
            _make_dil_kernel(dil, window // dil, seq),
            out_shape=(sds, sds),
            grid=(bsz, gw // V7X_LANES),
            in_specs=[pl.BlockSpec((None, seq, V7X_LANES), cmap(c)) for c in (q_col, k_col, v_col)],
            out_specs=(out_spec, out_spec),
            compiler_params=_cparams(2),
            name="dilated_attention",
        )(u, u, u)
        outs.append(o)
        lses.append(lse)
    return outs, lses


GLA_SUB = 16


def _gla_kernel(q_ref, k_ref, v_ref, r_ref, lr_ref, wa_ref, ba_ref, g_ref, o_ref, state_ref):
    c, dk, dv, sub = GLA_CHUNK, GLA_DK, GLA_DV, GLA_SUB

    @pl.when(pl.program_id(1) == 0)
    def _():
        state_ref[...] = jnp.zeros(state_ref.shape, jnp.float32)

    x = _dot_split(lr_ref[...], wa_ref[...]) + ba_ref[...]
    log_a = (jnp.minimum(x, 0.0) - jnp.log1p(jnp.exp(-jnp.abs(x)))) * (1.0 / GLA_TAU)
    tri = (lax.broadcasted_iota(jnp.int32, (c, c), 0) >= lax.broadcasted_iota(jnp.int32, (c, c), 1))
    la_hi, la_lo = _split_bf16(log_a)
    tri = tri.astype(BF)
    b_all = (jnp.dot(tri, la_hi, preferred_element_type=jnp.float32)
             + jnp.dot(tri, la_lo, preferred_element_type=jnp.float32))
    col = lax.broadcasted_iota(jnp.int32, (sub, c), 1)
    row = lax.broadcasted_iota(jnp.int32, (sub, c), 0)
    for h in range(GLA_HEADS):
        q = q_ref[:, h * dk:(h + 1) * dk] * (dk ** -0.5)
        k = k_ref[:, h * dk:(h + 1) * dk]
        v = v_ref[:, h * dv:(h + 1) * dv]
        b = b_all[:, h * dk:(h + 1) * dk]
        a_rows = []
        for blk in range(c // sub):
            lo = blk * sub
            q_i, b_i, k_i = q[lo:lo + sub], b[lo:lo + sub], k[lo:lo + sub]
            if blk > 0:
                b_prev = b[lo - 1:lo]
                q_t = q_i * jnp.exp(b_i - b_prev)
                k_t = k * jnp.exp(jnp.minimum(b_prev - b, 0.0))
                acc = jnp.where(col < lo, _dot_nt(q_t.astype(BF), k_t.astype(BF)), 0.0)
            else:
                acc = jnp.zeros((sub, c), jnp.float32)
            for j in range(sub):
                t = q_i * k_i[j:j + 1] * jnp.exp(jnp.minimum(b_i - b_i[j:j + 1], 0.0))
                cs = jnp.sum(t, axis=-1, keepdims=True)
                acc = acc + jnp.where((col == lo + j) & (row >= j), cs, 0.0)
            a_rows.append(acc)
        attn = jnp.concatenate(a_rows, axis=0)
        st = state_ref[h]
        o = jnp.dot(attn.astype(BF), v.astype(BF), preferred_element_type=jnp.float32)
        o = o + _dot_nt((q * jnp.exp(b)).astype(BF), st.astype(BF))
        b_last = b[c - 1:c]
        k_d = k * jnp.exp(b_last - b)
        upd = lax.dot_general(v.astype(BF), k_d.astype(BF), (((0,), (0,)), ((), ())),
                              preferred_element_type=jnp.float32)
        state_ref[h] = st * jnp.exp(b_last) + upd
        o = o * lax.rsqrt(jnp.mean(o * o, axis=-1, keepdims=True) + LN_EPS)
        r = r_ref[:, h * dv:(h + 1) * dv]
        o_ref[:, h * dv:(h + 1) * dv] = o * g_ref[...] * (r / (1.0 + jnp.exp(-r)))


def gla_mixer_pallas(u_gla, u_small, w_alpha2, b_alpha, norm_g):
    bsz, seq, _ = u_gla.shape
    c = GLA_CHUNK

    def tok(width, blk):
        return pl.BlockSpec((None, c, width), lambda b, i: (b, i, blk))

    def whole(shape):
        return pl.BlockSpec(shape, lambda b, i: (0,) * len(shape))

    wa = jnp.pad(w_alpha2, ((0, V7X_LANES - GLA_RANK), (0, 0)))
    return pl.pallas_call(
        _gla_kernel,
        out_shape=jax.ShapeDtypeStruct((bsz, seq, B_V), jnp.float32),
        grid=(bsz, seq // c),
        in_specs=[tok(B_K, 0), tok(B_K, 1), tok(B_V, 1), tok(B_V, 2), tok(V7X_LANES, 0),
                  whole((V7X_LANES, B_K)), whole((1, B_K)), whole((1, GLA_DV))],
        out_specs=tok(B_V, 0),
        scratch_shapes=[pltpu.VMEM((GLA_HEADS, GLA_DV, GLA_DK), jnp.float32)],
        compiler_params=_cparams(2),
        name="gla_mixer",
    )(u_gla, u_gla, u_gla, u_gla, u_small, wa, b_alpha.reshape(1, -1), norm_g.reshape(1, -1))


CMP_HALF = NSA_CMP_STRIDE * HEAD_DIM


def _gelu_tanh(x):
    return 0.5 * x * (1.0 + jnp.tanh(np.sqrt(2.0 / np.pi) * (x + 0.044715 * (x * x * x))))


def _compress_kernel(hk_ref, hv_ref, pek_ref, w1k_ref, w2k_ref, pev_ref, w1v_ref, w2v_ref, kc_ref, vc_ref):
    for h_ref, pe_ref, w1_ref, w2_ref, o_ref in ((hk_ref, pek_ref, w1k_ref, w2k_ref, kc_ref),
                                                 (hv_ref, pev_ref, w1v_ref, w2v_ref, vc_ref)):
        w1a = w1_ref[0].astype(BF)
        w1b = w1_ref[1].astype(BF)
        pe = pe_ref[...].astype(BF)
        pe_term = (jnp.dot(pe[:, :CMP_HALF], w1a, preferred_element_type=jnp.float32)
                   + jnp.dot(pe[:, CMP_HALF:], w1b, preferred_element_type=jnp.float32))
        w2 = w2_ref[...].astype(BF)
        for g in range(NSA_KV_GROUPS):
            hb = h_ref[g].astype(BF)
            y1 = jnp.dot(hb, w1a, preferred_element_type=jnp.float32)
            y2 = jnp.dot(hb, w1b, preferred_element_type=jnp.float32)
            n_half = y2.shape[0]
            pre = y1 + pltpu.roll(y2, n_half - 1, axis=0) + pe_term
            out = jnp.dot(_gelu_tanh(pre).astype(BF), w2, preferred_element_type=jnp.float32)
            o_ref[g] = jnp.concatenate([out, out], axis=1)


def nsa_compress_pallas(u, k_col, v_col, pe_k, w1_k, w2_k, pe_v, w1_v, w2_v):
    bsz, seq, _ = u.shape
    n_half = seq // NSA_CMP_STRIDE

    def halves(col):
        t = u[..., col:col + C_KV].reshape(bsz, n_half, NSA_CMP_STRIDE, NSA_KV_GROUPS, HEAD_DIM)
        return t.transpose(0, 3, 1, 2, 4).reshape(bsz, NSA_KV_GROUPS, n_half, CMP_HALF)

    def whole(shape):
        return pl.BlockSpec(shape, lambda b: (0,) * len(shape))

    hspec = pl.BlockSpec((None, NSA_KV_GROUPS, n_half, CMP_HALF), lambda b: (b, 0, 0, 0))
    ospec = pl.BlockSpec((None, NSA_KV_GROUPS, n_half, V7X_LANES), lambda b: (b, 0, 0, 0))
    wspecs = [whole((1, 2 * CMP_HALF)), whole((2, CMP_HALF, NSA_CMP_HIDDEN)), whole((NSA_CMP_HIDDEN, HEAD_DIM))]
    sds = jax.ShapeDtypeStruct((bsz, NSA_KV_GROUPS, n_half, V7X_LANES), jnp.float32)
    return pl.pallas_call(
        _compress_kernel,
        out_shape=(sds, sds),
        grid=(bsz,),
        in_specs=[hspec, hspec] + wspecs + wspecs,
        out_specs=(ospec, ospec),
        compiler_params=_cparams(1),
        name="nsa_compress",
    )(halves(k_col), halves(v_col),
      pe_k.reshape(1, -1), w1_k.reshape(2, CMP_HALF, NSA_CMP_HIDDEN), w2_k,
      pe_v.reshape(1, -1), w1_v.reshape(2, CMP_HALF, NSA_CMP_HIDDEN), w2_v)


CMP_TQ = 256


def _overlap_matrix(n_cmp_pad, n_blk):
    c0 = np.arange(n_cmp_pad)[:, None] * NSA_CMP_STRIDE
    s0 = np.arange(n_blk)[None, :] * NSA_SEL_LEN
    return ((c0 < s0 + NSA_SEL_LEN) & (c0 + NSA_CMP_LEN > s0)).astype(np.float32)


def _cmp_attn_kernel(q_ref, kc_ref, vc_ref, ov_ref, o_ref, m_ref):
    i = pl.program_id(2)
    tq = CMP_TQ
    n_c = kc_ref.shape[0]
    n_blk = ov_ref.shape[0]
    kc = kc_ref[...].astype(BF)
    vc = vc_ref[...].astype(BF)
    pos = i * tq + lax.broadcasted_iota(jnp.int32, (tq, n_c), 0)
    c_end = lax.broadcasted_iota(jnp.int32, (tq, n_c), 1) * NSA_CMP_STRIDE + (NSA_CMP_LEN - 1)
    cvalid = c_end <= pos
    lane = lax.broadcasted_iota(jnp.int32, (tq, V7X_LANES), 1)
    p_sum = jnp.zeros((tq, n_c), jnp.float32)
    outs = []
    for rr in range(NSA_GROUP_HEADS):
        qc = q_ref[:, (rr // 2) * V7X_LANES:(rr // 2 + 1) * V7X_LANES] * (HEAD_DIM ** -0.5)
        qh = jnp.where(_lane_half_mask(qc.shape, rr % 2), qc, 0.0).astype(BF)
        s = jnp.where(cvalid, _dot_nt(qh, kc), NEG_BIG)
        m = jnp.max(s, axis=-1, keepdims=True)
        p = jnp.where(cvalid, jnp.exp(s - m), 0.0)
        p = p / jnp.maximum(jnp.sum(p, axis=-1, keepdims=True), 1e-30)
        p_sum = p_sum + p
        outs.append(jnp.dot(p.astype(BF), vc, preferred_element_type=jnp.float32))
    for c in range(NSA_GROUP_HEADS // 2):
        o_ref[:, c * V7X_LANES:(c + 1) * V7X_LANES] = jnp.where(lane < HEAD_DIM, outs[2 * c], outs[2 * c + 1])
    ps_hi, ps_lo = _split_bf16(p_sum)
    imp = _dot_nt(ov_ref[...], ps_hi) + _dot_nt(ov_ref[...], ps_lo)
    j = lax.broadcasted_iota(jnp.int32, (n_blk, tq), 0)
    q_blk = (i * tq + lax.broadcasted_iota(jnp.int32, (n_blk, tq), 1)) // NSA_SEL_LEN
    forced = (j == 0) | (j == q_blk) | (j == q_blk - 1)
    imp = jnp.where(forced, jnp.inf, jnp.where(j > q_blk, -jnp.inf, imp))
    rank = jnp.zeros((n_blk, tq), jnp.float32)
    for jj in range(n_blk):
        other = imp[jj:jj + 1, :]
        beats = (other > imp) | ((other == imp) & (jj < j))
        rank = rank + jnp.where(beats, 1.0, 0.0)
    m_ref[...] = jnp.where(rank < float(min(NSA_N_SELECT, n_blk)), 1.0, 0.0).astype(m_ref.dtype)


def nsa_cmp_attention_pallas(u, q_col, kc, vc):
    bsz, seq, _ = u.shape
    n_c = kc.shape[2]
    n_blk = seq // NSA_SEL_LEN
    gw = NSA_GROUP_HEADS * HEAD_DIM
    qb = q_col // gw
    cspec = pl.BlockSpec((None, None, n_c, V7X_LANES), lambda b, g, i: (b, g, 0, 0))
    return pl.pallas_call(
        _cmp_attn_kernel,
        out_shape=(jax.ShapeDtypeStruct((bsz, seq, NSA_HEADS * HEAD_DIM), jnp.float32),
                   jax.ShapeDtypeStruct((bsz, NSA_KV_GROUPS, n_blk, seq), BF)),
        grid=(bsz, NSA_KV_GROUPS, seq // CMP_TQ),
        in_specs=[pl.BlockSpec((None, CMP_TQ, gw), lambda b, g, i: (b, i, qb + g)),
                  cspec, cspec,
                  pl.BlockSpec((n_blk, n_c), lambda b, g, i: (0, 0))],
        out_specs=(pl.BlockSpec((None, CMP_TQ, gw), lambda b, g, i: (b, i, g)),
                   pl.BlockSpec((None, None, n_blk, CMP_TQ), lambda b, g, i: (b, g, 0, i))),
        compiler_params=_cparams(3),
        name="nsa_cmp_attention",
    )(u, kc, vc, jnp.asarray(_overlap_matrix(n_c, n_blk).T, BF))


MERGE_TM = 256


def _layer_norm_rows(x, g, b):
    mu = jnp.mean(x, axis=-1, keepdims=True)
    xc = x - mu
    var = jnp.mean(xc * xc, axis=-1, keepdims=True)
    return xc * lax.rsqrt(var + LN_EPS) * g + b


def _gate_expand_matrices():
    e = np.zeros((NSA_BRANCHES, V7X_LANES, NSA_HEADS * HEAD_DIM), np.float32)
    for br in range(NSA_BRANCHES):
        for h in range(NSA_HEADS):
            e[br, GLA_RANK + NSA_BRANCHES * h + br, h * HEAD_DIM:(h + 1) * HEAD_DIM] = 1.0
    return e


def _sigmoid(x):
    return 1.0 / (1.0 + jnp.exp(-x))


def _merge_kernel(x_ref, o0_ref, o1_ref, o2_ref, l0_ref, l1_ref, l2_ref, yb_ref, oc_ref, os_ref, ow_ref,
                  us_ref, mg_ref, ex_ref, wa_ref, wb_ref, wc_ref, wo_ref, g_ref, b_ref, out_ref):
    l0, l1, l2 = l0_ref[...], l1_ref[...], l2_ref[...]
    lm = jnp.maximum(jnp.maximum(l0, l1), l2)
    e0, e1, e2 = jnp.exp(l0 - lm), jnp.exp(l1 - lm), jnp.exp(l2 - lm)
    y_a = (e0 * o0_ref[...] + e1 * o1_ref[...] + e2 * o2_ref[...]) / (e0 + e1 + e2)
    sg = _sigmoid(us_ref[...])
    y_c = (_dot_split_lhs(sg, ex_ref[0]) * oc_ref[...] + _dot_split_lhs(sg, ex_ref[1]) * os_ref[...]
           + _dot_split_lhs(sg, ex_ref[2]) * ow_ref[...])
    d = D_MODEL
    f = functools.partial(jnp.dot, preferred_element_type=jnp.float32)
    merged = (_sigmoid(mg_ref[:, :d]) * f(y_a.astype(BF), wa_ref[...])
              + _sigmoid(mg_ref[:, d:2 * d]) * f(yb_ref[...].astype(BF), wb_ref[...])
              + _sigmoid(mg_ref[:, 2 * d:]) * f(y_c.astype(BF), wc_ref[...]))
    mix = f(merged.astype(BF), wo_ref[...])
    out_ref[...] = _layer_norm_rows(DEEPNORM_ALPHA * x_ref[...] + mix, g_ref[...], b_ref[...])


def mixer_merge_pallas(x, dil_o, dil_lse, y_b, o_cmp, o_sel, o_win, u_small, m_g, w_br_a, w_br_b, w_br_c, w_o_mix, ln_g, ln_b):
    n, d = x.shape
    tm = MERGE_TM
    aw = DIL_HEADS * HEAD_DIM

    def tok(width):
        return pl.BlockSpec((tm, width), lambda i: (i, 0))

    def whole(shape):
        return pl.BlockSpec(shape, lambda i: (0,) * len(shape))

    ex = jnp.asarray(_gate_expand_matrices(), BF)
    return pl.pallas_call(
        _merge_kernel,
        out_shape=jax.ShapeDtypeStruct((n, d), jnp.float32),
        grid=(n // tm,),
        in_specs=[tok(d)] + [tok(aw)] * 6 + [tok(B_V)] + [tok(C_Q)] * 3 + [tok(V7X_LANES), tok(N_BRANCHES * d),
                  whole(ex.shape), whole((aw, d)), whole((B_V, d)), whole((C_Q, d)), whole((d, d)),
                  whole((1, d)), whole((1, d))],
        out_specs=tok(d),
        compiler_params=_cparams(1),
        name="mixer_merge",
    )(x, *dil_o, *dil_lse, y_b, o_cmp, o_sel, o_win, u_small, m_g, ex,
      w_br_a.astype(BF), w_br_b.astype(BF), w_br_c.astype(BF), w_o_mix.astype(BF),
      ln_g.reshape(1, d), ln_b.reshape(1, d))


XATTN_TM = 256


def _xattn_kernel(x_ref, k_ref, v_ref, wq_ref, wo_ref, g_ref, b_ref, o_ref):
    x = x_ref[...]
    d = x.shape[-1]
    hd = d // XATTN_HEADS
    q = jnp.dot(x.astype(BF), wq_ref[...], preferred_element_type=jnp.float32)
    outs = []
    for h in range(XATTN_HEADS):
        sl = slice(h * hd, (h + 1) * hd)
        s = _dot_nt((q[:, sl] * (hd ** -0.5)).astype(BF), k_ref[:, sl].astype(BF))
        p = jnp.exp(s - jnp.max(s, axis=-1, keepdims=True))
        den = jnp.sum(p, axis=-1, keepdims=True)
        outs.append(jnp.dot(p.astype(BF), v_ref[:, sl].astype(BF), preferred_element_type=jnp.float32) / den)
    o = jnp.concatenate(outs, axis=1)
    xa = jnp.dot(o.astype(BF), wo_ref[...], preferred_element_type=jnp.float32)
    o_ref[...] = _layer_norm_rows(DEEPNORM_ALPHA * x + xa, g_ref[...], b_ref[...])


def cross_attention_block_pallas(x, mem, w_xq, w_xk, w_xv, w_xo, ln_g, ln_b):
    bsz, seq, d = x.shape
    mem_len = mem.shape[1]
    mem2 = mem.reshape(-1, d)
    k = _mm(mem2, w_xk)
    v = _mm(mem2, w_xv)
    n_t = seq // XATTN_TM

    def whole(shape):
        return pl.BlockSpec(shape, lambda b, i: (0,) * len(shape))

    out = pl.pallas_call(
        _xattn_kernel,
        out_shape=jax.ShapeDtypeStruct((bsz * seq, d), jnp.float32),
        grid=(bsz, n_t),
        in_specs=[pl.BlockSpec((XATTN_TM, d), lambda b, i: (b * n_t + i, 0)),
                  pl.BlockSpec((mem_len, d), lambda b, i: (b, 0)),
                  pl.BlockSpec((mem_len, d), lambda b, i: (b, 0)),
                  whole((d, d)), whole((d, d)), whole((1, d)), whole((1, d))],
        out_specs=pl.BlockSpec((XATTN_TM, d), lambda b, i: (b * n_t + i, 0)),
        compiler_params=_cparams(2),
        name="cross_attention_block",
    )(x.reshape(-1, d), k, v, w_xq.astype(BF), w_xo.astype(BF), ln_g.reshape(1, d), ln_b.reshape(1, d))
    return out.reshape(bsz, seq, d)


def _ln_kernel(x_ref, g_ref, b_ref, o_ref):
    o_ref[...] = _layer_norm_rows(x_ref[...], g_ref[...], b_ref[...])


def layer_norm_pallas(x, g, b):
    shape = x.shape
    d = shape[-1]
    x2 = x.reshape(-1, d)
    tm = _pick_tile(x2.shape[0], (1024, 512, 256, 128, 64, 32, 16, 8))
    out = pl.pallas_call(
        _ln_kernel,
        out_shape=jax.ShapeDtypeStruct(x2.shape, jnp.float32),
        grid=(x2.shape[0] // tm,),
        in_specs=[pl.BlockSpec((tm, d), lambda i: (i, 0)),
                  pl.BlockSpec((1, d), lambda i: (0, 0)), pl.BlockSpec((1, d), lambda i: (0, 0))],
        out_specs=pl.BlockSpec((tm, d), lambda i: (i, 0)),
        compiler_params=_cparams(1),
        name="layer_norm",
    )(x2, g.reshape(1, d), b.reshape(1, d))
    return out.reshape(shape)


MOE_TB = 256


def _moe_ffn_kernel(be_ref, nused_ref, x_ref, wgu_ref, bgu_ref, wd_ref, bd_ref, o_ref, wgu_bf, wd_bf):
    i = pl.program_id(0)
    changed = jnp.logical_or(i == 0, be_ref[i] != be_ref[jnp.maximum(i - 1, 0)])

    @pl.when(changed)
    def _():
        wgu_bf[...] = wgu_ref[...].astype(BF)
        wd_bf[...] = wd_ref[...].astype(BF)

    @pl.when(i < nused_ref[0])
    def _():
        gu = jnp.dot(x_ref[...].astype(BF), wgu_bf[...], preferred_element_type=jnp.float32) + bgu_ref[...]
        gate = jnp.minimum(gu[:, :D_FF], SWIGLU_LIMIT)
        up = jnp.clip(gu[:, D_FF:], -SWIGLU_LIMIT, SWIGLU_LIMIT)
        act = (up + 1.0) * gate * (1.0 / (1.0 + jnp.exp(-SWIGLU_ALPHA * gate)))
        o_ref[...] = jnp.dot(act.astype(BF), wd_bf[...], preferred_element_type=jnp.float32) + bd_ref[...]

    @pl.when(i >= nused_ref[0])
    def _():
        o_ref[...] = jnp.zeros(o_ref.shape, o_ref.dtype)


def moe_expert_ffn_pallas(buf, block_e, n_used, w_gu, b_gu, w_down, b_down):
    n_rows, d = buf.shape
    n_e = w_gu.shape[0]
    grid_spec = pltpu.PrefetchScalarGridSpec(
        num_scalar_prefetch=2,
        grid=(n_rows // MOE_TB,),
        in_specs=[
            pl.BlockSpec((MOE_TB, d), lambda i, be, nu: (i, 0)),
            pl.BlockSpec((None, d, 2 * D_FF), lambda i, be, nu: (be[i], 0, 0)),
            pl.BlockSpec((None, 1, 2 * D_FF), lambda i, be, nu: (be[i], 0, 0)),
            pl.BlockSpec((None, D_FF, d), lambda i, be, nu: (be[i], 0, 0)),
            pl.BlockSpec((None, 1, d), lambda i, be, nu: (be[i], 0, 0)),
        ],
        out_specs=pl.BlockSpec((MOE_TB, d), lambda i, be, nu: (i, 0)),
        scratch_shapes=[pltpu.VMEM((d, 2 * D_FF), BF), pltpu.VMEM((D_FF, d), BF)],
    )
    return pl.pallas_call(
        _moe_ffn_kernel,
        out_shape=jax.ShapeDtypeStruct((n_rows, d), jnp.float32),
        grid_spec=grid_spec,
        compiler_params=_cparams(1),
        name="moe_expert_ffn",
    )(block_e, n_used, buf, w_gu, b_gu.reshape(n_e, 1, -1), w_down, b_down.reshape(n_e, 1, -1))


MOE_ROUTE_TM = 256


def _router_kernel(x_ref, wr_ref, br_ref, route_ref, counts_ref, carry_ref):
    tm = MOE_ROUTE_TM

    @pl.when(pl.program_id(0) == 0)
    def _():
        carry_ref[...] = jnp.zeros(carry_ref.shape, jnp.float32)

    logits = _dot_split(x_ref[...], wr_ref[...]) + br_ref[...]
    lane = lax.broadcasted_iota(jnp.int32, (tm, N_EXPERTS), 1).astype(jnp.float32)
    onehots, vals, idxs = [], [], []
    rest = logits
    for _ in range(TOP_K):
        m = jnp.max(rest, axis=-1, keepdims=True)
        idx = jnp.min(jnp.where(rest == m, lane, float(N_EXPERTS)), axis=-1, keepdims=True)
        hit = lane == idx
        onehots.append(hit)
        vals.append(m)
        idxs.append(idx)
        rest = jnp.where(hit, -jnp.inf, rest)
    exps = [jnp.exp(v - vals[0]) for v in vals]
    den = exps[0] + exps[1] + exps[2] + exps[3]
    cnt = jnp.zeros((tm, N_EXPERTS), jnp.float32)
    for hit in onehots:
        cnt = cnt + jnp.where(hit, 1.0, 0.0)
    earlier = (lax.broadcasted_iota(jnp.int32, (tm, tm), 1) < lax.broadcasted_iota(jnp.int32, (tm, tm), 0))
    before = jnp.dot(earlier.astype(BF), cnt.astype(BF), preferred_element_type=jnp.float32) + carry_ref[...]
    lane_out = lax.broadcasted_iota(jnp.int32, (tm, V7X_LANES), 1)
    out = jnp.zeros((tm, V7X_LANES), jnp.float32)
    for k in range(TOP_K):
        pos = jnp.sum(jnp.where(onehots[k], before, 0.0), axis=-1, keepdims=True)
        out = jnp.where(lane_out == k, idxs[k], out)
        out = jnp.where(lane_out == TOP_K + k, exps[k] / den, out)
        out = jnp.where(lane_out == 2 * TOP_K + k, pos, out)
    route_ref[...] = out
    carry_ref[...] = carry_ref[...] + jnp.sum(cnt, axis=0, keepdims=True)
    counts_ref[...] = carry_ref[...]


def moe_router_pallas(x, w_router, b_router):
    n, d = x.shape
    tm = MOE_ROUTE_TM
    return pl.pallas_call(
        _router_kernel,
        out_shape=(jax.ShapeDtypeStruct((n, V7X_LANES), jnp.float32),
                   jax.ShapeDtypeStruct((1, N_EXPERTS), jnp.float32)),
        grid=(n // tm,),
        in_specs=[pl.BlockSpec((tm, d), lambda i: (i, 0)),
                  pl.BlockSpec((d, N_EXPERTS), lambda i: (0, 0)),
                  pl.BlockSpec((1, N_EXPERTS), lambda i: (0, 0))],
        out_specs=(pl.BlockSpec((tm, V7X_LANES), lambda i: (i, 0)),
                   pl.BlockSpec((1, N_EXPERTS), lambda i: (0, 0))),
        scratch_shapes=[pltpu.VMEM((1, N_EXPERTS), jnp.float32)],
        compiler_params=_cparams(1),
        name="moe_router",
    )(x, w_router, b_router.reshape(1, -1))


def _row_copy(src, dst, sem):
    return pltpu.make_async_copy(src, dst, sem)


def _dispatch_kernel(dest_ref, x_ref, zeros_ref, buf_ref, sem):
    del zeros_ref
    tm = MOE_ROUTE_TM

    def body(t, carry):
        for k in range(TOP_K):
            row = dest_ref[0, TOP_K * t + k]
            _row_copy(x_ref.at[pl.ds(t, 1)], buf_ref.at[pl.ds(row, 1)], sem).start()
        return carry

    lax.fori_loop(0, tm, body, 0)
    for _ in range(TOP_K):
        _row_copy(x_ref, buf_ref.at[pl.ds(0, tm)], sem).wait()


def moe_dispatch_pallas(x, dest, n_rows):
    n, d = x.shape
    tm = MOE_ROUTE_TM
    return pl.pallas_call(
        _dispatch_kernel,
        out_shape=jax.ShapeDtypeStruct((n_rows, d), jnp.float32),
        grid=(n // tm,),
        in_specs=[pl.BlockSpec((None, 1, TOP_K * tm), lambda i: (i, 0, 0), memory_space=pltpu.SMEM),
                  pl.BlockSpec((tm, d), lambda i: (i, 0)),
                  pl.BlockSpec(memory_space=pl.ANY)],
        out_specs=pl.BlockSpec(memory_space=pl.ANY),
        scratch_shapes=[pltpu.SemaphoreType.DMA],
        input_output_aliases={2: 0},
        compiler_params=_cparams(1),
        name="moe_dispatch",
    )(dest, x, jnp.zeros((n_rows, d), jnp.float32))


def _combine_kernel(dest_ref, x_ref, route_ref, obuf_ref, g_ref, b_ref, o_ref, rows_ref, sem):
    tm = MOE_ROUTE_TM

    def body(t, carry):
        for k in range(TOP_K):
            row = dest_ref[0, TOP_K * t + k]
            _row_copy(obuf_ref.at[pl.ds(row, 1)], rows_ref.at[k, pl.ds(t, 1)], sem).start()
        return carry

    lax.fori_loop(0, tm, body, 0)
    for k in range(TOP_K):
        _row_copy(obuf_ref.at[pl.ds(0, tm)], rows_ref.at[k], sem).wait()
    ff = jnp.zeros(x_ref.shape, jnp.float32)
    for k in range(TOP_K):
        ff = ff + route_ref[:, TOP_K + k:TOP_K + k + 1] * rows_ref[k]
    o_ref[...] = _layer_norm_rows(DEEPNORM_ALPHA * x_ref[...] + ff, g_ref[...], b_ref[...])


def moe_combine_pallas(x, route, dest, out_buf, ln_g, ln_b):
    n, d = x.shape
    tm = MOE_ROUTE_TM
    return pl.pallas_call(
        _combine_kernel,
        out_shape=jax.ShapeDtypeStruct((n, d), jnp.float32),
        grid=(n // tm,),
        in_specs=[pl.BlockSpec((None, 1, TOP_K * tm), lambda i: (i, 0, 0), memory_space=pltpu.SMEM),
                  pl.BlockSpec((tm, d), lambda i: (i, 0)),
                  pl.BlockSpec((tm, V7X_LANES), lambda i: (i, 0)),
                  pl.BlockSpec(memory_space=pl.ANY),
                  pl.BlockSpec((1, d), lambda i: (0, 0)), pl.BlockSpec((1, d), lambda i: (0, 0))],
        out_specs=pl.BlockSpec((tm, d), lambda i: (i, 0)),
        scratch_shapes=[pltpu.VMEM((TOP_K, tm, d), jnp.float32), pltpu.SemaphoreType.DMA],
        compiler_params=_cparams(1),
        name="moe_combine",
    )(dest, x, route, out_buf, ln_g.reshape(1, d), ln_b.reshape(1, d))


def token_mixer_block(h, w_in, b_in, w_alpha2, b_alpha, gla_norm_g, cmp_pe_k, cmp_w1_k, cmp_w2_k,
                      cmp_pe_v, cmp_w1_v, cmp_w2_v, w_br_a, w_br_b, w_br_c, w_o_mix, ln_g, ln_b):
    bsz, seq, d = h.shape
    offs = np.concatenate([[0], np.cumsum(IN_WIDTHS)]).tolist()
    h2 = h.reshape(-1, d)

    def proj(lo, hi):
        return _mm(h2, w_in[:, offs[lo]:offs[hi]], b_in[offs[lo]:offs[hi]]).reshape(bsz, seq, -1)

    u_dil = proj(0, 3)
    u_gla = proj(3, 7)
    w_small = jnp.concatenate([w_in[:, offs[7]:offs[8]], w_in[:, offs[15]:offs[16]]], axis=1)
    b_small = jnp.concatenate([b_in[offs[7]:offs[8]], b_in[offs[15]:offs[16]]])
    n_small = w_small.shape[1]
    w_small = jnp.pad(w_small, ((0, 0), (0, V7X_LANES - n_small)))
    b_small = jnp.pad(b_small, (0, V7X_LANES - n_small))
    u_small = _mm(h2, w_small, b_small)
    u_c = proj(8, 15)
    m_g = _mm(h2, w_in[:, offs[16]:offs[17]], b_in[offs[16]:offs[17]])
    dil_o, dil_lse = dilated_attention_pallas(u_dil, 0, A_W, 2 * A_W)
    y_b = gla_mixer_pallas(u_gla, u_small.reshape(bsz, seq, V7X_LANES), w_alpha2, b_alpha, gla_norm_g)
    kc, vc = nsa_compress_pallas(u_c, C_Q, C_Q + C_KV, cmp_pe_k, cmp_w1_k, cmp_w2_k, cmp_pe_v, cmp_w1_v, cmp_w2_v)
    o_cmp, selmask = nsa_cmp_attention_pallas(u_c, 0, kc, vc)
    o_sel = nsa_gqa_attention_pallas(u_c, 0, C_Q + 2 * C_KV, C_Q + 3 * C_KV, selmask)
    o_win = nsa_gqa_attention_pallas(u_c, 0, C_Q + 4 * C_KV, C_Q + 5 * C_KV)

    def flat(t):
        return t.reshape(bsz * seq, -1)

    out = mixer_merge_pallas(h2, [flat(t) for t in dil_o], [flat(t) for t in dil_lse], flat(y_b), flat(o_cmp),
                             flat(o_sel), flat(o_win), u_small, m_g, w_br_a, w_br_b, w_br_c, w_o_mix, ln_g, ln_b)
    return out.reshape(bsz, seq, d)


def moe_block(x, w_router, b_router, w_gu, b_gu, w_down, b_down, ln_g, ln_b):
    n, d = x.shape
    route, counts = moe_router_pallas(x, w_router, b_router)
    expert = route[:, :TOP_K].astype(jnp.int32)
    pos = route[:, 2 * TOP_K:3 * TOP_K].astype(jnp.int32)
    counts = counts.reshape(-1).astype(jnp.int32)
    padded = (counts + MOE_TB - 1) // MOE_TB * MOE_TB
    pend = jnp.cumsum(padded)
    pstart = pend - padded
    dest = (pstart[expert] + pos).reshape(n // MOE_ROUTE_TM, 1, TOP_K * MOE_ROUTE_TM)
    n_rows = n * TOP_K + N_EXPERTS * MOE_TB
    n_blocks = n_rows // MOE_TB
    block_e = jnp.minimum(jnp.searchsorted(pend, jnp.arange(n_blocks) * MOE_TB, side='right'), N_EXPERTS - 1)
    n_used = (pend[-1:] // MOE_TB).astype(jnp.int32)
    buf = moe_dispatch_pallas(x, dest, n_rows)
    out = moe_expert_ffn_pallas(buf, block_e.astype(jnp.int32), n_used, w_gu, b_gu, w_down, b_down)
    return moe_combine_pallas(x, route, dest, out, ln_g, ln_b)


def kernel(x, mem, ln0_g, ln0_b, w_in, b_in, w_alpha2, b_alpha, gla_norm_g, cmp_pe_k, cmp_w1_k, cmp_w2_k, cmp_pe_v, cmp_w1_v, cmp_w2_v, w_br_a, w_br_b, w_br_c, w_o_mix, ln1_g, ln1_b, w_xq, w_xk, w_xv, w_xo, ln2_g, ln2_b, w_router, b_router, w_gu, b_gu, w_down, b_down, ln3_g, ln3_b):
    x = layer_norm_pallas(x, ln0_g, ln0_b)
    for li in range(DEPTH):
        x = token_mixer_block(x, w_in[li], b_in[li], w_alpha2[li], b_alpha[li], gla_norm_g[li],
                              cmp_pe_k[li], cmp_w1_k[li], cmp_w2_k[li], cmp_pe_v[li], cmp_w1_v[li], cmp_w2_v[li],
                              w_br_a[li], w_br_b[li], w_br_c[li], w_o_mix[li], ln1_g[li], ln1_b[li])
        x = cross_attention_block_pallas(x, mem, w_xq[li], w_xk[li], w_xv[li], w_xo[li], ln2_g[li], ln2_b[li])
        x = moe_block(x.reshape(-1, x.shape[-1]), w_router[li], b_router[li], w_gu[li], b_gu[li], w_down[li], b_down[li],
                      ln3_g[li], ln3_b[li]).reshape(x.shape)
    return x
```

```python
import functools

import jax
import jax.numpy as jnp
import numpy as np
from jax import lax
from jax.experimental import pallas as pl
from jax.experimental.pallas import tpu as pltpu

D_MODEL = 1024
DEPTH = 2
HEAD_DIM = 64
N_BRANCHES = 3
LN_EPS = 1e-5

DIL_GROUPS = ((128, 1), (512, 4), (2048, 16))
DIL_HEADS = 4

GLA_HEADS = 4
GLA_DK = D_MODEL // 2 // GLA_HEADS
GLA_DV = D_MODEL // GLA_HEADS
GLA_RANK = 16
GLA_TAU = 16.0
GLA_CHUNK = 64

NSA_HEADS = 16
NSA_KV_GROUPS = 4
NSA_GROUP_HEADS = NSA_HEADS // NSA_KV_GROUPS
NSA_BRANCHES = 3
NSA_CMP_LEN = 32
NSA_CMP_STRIDE = 16
NSA_CMP_HIDDEN = 2 * HEAD_DIM
NSA_SEL_LEN = 64
NSA_N_SELECT = 16
NSA_WINDOW = 512

XATTN_HEADS = 4

N_EXPERTS = 32
TOP_K = 4
D_FF = D_MODEL
SWIGLU_LIMIT = 7.0
SWIGLU_ALPHA = 1.702

DEEPNORM_ALPHA = (2 * DEPTH) ** 0.25

A_W = len(DIL_GROUPS) * DIL_HEADS * HEAD_DIM
B_K = GLA_HEADS * GLA_DK
B_V = GLA_HEADS * GLA_DV
C_Q = NSA_HEADS * HEAD_DIM
C_KV = NSA_KV_GROUPS * HEAD_DIM
IN_WIDTHS = (A_W, A_W, A_W,
             B_K, B_K, B_V, B_V, GLA_RANK,
             C_Q, C_KV, C_KV, C_KV, C_KV, C_KV, C_KV, NSA_HEADS * NSA_BRANCHES,
             N_BRANCHES * D_MODEL)

V7X_VMEM_LIMIT_BYTES = 48 * 1024 * 1024
V7X_LANES = 128
NEG_BIG = -1e30
BF = jnp.bfloat16


def _cparams(n_axes):
    return pltpu.CompilerParams(dimension_semantics=("arbitrary",) * n_axes,
                                vmem_limit_bytes=V7X_VMEM_LIMIT_BYTES)


def _split_bf16(x):
    hi = x.astype(BF)
    lo = (x - hi.astype(jnp.float32)).astype(BF)
    return hi, lo


def _dot_split(a, b):
    a_hi, a_lo = _split_bf16(a)
    b_hi, b_lo = _split_bf16(b)
    f = functools.partial(jnp.dot, preferred_element_type=jnp.float32)
    return f(a_hi, b_hi) + (f(a_hi, b_lo) + f(a_lo, b_hi))


def _dot_split_lhs(a, b_exact):
    a_hi, a_lo = _split_bf16(a)
    f = functools.partial(jnp.dot, preferred_element_type=jnp.float32)
    return f(a_hi, b_exact) + f(a_lo, b_exact)


def _dot_nt(a, b):
    return lax.dot_general(a, b, (((1,), (1,)), ((), ())), preferred_element_type=jnp.float32)


def _mm_kernel(x_ref, w_ref, b_ref, o_ref):
    acc = jnp.dot(x_ref[...].astype(BF), w_ref[...].astype(BF), preferred_element_type=jnp.float32)
    o_ref[...] = (acc + b_ref[...]).astype(o_ref.dtype)


def _pick_tile(n, candidates):
    for c in candidates:
        if n % c == 0:
            return c
    return n


def _mm(x, w, b=None, out_dtype=jnp.float32):
    m, k = x.shape
    n = w.shape[1]
    if b is None:
        b = jnp.zeros((n,), jnp.float32)
    tm = _pick_tile(m, (1024, 512, 256, 128, 64, 32, 16, 8))
    tn = _pick_tile(n, (1536, 1280, 1152, 1024, 768, 512, 384, 256, 128))
    return pl.pallas_call(
        _mm_kernel,
        out_shape=jax.ShapeDtypeStruct((m, n), out_dtype),
        grid=(n // tn, m // tm),
        in_specs=[
            pl.BlockSpec((tm, k), lambda j, i: (i, 0)),
            pl.BlockSpec((k, tn), lambda j, i: (0, j)),
            pl.BlockSpec((1, tn), lambda j, i: (0, j)),
        ],
        out_specs=pl.BlockSpec((tm, tn), lambda j, i: (i, j)),
        compiler_params=_cparams(2),
        name="dense_proj",
    )(x, w, b.reshape(1, n))


def _mm3(x, w, b=None):
    lead = x.shape[:-1]
    return _mm(x.reshape(-1, x.shape[-1]), w, b).reshape(*lead, w.shape[1])


GQA_TQ = 256
GQA_TK = 256


def _half_select(x, want_half, have_half):
    if want_half != have_half:
        x = pltpu.roll(x, HEAD_DIM, axis=1)
    return x


def _lane_half_mask(shape, half):
    lane = lax.broadcasted_iota(jnp.int32, shape, 1)
    return (lane >= half * HEAD_DIM) & (lane < (half + 1) * HEAD_DIM)


def _make_gqa_kernel(select, max_dist, n_outer):
    r, tq, tk = NSA_GROUP_HEADS, GQA_TQ, GQA_TK

    def kern(q_ref, k_ref, v_ref, *rest):
        if select:
            m_ref, e_ref, o_ref, mx_ref, l_ref, acc_ref = rest
        else:
            o_ref, mx_ref, l_ref, acc_ref = rest
        i = pl.program_id(n_outer)
        dist0 = i * tq + lax.broadcasted_iota(jnp.int32, (tk, tq), 1) - lax.broadcasted_iota(jnp.int32, (tk, tq), 0)
        qs = []
        for half in range(2):
            parts = []
            for rr in range(r):
                hh = half * r + rr
                qc = q_ref[:, (hh // 2) * V7X_LANES:(hh // 2 + 1) * V7X_LANES].astype(jnp.float32) * (HEAD_DIM ** -0.5)
                qc = _half_select(qc, half, hh % 2)
                parts.append(jnp.where(_lane_half_mask(qc.shape, half), qc, 0.0).astype(BF))
            qs.append(jnp.concatenate(parts, axis=0))
        mx_ref[...] = jnp.full(mx_ref.shape, NEG_BIG, jnp.float32)
        l_ref[...] = jnp.zeros(l_ref.shape, jnp.float32)
        acc_ref[...] = jnp.zeros(acc_ref.shape, jnp.float32)

        def body(j, carry):
            k0 = pl.multiple_of(j * tk, tk)
            kt = k_ref[pl.ds(k0, tk), :].astype(BF)
            vt = v_ref[pl.ds(k0, tk), :].astype(BF)
            dist = dist0 - k0
            in_band = (dist >= 0) if select else ((dist >= 0) & (dist <= max_dist))
            for half in range(2):
                if select:
                    blk = jnp.dot(e_ref[j], m_ref[half], preferred_element_type=jnp.float32)
                    valid = (blk > 0.5) & in_band
                else:
                    valid = in_band
                bias = jnp.where(valid, 0.0, NEG_BIG)
                s = _dot_nt(kt, qs[half]) + jnp.concatenate([bias] * r, axis=1)
                m_old = mx_ref[half]
                m_new = jnp.maximum(m_old, jnp.max(s, axis=0, keepdims=True))
                p = jnp.exp(s - m_new)
                corr = jnp.exp(m_old - m_new)
                l_ref[half] = corr * l_ref[half] + jnp.sum(p, axis=0, keepdims=True)
                pv = lax.dot_general(vt, p.astype(BF), (((0,), (0,)), ((), ())), preferred_element_type=jnp.float32)
                acc_ref[half] = corr * acc_ref[half] + pv
                mx_ref[half] = m_new
            return carry

        lo = 0 if select else jnp.maximum(i * tq - max_dist, 0) // tk
        lax.fori_loop(lo, (i * tq + tq - 1) // tk + 1, body, 0)
        outs = []
        for half in range(2):
            o = (acc_ref[half] / l_ref[half]).T
            for rr in range(r):
                outs.append(_half_select(o[rr * tq:(rr + 1) * tq], (half * r + rr) % 2, half))
        lane = lax.broadcasted_iota(jnp.int32, (tq, V7X_LANES), 1)
        for c in range(r):
            o_ref[:, c * V7X_LANES:(c + 1) * V7X_LANES] = jnp.where(lane < HEAD_DIM, outs[2 * c], outs[2 * c + 1])

    return kern


def _sel_expand_matrix(seq):
    n_blk = seq // NSA_SEL_LEN
    e = (np.arange(seq)[:, None] // NSA_SEL_LEN == np.arange(n_blk)[None, :]).astype(np.float32)
    return e.reshape(seq // GQA_TK, GQA_TK, n_blk)


def nsa_gqa_attention_pallas(u, q_col, k_col, v_col, selmask=None):
    bsz, seq, _ = u.shape
    select = selmask is not None
    n_pairs = NSA_KV_GROUPS // 2
    qw = 2 * NSA_GROUP_HEADS * HEAD_DIM
    qb, kb, vb = q_col // qw, k_col // V7X_LANES, v_col // V7X_LANES
    rows = NSA_GROUP_HEADS * GQA_TQ
    in_specs = [
        pl.BlockSpec((None, GQA_TQ, qw), lambda b, gp, i: (b, i, qb + gp)),
        pl.BlockSpec((None, seq, V7X_LANES), lambda b, gp, i: (b, 0, kb + gp)),
        pl.BlockSpec((None, seq, V7X_LANES), lambda b, gp, i: (b, 0, vb + gp)),
    ]
    args = [u, u, u]
    if select:
        n_blk = selmask.shape[2]
        in_specs += [pl.BlockSpec((None, 2, n_blk, GQA_TQ), lambda b, gp, i: (b, gp, 0, i)),
                     pl.BlockSpec((seq // GQA_TK, GQA_TK, n_blk), lambda b, gp, i: (0, 0, 0))]
        args += [selmask, jnp.asarray(_sel_expand_matrix(seq), BF)]
    return pl.pallas_call(
        _make_gqa_kernel(select, NSA_WINDOW - 1, 2),
        out_shape=jax.ShapeDtypeStruct((bsz, seq, NSA_HEADS * HEAD_DIM), jnp.float32),
        grid=(bsz, n_pairs, seq // GQA_TQ),
        in_specs=in_specs,
        out_specs=pl.BlockSpec((None, GQA_TQ, qw), lambda b, gp, i: (b, i, gp)),
        scratch_shapes=[pltpu.VMEM((2, 1, rows), jnp.float32),
                        pltpu.VMEM((2, 1, rows), jnp.float32),
                        pltpu.VMEM((2, V7X_LANES, rows), jnp.float32)],
        compiler_params=_cparams(3),
        name="nsa_selected_attention" if select else "nsa_window_attention",
    )(*args)


DIL_TILE = 128


def _make_dil_kernel(dil, max_dist, seq):
    t = DIL_TILE
    n_sub = seq // dil // t
    n_kv = min(-(-max_dist // t) + 1, n_sub)

    def rows(first, size):
        return pl.ds(first, size, stride=dil) if dil > 1 else pl.ds(first, size)

    def kern(q_ref, k_ref, v_ref, o_ref, lse_ref):
        kw = n_kv * t
        lane = lax.broadcasted_iota(jnp.int32, (t, V7X_LANES), 1)
        for rr in range(dil):
            for i in range(n_sub):
                k_tile0 = max(i + 1 - n_kv, 0)
                dist = ((i - k_tile0) * t + lax.broadcasted_iota(jnp.int32, (t, kw), 0)
                        - lax.broadcasted_iota(jnp.int32, (t, kw), 1))
                bias = jnp.where((dist >= 0) & (dist <= max_dist), 0.0, NEG_BIG)
                q_rows = rows(rr + i * t * dil, t)
                k_rows = rows(rr + k_tile0 * t * dil, kw)
                qc = q_ref[q_rows, :] * (HEAD_DIM ** -0.5)
                kt = k_ref[k_rows, :].astype(BF)
                vt = v_ref[k_rows, :].astype(BF)
                os_, ls_ = [], []
                for half in range(2):
                    qh = jnp.where(_lane_half_mask(qc.shape, half), qc, 0.0).astype(BF)
                    s = _dot_nt(qh, kt) + bias
                    m = jnp.max(s, axis=-1, keepdims=True)
                    p = jnp.exp(s - m)
                    den = jnp.sum(p, axis=-1, keepdims=True)
                    os_.append(jnp.dot(p.astype(BF), vt, preferred_element_type=jnp.float32) / den)
                    ls_.append(jnp.broadcast_to(m + jnp.log(den), (t, V7X_LANES)))
                o_ref[q_rows, :] = jnp.where(lane < HEAD_DIM, os_[0], os_[1])
                lse_ref[q_rows, :] = jnp.where(lane < HEAD_DIM, ls_[0], ls_[1])

    return kern


def dilated_attention_pallas(u, q_col, k_col, v_col):
    bsz, seq, _ = u.shape
    gw = DIL_HEADS * HEAD_DIM
    outs, lses = [], []
    sds = jax.ShapeDtypeStruct((bsz, seq, gw), jnp.float32)
    out_spec = pl.BlockSpec((None, seq, V7X_LANES), lambda b, c: (b, 0, c))
    for gi, (window, dil) in enumerate(DIL_GROUPS):

        def cmap(col):
            blk = (col + gi * gw) // V7X_LANES
            return lambda b, c: (b, 0, blk + c)

        o, lse = pl.pallas_call(
            _make_dil_kernel(dil, window // dil, seq),
            out_shape=(sds, sds),
            grid=(bsz, gw // V7X_LANES),
            in_specs=[pl.BlockSpec((None, seq, V7X_LANES), cmap(c)) for c in (q_col, k_col, v_col)],
            out_specs=(out_spec, out_spec),
            compiler_params=_cparams(2),
            name="dilated_attention",
        )(u, u, u)
        outs.append(o)
        lses.append(lse)
    return outs, lses


GLA_SUB = 16


def _gla_kernel(q_ref, k_ref, v_ref, r_ref, lr_ref, wa_ref, ba_ref, g_ref, o_ref, state_ref):
    c, dk, dv, sub = GLA_CHUNK, GLA_DK, GLA_DV, GLA_SUB

    @pl.when(pl.program_id(1) == 0)
    def _():
        state_ref[...] = jnp.zeros(state_ref.shape, jnp.float32)

    x = _dot_split(lr_ref[...], wa_ref[...]) + ba_ref[...]
    log_a = (jnp.minimum(x, 0.0) - jnp.log1p(jnp.exp(-jnp.abs(x)))) * (1.0 / GLA_TAU)
    tri = (lax.broadcasted_iota(jnp.int32, (c, c), 0) >= lax.broadcasted_iota(jnp.int32, (c, c), 1))
    la_hi, la_lo = _split_bf16(log_a)
    tri = tri.astype(BF)
    b_all = (jnp.dot(tri, la_hi, preferred_element_type=jnp.float32)
             + jnp.dot(tri, la_lo, preferred_element_type=jnp.float32))
    col = lax.broadcasted_iota(jnp.int32, (sub, c), 1)
    row = lax.broadcasted_iota(jnp.int32, (sub, c), 0)
    for h in range(GLA_HEADS):
        q = q_ref[:, h * dk:(h + 1) * dk] * (dk ** -0.5)
        k = k_ref[:, h * dk:(h + 1) * dk]
        v = v_ref[:, h * dv:(h + 1) * dv]
        b = b_all[:, h * dk:(h + 1) * dk]
        a_rows = []
        for blk in range(c // sub):
            lo = blk * sub
            q_i, b_i, k_i = q[lo:lo + sub], b[lo:lo + sub], k[lo:lo + sub]
            if blk > 0:
                b_prev = b[lo - 1:lo]
                q_t = q_i * jnp.exp(b_i - b_prev)
                k_t = k * jnp.exp(jnp.minimum(b_prev - b, 0.0))
                acc = jnp.where(col < lo, _dot_nt(q_t.astype(BF), k_t.astype(BF)), 0.0)
            else:
                acc = jnp.zeros((sub, c), jnp.float32)
            for j in range(sub):
                t = q_i * k_i[j:j + 1] * jnp.exp(jnp.minimum(b_i - b_i[j:j + 1], 0.0))
                cs = jnp.sum(t, axis=-1, keepdims=True)
                acc = acc + jnp.where((col == lo + j) & (row >= j), cs, 0.0)
            a_rows.append(acc)
        attn = jnp.concatenate(a_rows, axis=0)
        st = state_ref[h]
        o = jnp.dot(attn.astype(BF), v.astype(BF), preferred_element_type=jnp.float32)
        o = o + _dot_nt((q * jnp.exp(b)).astype(BF), st.astype(BF))
        b_last = b[c - 1:c]
        k_d = k * jnp.exp(b_last - b)
        upd = lax.dot_general(v.astype(BF), k_d.astype(BF), (((0,), (0,)), ((), ())),
                              preferred_element_type=jnp.float32)
        state_ref[h] = st * jnp.exp(b_last) + upd
        o = o * lax.rsqrt(jnp.mean(o * o, axis=-1, keepdims=True) + LN_EPS)
        r = r_ref[:, h * dv:(h + 1) * dv]
        o_ref[:, h * dv:(h + 1) * dv] = o * g_ref[...] * (r / (1.0 + jnp.exp(-r)))


def gla_mixer_pallas(u_gla, u_small, w_alpha2, b_alpha, norm_g):
    bsz, seq, _ = u_gla.shape
    c = GLA_CHUNK

    def tok(width, blk):
        return pl.BlockSpec((None, c, width), lambda b, i: (b, i, blk))

    def whole(shape):
        return pl.BlockSpec(shape, lambda b, i: (0,) * len(shape))

    wa = jnp.pad(w_alpha2, ((0, V7X_LANES - GLA_RANK), (0, 0)))
    return pl.pallas_call(
        _gla_kernel,
        out_shape=jax.ShapeDtypeStruct((bsz, seq, B_V), jnp.float32),
        grid=(bsz, seq // c),
        in_specs=[tok(B_K, 0), tok(B_K, 1), tok(B_V, 1), tok(B_V, 2), tok(V7X_LANES, 0),
                  whole((V7X_LANES, B_K)), whole((1, B_K)), whole((1, GLA_DV))],
        out_specs=tok(B_V, 0),
        scratch_shapes=[pltpu.VMEM((GLA_HEADS, GLA_DV, GLA_DK), jnp.float32)],
        compiler_params=_cparams(2),
        name="gla_mixer",
    )(u_gla, u_gla, u_gla, u_gla, u_small, wa, b_alpha.reshape(1, -1), norm_g.reshape(1, -1))


CMP_HALF = NSA_CMP_STRIDE * HEAD_DIM


def _gelu_tanh(x):
    return 0.5 * x * (1.0 + jnp.tanh(np.sqrt(2.0 / np.pi) * (x + 0.044715 * (x * x * x))))


def _compress_kernel(hk_ref, hv_ref, pek_ref, w1k_ref, w2k_ref, pev_ref, w1v_ref, w2v_ref, kc_ref, vc_ref):
    for h_ref, pe_ref, w1_ref, w2_ref, o_ref in ((hk_ref, pek_ref, w1k_ref, w2k_ref, kc_ref),
                                                 (hv_ref, pev_ref, w1v_ref, w2v_ref, vc_ref)):
        w1a = w1_ref[0].astype(BF)
        w1b = w1_ref[1].astype(BF)
        pe = pe_ref[...].astype(BF)
        pe_term = (jnp.dot(pe[:, :CMP_HALF], w1a, preferred_element_type=jnp.float32)
                   + jnp.dot(pe[:, CMP_HALF:], w1b, preferred_element_type=jnp.float32))
        w2 = w2_ref[...].astype(BF)
        for g in range(NSA_KV_GROUPS):
            hb = h_ref[g].astype(BF)
            y1 = jnp.dot(hb, w1a, preferred_element_type=jnp.float32)
            y2 = jnp.dot(hb, w1b, preferred_element_type=jnp.float32)
            n_half = y2.shape[0]
            pre = y1 + pltpu.roll(y2, n_half - 1, axis=0) + pe_term
            out = jnp.dot(_gelu_tanh(pre).astype(BF), w2, preferred_element_type=jnp.float32)
            o_ref[g] = jnp.concatenate([out, out], axis=1)


def nsa_compress_pallas(u, k_col, v_col, pe_k, w1_k, w2_k, pe_v, w1_v, w2_v):
    bsz, seq, _ = u.shape
    n_half = seq // NSA_CMP_STRIDE

    def halves(col):
        t = u[..., col:col + C_KV].reshape(bsz, n_half, NSA_CMP_STRIDE, NSA_KV_GROUPS, HEAD_DIM)
        return t.transpose(0, 3, 1, 2, 4).reshape(bsz, NSA_KV_GROUPS, n_half, CMP_HALF)

    def whole(shape):
        return pl.BlockSpec(shape, lambda b: (0,) * len(shape))

    hspec = pl.BlockSpec((None, NSA_KV_GROUPS, n_half, CMP_HALF), lambda b: (b, 0, 0, 0))
    ospec = pl.BlockSpec((None, NSA_KV_GROUPS, n_half, V7X_LANES), lambda b: (b, 0, 0, 0))
    wspecs = [whole((1, 2 * CMP_HALF)), whole((2, CMP_HALF, NSA_CMP_HIDDEN)), whole((NSA_CMP_HIDDEN, HEAD_DIM))]
    sds = jax.ShapeDtypeStruct((bsz, NSA_KV_GROUPS, n_half, V7X_LANES), jnp.float32)
    return pl.pallas_call(
        _compress_kernel,
        out_shape=(sds, sds),
        grid=(bsz,),
        in_specs=[hspec, hspec] + wspecs + wspecs,
        out_specs=(ospec, ospec),
        compiler_params=_cparams(1),
        name="nsa_compress",
    )(halves(k_col), halves(v_col),
      pe_k.reshape(1, -1), w1_k.reshape(2, CMP_HALF, NSA_CMP_HIDDEN), w2_k,
      pe_v.reshape(1, -1), w1_v.reshape(2, CMP_HALF, NSA_CMP_HIDDEN), w2_v)


CMP_TQ = 256


def _overlap_matrix(n_cmp_pad, n_blk):
    c0 = np.arange(n_cmp_pad)[:, None] * NSA_CMP_STRIDE
    s0 = np.arange(n_blk)[None, :] * NSA_SEL_LEN
    return ((c0 < s0 + NSA_SEL_LEN) & (c0 + NSA_CMP_LEN > s0)).astype(np.float32)


def _cmp_attn_kernel(q_ref, kc_ref, vc_ref, ov_ref, o_ref, m_ref):
    i = pl.program_id(2)
    tq = CMP_TQ
    n_c = kc_ref.shape[0]
    n_blk = ov_ref.shape[0]
    kc = kc_ref[...].astype(BF)
    vc = vc_ref[...].astype(BF)
    pos = i * tq + lax.broadcasted_iota(jnp.int32, (tq, n_c), 0)
    c_end = lax.broadcasted_iota(jnp.int32, (tq, n_c), 1) * NSA_CMP_STRIDE + (NSA_CMP_LEN - 1)
    cvalid = c_end <= pos
    lane = lax.broadcasted_iota(jnp.int32, (tq, V7X_LANES), 1)
    p_sum = jnp.zeros((tq, n_c), jnp.float32)
    outs = []
    for rr in range(NSA_GROUP_HEADS):
        qc = q_ref[:, (rr // 2) * V7X_LANES:(rr // 2 + 1) * V7X_LANES].astype(jnp.float32) * (HEAD_DIM ** -0.5)
        qh = jnp.where(_lane_half_mask(qc.shape, rr % 2), qc, 0.0).astype(BF)
        s = jnp.where(cvalid, _dot_nt(qh, kc), NEG_BIG)
        m = jnp.max(s, axis=-1, keepdims=True)
        p = jnp.where(cvalid, jnp.exp(s - m), 0.0)
        p = p / jnp.maximum(jnp.sum(p, axis=-1, keepdims=True), 1e-30)
        p_sum = p_sum + p
        outs.append(jnp.dot(p.astype(BF), vc, preferred_element_type=jnp.float32))
    for c in range(NSA_GROUP_HEADS // 2):
        o_ref[:, c * V7X_LANES:(c + 1) * V7X_LANES] = jnp.where(lane < HEAD_DIM, outs[2 * c], outs[2 * c + 1])
    ps_hi, ps_lo = _split_bf16(p_sum)
    imp = _dot_nt(ov_ref[...], ps_hi) + _dot_nt(ov_ref[...], ps_lo)
    j = lax.broadcasted_iota(jnp.int32, (n_blk, tq), 0)
    q_blk = (i * tq + lax.broadcasted_iota(jnp.int32, (n_blk, tq), 1)) // NSA_SEL_LEN
    forced = (j == 0) | (j == q_blk) | (j == q_blk - 1)
    imp = jnp.where(forced, jnp.inf, jnp.where(j > q_blk, -jnp.inf, imp))
    rank = jnp.zeros((n_blk, tq), jnp.float32)
    for jj in range(n_blk):
        other = imp[jj:jj + 1, :]
        beats = (other > imp) | ((other == imp) & (jj < j))
        rank = rank + jnp.where(beats, 1.0, 0.0)
    m_ref[...] = jnp.where(rank < float(min(NSA_N_SELECT, n_blk)), 1.0, 0.0).astype(m_ref.dtype)


def nsa_cmp_attention_pallas(u, q_col, kc, vc):
    bsz, seq, _ = u.shape
    n_c = kc.shape[2]
    n_blk = seq // NSA_SEL_LEN
    gw = NSA_GROUP_HEADS * HEAD_DIM
    qb = q_col // gw
    cspec = pl.BlockSpec((None, None, n_c, V7X_LANES), lambda b, g, i: (b, g, 0, 0))
    return pl.pallas_call(
        _cmp_attn_kernel,
        out_shape=(jax.ShapeDtypeStruct((bsz, seq, NSA_HEADS * HEAD_DIM), jnp.float32),
                   jax.ShapeDtypeStruct((bsz, NSA_KV_GROUPS, n_blk, seq), BF)),
        grid=(bsz, NSA_KV_GROUPS, seq // CMP_TQ),
        in_specs=[pl.BlockSpec((None, CMP_TQ, gw), lambda b, g, i: (b, i, qb + g)),
                  cspec, cspec,
                  pl.BlockSpec((n_blk, n_c), lambda b, g, i: (0, 0))],
        out_specs=(pl.BlockSpec((None, CMP_TQ, gw), lambda b, g, i: (b, i, g)),
                   pl.BlockSpec((None, None, n_blk, CMP_TQ), lambda b, g, i: (b, g, 0, i))),
        compiler_params=_cparams(3),
        name="nsa_cmp_attention",
    )(u, kc, vc, jnp.asarray(_overlap_matrix(n_c, n_blk).T, BF))


MERGE_TM = 256


def _layer_norm_rows(x, g, b):
    mu = jnp.mean(x, axis=-1, keepdims=True)
    xc = x - mu
    var = jnp.mean(xc * xc, axis=-1, keepdims=True)
    return xc * lax.rsqrt(var + LN_EPS) * g + b


def _gate_expand_matrices():
    e = np.zeros((NSA_BRANCHES, V7X_LANES, NSA_HEADS * HEAD_DIM), np.float32)
    for br in range(NSA_BRANCHES):
        for h in range(NSA_HEADS):
            e[br, GLA_RANK + NSA_BRANCHES * h + br, h * HEAD_DIM:(h + 1) * HEAD_DIM] = 1.0
    return e


def _sigmoid(x):
    return 1.0 / (1.0 + jnp.exp(-x))


def _merge_kernel(x_ref, o0_ref, o1_ref, o2_ref, l0_ref, l1_ref, l2_ref, yb_ref, oc_ref, os_ref, ow_ref,
                  us_ref, mg_ref, ex_ref, wa_ref, wb_ref, wc_ref, wo_ref, g_ref, b_ref, out_ref):
    l0, l1, l2 = l0_ref[...], l1_ref[...], l2_ref[...]
    lm = jnp.maximum(jnp.maximum(l0, l1), l2)
    e0, e1, e2 = jnp.exp(l0 - lm), jnp.exp(l1 - lm), jnp.exp(l2 - lm)
    y_a = (e0 * o0_ref[...] + e1 * o1_ref[...] + e2 * o2_ref[...]) / (e0 + e1 + e2)
    sg = _sigmoid(us_ref[...])
    y_c = (_dot_split_lhs(sg, ex_ref[0]) * oc_ref[...] + _dot_split_lhs(sg, ex_ref[1]) * os_ref[...]
           + _dot_split_lhs(sg, ex_ref[2]) * ow_ref[...])
    d = D_MODEL
    f = functools.partial(jnp.dot, preferred_element_type=jnp.float32)
    merged = (_sigmoid(mg_ref[:, :d]) * f(y_a.astype(BF), wa_ref[...])
              + _sigmoid(mg_ref[:, d:2 * d]) * f(yb_ref[...].astype(BF), wb_ref[...])
              + _sigmoid(mg_ref[:, 2 * d:]) * f(y_c.astype(BF), wc_ref[...]))
    mix = f(merged.astype(BF), wo_ref[...])
    out_ref[...] = _layer_norm_rows(DEEPNORM_ALPHA * x_ref[...] + mix, g_ref[...], b_ref[...])


def mixer_merge_pallas(x, dil_o, dil_lse, y_b, o_cmp, o_sel, o_win, u_small, m_g, w_br_a, w_br_b, w_br_c, w_o_mix, ln_g, ln_b):
    n, d = x.shape
    tm = MERGE_TM
    aw = DIL_HEADS * HEAD_DIM

    def tok(width):
        return pl.BlockSpec((tm, width), lambda i: (i, 0))

    def whole(shape):
        return pl.BlockSpec(shape, lambda i: (0,) * len(shape))

    ex = jnp.asarray(_gate_expand_matrices(), BF)
    return pl.pallas_call(
        _merge_kernel,
        out_shape=jax.ShapeDtypeStruct((n, d), jnp.float32),
        grid=(n // tm,),
        in_specs=[tok(d)] + [tok(aw)] * 6 + [tok(B_V)] + [tok(C_Q)] * 3 + [tok(V7X_LANES), tok(N_BRANCHES * d),
                  whole(ex.shape), whole((aw, d)), whole((B_V, d)), whole((C_Q, d)), whole((d, d)),
                  whole((1, d)), whole((1, d))],
        out_specs=tok(d),
        compiler_params=_cparams(1),
        name="mixer_merge",
    )(x, *dil_o, *dil_lse, y_b, o_cmp, o_sel, o_win, u_small, m_g, ex,
      w_br_a.astype(BF), w_br_b.astype(BF), w_br_c.astype(BF), w_o_mix.astype(BF),
      ln_g.reshape(1, d), ln_b.reshape(1, d))


XATTN_TM = 256


def _xattn_kernel(x_ref, k_ref, v_ref, wq_ref, wo_ref, g_ref, b_ref, o_ref):
    x = x_ref[...]
    d = x.shape[-1]
    hd = d // XATTN_HEADS
    q = jnp.dot(x.astype(BF), wq_ref[...], preferred_element_type=jnp.float32)
    outs = []
    for h in range(XATTN_HEADS):
        sl = slice(h * hd, (h + 1) * hd)
        s = _dot_nt((q[:, sl] * (hd ** -0.5)).astype(BF), k_ref[:, sl].astype(BF))
        p = jnp.exp(s - jnp.max(s, axis=-1, keepdims=True))
        den = jnp.sum(p, axis=-1, keepdims=True)
        outs.append(jnp.dot(p.astype(BF), v_ref[:, sl].astype(BF), preferred_element_type=jnp.float32) / den)
    o = jnp.concatenate(outs, axis=1)
    xa = jnp.dot(o.astype(BF), wo_ref[...], preferred_element_type=jnp.float32)
    o_ref[...] = _layer_norm_rows(DEEPNORM_ALPHA * x + xa, g_ref[...], b_ref[...])


def cross_attention_block_pallas(x, mem, w_xq, w_xk, w_xv, w_xo, ln_g, ln_b):
    bsz, seq, d = x.shape
    mem_len = mem.shape[1]
    mem2 = mem.reshape(-1, d)
    k = _mm(mem2, w_xk)
    v = _mm(mem2, w_xv)
    n_t = seq // XATTN_TM

    def whole(shape):
        return pl.BlockSpec(shape, lambda b, i: (0,) * len(shape))

    out = pl.pallas_call(
        _xattn_kernel,
        out_shape=jax.ShapeDtypeStruct((bsz * seq, d), jnp.float32),
        grid=(bsz, n_t),
        in_specs=[pl.BlockSpec((XATTN_TM, d), lambda b, i: (b * n_t + i, 0)),
                  pl.BlockSpec((mem_len, d), lambda b, i: (b, 0)),
                  pl.BlockSpec((mem_len, d), lambda b, i: (b, 0)),
                  whole((d, d)), whole((d, d)), whole((1, d)), whole((1, d))],
        out_specs=pl.BlockSpec((XATTN_TM, d), lambda b, i: (b * n_t + i, 0)),
        compiler_params=_cparams(2),
        name="cross_attention_block",
    )(x.reshape(-1, d), k, v, w_xq.astype(BF), w_xo.astype(BF), ln_g.reshape(1, d), ln_b.reshape(1, d))
    return out.reshape(bsz, seq, d)


def _ln_kernel(x_ref, g_ref, b_ref, o_ref, obf_ref):
    y = _layer_norm_rows(x_ref[...], g_ref[...], b_ref[...])
    o_ref[...] = y
    obf_ref[...] = y.astype(BF)


def layer_norm_pallas(x, g, b):
    n, d = x.shape
    tm = _pick_tile(n, (1024, 512, 256, 128, 64, 32, 16, 8))
    spec = pl.BlockSpec((tm, d), lambda i: (i, 0))
    return pl.pallas_call(
        _ln_kernel,
        out_shape=(jax.ShapeDtypeStruct((n, d), jnp.float32), jax.ShapeDtypeStruct((n, d), BF)),
        grid=(n // tm,),
        in_specs=[spec, pl.BlockSpec((1, d), lambda i: (0, 0)), pl.BlockSpec((1, d), lambda i: (0, 0))],
        out_specs=(spec, spec),
        compiler_params=_cparams(1),
        name="layer_norm",
    )(x, g.reshape(1, d), b.reshape(1, d))


MOE_TB = 256


def _moe_ffn_kernel(be_ref, nxt_ref, nused_ref, x_ref, wgu_hbm, bgu_ref, wd_hbm, bd_ref, o_ref,
                    wgu_st, wd_st, wgu_bf, wd_bf, sem):
    i = pl.program_id(0)
    e = be_ref[i]
    changed = jnp.logical_or(i == 0, e != be_ref[jnp.maximum(i - 1, 0)])

    def fetch(expert):
        return (pltpu.make_async_copy(wgu_hbm.at[expert], wgu_st, sem.at[0]),
                pltpu.make_async_copy(wd_hbm.at[expert], wd_st, sem.at[1]))

    @pl.when(i == 0)
    def _():
        for cp in fetch(e):
            cp.start()

    @pl.when(changed)
    def _():
        for cp in fetch(e):
            cp.wait()
        wgu_bf[...] = wgu_st[...].astype(BF)
        wd_bf[...] = wd_st[...].astype(BF)

        @pl.when(nxt_ref[i] != e)
        def _():
            for cp in fetch(nxt_ref[i]):
                cp.start()

    @pl.when(i < nused_ref[0])
    def _():
        gu = jnp.dot(x_ref[...].astype(BF), wgu_bf[...], preferred_element_type=jnp.float32) + bgu_ref[...]
        gate = jnp.minimum(gu[:, :D_FF], SWIGLU_LIMIT)
        up = jnp.clip(gu[:, D_FF:], -SWIGLU_LIMIT, SWIGLU_LIMIT)
        act = (up + 1.0) * gate * (1.0 / (1.0 + jnp.exp(-SWIGLU_ALPHA * gate)))
        o_ref[...] = jnp.dot(act.astype(BF), wd_bf[...], preferred_element_type=jnp.float32) + bd_ref[...]

    @pl.when(i >= nused_ref[0])
    def _():
        o_ref[...] = jnp.zeros(o_ref.shape, o_ref.dtype)


def moe_expert_ffn_pallas(buf, block_e, next_e, n_used, w_gu, b_gu, w_down, b_down):
    n_rows, d = buf.shape
    n_e = w_gu.shape[0]
    grid_spec = pltpu.PrefetchScalarGridSpec(
        num_scalar_prefetch=3,
        grid=(n_rows // MOE_TB,),
        in_specs=[
            pl.BlockSpec((MOE_TB, d), lambda i, be, nx, nu: (i, 0)),
            pl.BlockSpec(memory_space=pl.ANY),
            pl.BlockSpec((None, 1, 2 * D_FF), lambda i, be, nx, nu: (be[i], 0, 0)),
            pl.BlockSpec(memory_space=pl.ANY),
            pl.BlockSpec((None, 1, d), lambda i, be, nx, nu: (be[i], 0, 0)),
        ],
        out_specs=pl.BlockSpec((MOE_TB, d), lambda i, be, nx, nu: (i, 0)),
        scratch_shapes=[pltpu.VMEM((d, 2 * D_FF), jnp.float32), pltpu.VMEM((D_FF, d), jnp.float32),
                        pltpu.VMEM((d, 2 * D_FF), BF), pltpu.VMEM((D_FF, d), BF),
                        pltpu.SemaphoreType.DMA((2,))],
    )
    return pl.pallas_call(
        _moe_ffn_kernel,
        out_shape=jax.ShapeDtypeStruct((n_rows, d), jnp.float32),
        grid_spec=grid_spec,
        compiler_params=_cparams(1),
        name="moe_expert_ffn",
    )(block_e, next_e, n_used, buf, w_gu, b_gu.reshape(n_e, 1, -1), w_down, b_down.reshape(n_e, 1, -1))


MOE_ROUTE_TM = 256


def _router_kernel(x_ref, wr_ref, br_ref, route_ref, counts_ref, carry_ref):
    tm = MOE_ROUTE_TM

    @pl.when(pl.program_id(0) == 0)
    def _():
        carry_ref[...] = jnp.zeros(carry_ref.shape, jnp.float32)

    logits = _dot_split(x_ref[...], wr_ref[...]) + br_ref[...]
    lane = lax.broadcasted_iota(jnp.int32, (tm, N_EXPERTS), 1).astype(jnp.float32)
    onehots, vals, idxs = [], [], []
    rest = logits
    for _ in range(TOP_K):
        m = jnp.max(rest, axis=-1, keepdims=True)
        idx = jnp.min(jnp.where(rest == m, lane, float(N_EXPERTS)), axis=-1, keepdims=True)
        hit = lane == idx
        onehots.append(hit)
        vals.append(m)
        idxs.append(idx)
        rest = jnp.where(hit, -jnp.inf, rest)
    exps = [jnp.exp(v - vals[0]) for v in vals]
    den = exps[0] + exps[1] + exps[2] + exps[3]
    cnt = jnp.zeros((tm, N_EXPERTS), jnp.float32)
    for hit in onehots:
        cnt = cnt + jnp.where(hit, 1.0, 0.0)
    earlier = (lax.broadcasted_iota(jnp.int32, (tm, tm), 1) < lax.broadcasted_iota(jnp.int32, (tm, tm), 0))
    before = jnp.dot(earlier.astype(BF), cnt.astype(BF), preferred_element_type=jnp.float32) + carry_ref[...]
    lane_out = lax.broadcasted_iota(jnp.int32, (tm, V7X_LANES), 1)
    out = jnp.zeros((tm, V7X_LANES), jnp.float32)
    for k in range(TOP_K):
        pos = jnp.sum(jnp.where(onehots[k], before, 0.0), axis=-1, keepdims=True)
        out = jnp.where(lane_out == k, idxs[k], out)
        out = jnp.where(lane_out == TOP_K + k, exps[k] / den, out)
        out = jnp.where(lane_out == 2 * TOP_K + k, pos, out)
    route_ref[...] = out
    carry_ref[...] = carry_ref[...] + jnp.sum(cnt, axis=0, keepdims=True)
    counts_ref[...] = carry_ref[...]


def moe_router_pallas(x, w_router, b_router):
    n, d = x.shape
    tm = MOE_ROUTE_TM
    return pl.pallas_call(
        _router_kernel,
        out_shape=(jax.ShapeDtypeStruct((n, V7X_LANES), jnp.float32),
                   jax.ShapeDtypeStruct((1, N_EXPERTS), jnp.float32)),
        grid=(n // tm,),
        in_specs=[pl.BlockSpec((tm, d), lambda i: (i, 0)),
                  pl.BlockSpec((d, N_EXPERTS), lambda i: (0, 0)),
                  pl.BlockSpec((1, N_EXPERTS), lambda i: (0, 0))],
        out_specs=(pl.BlockSpec((tm, V7X_LANES), lambda i: (i, 0)),
                   pl.BlockSpec((1, N_EXPERTS), lambda i: (0, 0))),
        scratch_shapes=[pltpu.VMEM((1, N_EXPERTS), jnp.float32)],
        compiler_params=_cparams(1),
        name="moe_router",
    )(x, w_router, b_router.reshape(1, -1))


def _row_copy(src, dst, sem):
    return pltpu.make_async_copy(src, dst, sem)


def _dispatch_kernel(dest_ref, x_ref, zeros_ref, buf_ref, sem):
    del zeros_ref
    tm = MOE_ROUTE_TM

    def body(t, carry):
        for k in range(TOP_K):
            row = dest_ref[0, TOP_K * t + k]
            _row_copy(x_ref.at[pl.ds(t, 1)], buf_ref.at[pl.ds(row, 1)], sem).start(priority=k % 2)
        return carry

    lax.fori_loop(0, tm, body, 0)
    for _ in range(TOP_K):
        _row_copy(x_ref, buf_ref.at[pl.ds(0, tm)], sem).wait()


def moe_dispatch_pallas(x, dest, n_rows):
    n, d = x.shape
    tm = MOE_ROUTE_TM
    return pl.pallas_call(
        _dispatch_kernel,
        out_shape=jax.ShapeDtypeStruct((n_rows, d), jnp.float32),
        grid=(n // tm,),
        in_specs=[pl.BlockSpec((None, 1, TOP_K * tm), lambda i: (i, 0, 0), memory_space=pltpu.SMEM),
                  pl.BlockSpec((tm, d), lambda i: (i, 0)),
                  pl.BlockSpec(memory_space=pl.ANY)],
        out_specs=pl.BlockSpec(memory_space=pl.ANY),
        scratch_shapes=[pltpu.SemaphoreType.DMA],
        input_output_aliases={2: 0},
        compiler_params=_cparams(1),
        name="moe_dispatch",
    )(dest, x, jnp.zeros((n_rows, d), jnp.float32))


def _combine_kernel(dest_ref, x_ref, route_ref, obuf_ref, g_ref, b_ref, o_ref, obf_ref, rows_ref, sem):
    tm = MOE_ROUTE_TM

    def body(t, carry):
        for k in range(TOP_K):
            row = dest_ref[0, TOP_K * t + k]
            _row_copy(obuf_ref.at[pl.ds(row, 1)], rows_ref.at[k, pl.ds(t, 1)], sem).start(priority=k % 2)
        return carry

    lax.fori_loop(0, tm, body, 0)
    for k in range(TOP_K):
        _row_copy(obuf_ref.at[pl.ds(0, tm)], rows_ref.at[k], sem).wait()
    ff = jnp.zeros(x_ref.shape, jnp.float32)
    for k in range(TOP_K):
        ff = ff + route_ref[:, TOP_K + k:TOP_K + k + 1] * rows_ref[k]
    y = _layer_norm_rows(DEEPNORM_ALPHA * x_ref[...] + ff, g_ref[...], b_ref[...])
    o_ref[...] = y
    obf_ref[...] = y.astype(BF)


def moe_combine_pallas(x, route, dest, out_buf, ln_g, ln_b):
    n, d = x.shape
    tm = MOE_ROUTE_TM
    return pl.pallas_call(
        _combine_kernel,
        out_shape=(jax.ShapeDtypeStruct((n, d), jnp.float32), jax.ShapeDtypeStruct((n, d), BF)),
        grid=(n // tm,),
        in_specs=[pl.BlockSpec((None, 1, TOP_K * tm), lambda i: (i, 0, 0), memory_space=pltpu.SMEM),
                  pl.BlockSpec((tm, d), lambda i: (i, 0)),
                  pl.BlockSpec((tm, V7X_LANES), lambda i: (i, 0)),
                  pl.BlockSpec(memory_space=pl.ANY),
                  pl.BlockSpec((1, d), lambda i: (0, 0)), pl.BlockSpec((1, d), lambda i: (0, 0))],
        out_specs=(pl.BlockSpec((tm, d), lambda i: (i, 0)), pl.BlockSpec((tm, d), lambda i: (i, 0))),
        scratch_shapes=[pltpu.VMEM((TOP_K, tm, d), jnp.float32), pltpu.SemaphoreType.DMA],
        compiler_params=_cparams(1),
        name="moe_combine",
    )(dest, x, route, out_buf, ln_g.reshape(1, d), ln_b.reshape(1, d))


def token_mixer_block(h, h_bf, w_in, b_in, w_alpha2, b_alpha, gla_norm_g, cmp_pe_k, cmp_w1_k, cmp_w2_k,
                      cmp_pe_v, cmp_w1_v, cmp_w2_v, w_br_a, w_br_b, w_br_c, w_o_mix, ln_g, ln_b):
    bsz, seq, d = h.shape
    offs = np.concatenate([[0], np.cumsum(IN_WIDTHS)]).tolist()
    h2 = h.reshape(-1, d)

    def proj(lo, hi, out_dtype=jnp.float32):
        return _mm(h_bf, w_in[:, offs[lo]:offs[hi]], b_in[offs[lo]:offs[hi]], out_dtype).reshape(bsz, seq, -1)

    u_dil = proj(0, 3)
    u_gla = proj(3, 7)
    w_small = jnp.concatenate([w_in[:, offs[7]:offs[8]], w_in[:, offs[15]:offs[16]]], axis=1)
    b_small = jnp.concatenate([b_in[offs[7]:offs[8]], b_in[offs[15]:offs[16]]])
    n_small = w_small.shape[1]
    w_small = jnp.pad(w_small, ((0, 0), (0, V7X_LANES - n_small)))
    b_small = jnp.pad(b_small, (0, V7X_LANES - n_small))
    u_small = _mm(h_bf, w_small, b_small)
    u_c = proj(8, 15, BF)
    m_g = _mm(h_bf, w_in[:, offs[16]:offs[17]], b_in[offs[16]:offs[17]])
    dil_o, dil_lse = dilated_attention_pallas(u_dil, 0, A_W, 2 * A_W)
    y_b = gla_mixer_pallas(u_gla, u_small.reshape(bsz, seq, V7X_LANES), w_alpha2, b_alpha, gla_norm_g)
    kc, vc = nsa_compress_pallas(u_c, C_Q, C_Q + C_KV, cmp_pe_k, cmp_w1_k, cmp_w2_k, cmp_pe_v, cmp_w1_v, cmp_w2_v)
    o_cmp, selmask = nsa_cmp_attention_pallas(u_c, 0, kc, vc)
    o_sel = nsa_gqa_attention_pallas(u_c, 0, C_Q + 2 * C_KV, C_Q + 3 * C_KV, selmask)
    o_win = nsa_gqa_attention_pallas(u_c, 0, C_Q + 4 * C_KV, C_Q + 5 * C_KV)

    def flat(t):
        return t.reshape(bsz * seq, -1)

    out = mixer_merge_pallas(h2, [flat(t) for t in dil_o], [flat(t) for t in dil_lse], flat(y_b), flat(o_cmp),
                             flat(o_sel), flat(o_win), u_small, m_g, w_br_a, w_br_b, w_br_c, w_o_mix, ln_g, ln_b)
    return out.reshape(bsz, seq, d)


def moe_block(x, w_router, b_router, w_gu, b_gu, w_down, b_down, ln_g, ln_b):
    n, d = x.shape
    route, counts = moe_router_pallas(x, w_router, b_router)
    expert = route[:, :TOP_K].astype(jnp.int32)
    pos = route[:, 2 * TOP_K:3 * TOP_K].astype(jnp.int32)
    counts = counts.reshape(-1).astype(jnp.int32)
    padded = (counts + MOE_TB - 1) // MOE_TB * MOE_TB
    pend = jnp.cumsum(padded)
    pstart = pend - padded
    dest = (pstart[expert] + pos).reshape(n // MOE_ROUTE_TM, 1, TOP_K * MOE_ROUTE_TM)
    n_rows = n * TOP_K + N_EXPERTS * MOE_TB
    n_blocks = n_rows // MOE_TB
    n_used = (pend[-1:] // MOE_TB).astype(jnp.int32)
    blk = jnp.minimum(jnp.arange(n_blocks), n_used[0] - 1) * MOE_TB
    block_e = jnp.minimum(jnp.searchsorted(pend, blk, side='right'), N_EXPERTS - 1).astype(jnp.int32)
    ids = jnp.arange(N_EXPERTS, dtype=jnp.int32)
    nonempty_at_or_after = lax.cummin(jnp.where(counts > 0, ids, N_EXPERTS), reverse=True)
    next_of = jnp.concatenate([nonempty_at_or_after[1:], jnp.full((1,), N_EXPERTS, jnp.int32)])
    next_of = jnp.where(next_of < N_EXPERTS, next_of, ids)
    buf = moe_dispatch_pallas(x, dest, n_rows)
    out = moe_expert_ffn_pallas(buf, block_e, next_of[block_e], n_used, w_gu, b_gu, w_down, b_down)
    return moe_combine_pallas(x, route, dest, out, ln_g, ln_b)


def kernel(x, mem, ln0_g, ln0_b, w_in, b_in, w_alpha2, b_alpha, gla_norm_g, cmp_pe_k, cmp_w1_k, cmp_w2_k, cmp_pe_v, cmp_w1_v, cmp_w2_v, w_br_a, w_br_b, w_br_c, w_o_mix, ln1_g, ln1_b, w_xq, w_xk, w_xv, w_xo, ln2_g, ln2_b, w_router, b_router, w_gu, b_gu, w_down, b_down, ln3_g, ln3_b):
    shape = x.shape
    x, x_bf = layer_norm_pallas(x.reshape(-1, shape[-1]), ln0_g, ln0_b)
    for li in range(DEPTH):
        x = token_mixer_block(x.reshape(shape), x_bf, w_in[li], b_in[li], w_alpha2[li], b_alpha[li], gla_norm_g[li],
                              cmp_pe_k[li], cmp_w1_k[li], cmp_w2_k[li], cmp_pe_v[li], cmp_w1_v[li], cmp_w2_v[li],
                              w_br_a[li], w_br_b[li], w_br_c[li], w_o_mix[li], ln1_g[li], ln1_b[li])
        x = cross_attention_block_pallas(x, mem, w_xq[li], w_xk[li], w_xv[li], w_xo[li], ln2_g[li], ln2_b[li])
        x, x_bf = moe_block(x.reshape(-1, shape[-1]), w_router[li], b_router[li], w_gu[li], b_gu[li], w_down[li],
                            b_down[li], ln3_g[li], ln3_b[li])
    return x.reshape(shape)
```

```python
import functools

import jax
import jax.numpy as jnp
import numpy as np
from jax import lax
from jax.experimental import pallas as pl
from jax.experimental.pallas import tpu as pltpu

D_MODEL = 1024
DEPTH = 2
HEAD_DIM = 64
N_BRANCHES = 3
LN_EPS = 1e-5

DIL_GROUPS = ((128, 1), (512, 4), (2048, 16))
DIL_HEADS = 4

GLA_HEADS = 4
GLA_DK = D_MODEL // 2 // GLA_HEADS
GLA_DV = D_MODEL // GLA_HEADS
GLA_RANK = 16
GLA_TAU = 16.0
GLA_CHUNK = 64

NSA_HEADS = 16
NSA_KV_GROUPS = 4
NSA_GROUP_HEADS = NSA_HEADS // NSA_KV_GROUPS
NSA_BRANCHES = 3
NSA_CMP_LEN = 32
NSA_CMP_STRIDE = 16
NSA_CMP_HIDDEN = 2 * HEAD_DIM
NSA_SEL_LEN = 64
NSA_N_SELECT = 16
NSA_WINDOW = 512

XATTN_HEADS = 4

N_EXPERTS = 32
TOP_K = 4
D_FF = D_MODEL
SWIGLU_LIMIT = 7.0
SWIGLU_ALPHA = 1.702

DEEPNORM_ALPHA = (2 * DEPTH) ** 0.25

A_W = len(DIL_GROUPS) * DIL_HEADS * HEAD_DIM
B_K = GLA_HEADS * GLA_DK
B_V = GLA_HEADS * GLA_DV
C_Q = NSA_HEADS * HEAD_DIM
C_KV = NSA_KV_GROUPS * HEAD_DIM
IN_WIDTHS = (A_W, A_W, A_W,
             B_K, B_K, B_V, B_V, GLA_RANK,
             C_Q, C_KV, C_KV, C_KV, C_KV, C_KV, C_KV, NSA_HEADS * NSA_BRANCHES,
             N_BRANCHES * D_MODEL)

V7X_VMEM_LIMIT_BYTES = 48 * 1024 * 1024
V7X_LANES = 128
NEG_BIG = -1e30
BF = jnp.bfloat16


def _cparams(n_axes):
    return pltpu.CompilerParams(dimension_semantics=("arbitrary",) * n_axes,
                                vmem_limit_bytes=V7X_VMEM_LIMIT_BYTES)


def _split_bf16(x):
    hi = x.astype(BF)
    lo = (x - hi.astype(jnp.float32)).astype(BF)
    return hi, lo


def _dot_split(a, b):
    a_hi, a_lo = _split_bf16(a)
    b_hi, b_lo = _split_bf16(b)
    f = functools.partial(jnp.dot, preferred_element_type=jnp.float32)
    return f(a_hi, b_hi) + (f(a_hi, b_lo) + f(a_lo, b_hi))


def _dot_split_lhs(a, b_exact):
    a_hi, a_lo = _split_bf16(a)
    f = functools.partial(jnp.dot, preferred_element_type=jnp.float32)
    return f(a_hi, b_exact) + f(a_lo, b_exact)


def _dot_nt(a, b):
    return lax.dot_general(a, b, (((1,), (1,)), ((), ())), preferred_element_type=jnp.float32)


def _mm_kernel(x_ref, w_ref, b_ref, o_ref):
    acc = jnp.dot(x_ref[...].astype(BF), w_ref[...].astype(BF), preferred_element_type=jnp.float32)
    o_ref[...] = (acc + b_ref[...]).astype(o_ref.dtype)


def _pick_tile(n, candidates):
    for c in candidates:
        if n % c == 0:
            return c
    return n


def _mm(x, w, b=None, out_dtype=jnp.float32):
    m, k = x.shape
    n = w.shape[1]
    if b is None:
        b = jnp.zeros((n,), jnp.float32)
    tm = _pick_tile(m, (1024, 512, 256, 128, 64, 32, 16, 8))
    tn = _pick_tile(n, (1536, 1280, 1152, 1024, 768, 512, 384, 256, 128))
    return pl.pallas_call(
        _mm_kernel,
        out_shape=jax.ShapeDtypeStruct((m, n), out_dtype),
        grid=(n // tn, m // tm),
        in_specs=[
            pl.BlockSpec((tm, k), lambda j, i: (i, 0)),
            pl.BlockSpec((k, tn), lambda j, i: (0, j)),
            pl.BlockSpec((1, tn), lambda j, i: (0, j)),
        ],
        out_specs=pl.BlockSpec((tm, tn), lambda j, i: (i, j)),
        compiler_params=_cparams(2),
        name="dense_proj",
    )(x, w, b.reshape(1, n))


def _mm3(x, w, b=None):
    lead = x.shape[:-1]
    return _mm(x.reshape(-1, x.shape[-1]), w, b).reshape(*lead, w.shape[1])


GQA_TQ = 256
GQA_TK_SELECT = 512
GQA_TK_BAND = 256


def _half_select(x, want_half, have_half):
    if want_half != have_half:
        x = pltpu.roll(x, HEAD_DIM, axis=1)
    return x


def _lane_half_mask(shape, half):
    lane = lax.broadcasted_iota(jnp.int32, shape, 1)
    return (lane >= half * HEAD_DIM) & (lane < (half + 1) * HEAD_DIM)


def _make_gqa_kernel(select, max_dist, n_outer):
    r, tq, tk = NSA_GROUP_HEADS, GQA_TQ, (GQA_TK_SELECT if select else GQA_TK_BAND)

    def kern(q_ref, k_ref, v_ref, *rest):
        if select:
            m_ref, e_ref, o_ref, mx_ref, acc_ref = rest
        else:
            o_ref, mx_ref, acc_ref = rest
        i = pl.program_id(n_outer)
        dist0 = i * tq + lax.broadcasted_iota(jnp.int32, (tk, tq), 1) - lax.broadcasted_iota(jnp.int32, (tk, tq), 0)
        qs = []
        for half in range(2):
            parts = []
            for rr in range(r):
                hh = half * r + rr
                qc = q_ref[:, (hh // 2) * V7X_LANES:(hh // 2 + 1) * V7X_LANES].astype(jnp.float32) * (HEAD_DIM ** -0.5)
                qc = _half_select(qc, half, hh % 2)
                parts.append(jnp.where(_lane_half_mask(qc.shape, half), qc, 0.0).astype(BF))
            qs.append(jnp.concatenate(parts, axis=0))
        mx_ref[...] = jnp.full(mx_ref.shape, NEG_BIG, jnp.float32)
        acc_ref[...] = jnp.zeros(acc_ref.shape, jnp.float32)
        v_lane = lax.broadcasted_iota(jnp.int32, (tk, V7X_LANES), 1)

        def body(j, carry):
            k0 = pl.multiple_of(j * tk, tk)
            kt = k_ref[pl.ds(k0, tk), :].astype(BF)
            vt = v_ref[pl.ds(k0, tk), :].astype(BF)
            dist = dist0 - k0
            in_band = (dist >= 0) if select else ((dist >= 0) & (dist <= max_dist))
            for half in range(2):
                if select:
                    blk = jnp.dot(e_ref[j], m_ref[half], preferred_element_type=jnp.float32)
                    valid = (blk > 0.5) & in_band
                else:
                    valid = in_band
                bias = jnp.where(valid, 0.0, NEG_BIG)
                s = _dot_nt(kt, qs[half]) + jnp.concatenate([bias] * r, axis=1)
                m_old = mx_ref[half]
                m_new = jnp.maximum(m_old, jnp.max(s, axis=0, keepdims=True))
                p = jnp.exp((s - m_new).astype(BF))
                v_aug = jnp.where((v_lane >= half * HEAD_DIM) & (v_lane < (half + 1) * HEAD_DIM), vt, 1.0)
                pv = lax.dot_general(v_aug, p, (((0,), (0,)), ((), ())), preferred_element_type=jnp.float32)
                acc_ref[half] = jnp.exp(m_old - m_new) * acc_ref[half] + pv
                mx_ref[half] = m_new
            return carry

        lo = 0 if select else jnp.maximum(i * tq - max_dist, 0) // tk
        lax.fori_loop(lo, (i * tq + tq - 1) // tk + 1, body, 0)
        outs = []
        for half in range(2):
            acc = acc_ref[half]
            den_row = (1 - half) * HEAD_DIM
            o = (acc / acc[den_row:den_row + 1, :]).T
            for rr in range(r):
                outs.append(_half_select(o[rr * tq:(rr + 1) * tq], (half * r + rr) % 2, half))
        lane = lax.broadcasted_iota(jnp.int32, (tq, V7X_LANES), 1)
        for c in range(r):
            o_ref[:, c * V7X_LANES:(c + 1) * V7X_LANES] = jnp.where(lane < HEAD_DIM, outs[2 * c], outs[2 * c + 1])

    return kern


def _sel_expand_matrix(seq):
    n_blk = seq // NSA_SEL_LEN
    e = (np.arange(seq)[:, None] // NSA_SEL_LEN == np.arange(n_blk)[None, :]).astype(np.float32)
    return e.reshape(seq // GQA_TK_SELECT, GQA_TK_SELECT, n_blk)


def nsa_gqa_attention_pallas(u, q_col, k_col, v_col, selmask=None):
    bsz, seq, _ = u.shape
    select = selmask is not None
    n_pairs = NSA_KV_GROUPS // 2
    qw = 2 * NSA_GROUP_HEADS * HEAD_DIM
    qb, kb, vb = q_col // qw, k_col // V7X_LANES, v_col // V7X_LANES
    rows = NSA_GROUP_HEADS * GQA_TQ
    in_specs = [
        pl.BlockSpec((None, GQA_TQ, qw), lambda b, gp, i: (b, i, qb + gp)),
        pl.BlockSpec((None, seq, V7X_LANES), lambda b, gp, i: (b, 0, kb + gp)),
        pl.BlockSpec((None, seq, V7X_LANES), lambda b, gp, i: (b, 0, vb + gp)),
    ]
    args = [u, u, u]
    if select:
        n_blk = selmask.shape[2]
        in_specs += [pl.BlockSpec((None, 2, n_blk, GQA_TQ), lambda b, gp, i: (b, gp, 0, i)),
                     pl.BlockSpec((seq // GQA_TK_SELECT, GQA_TK_SELECT, n_blk), lambda b, gp, i: (0, 0, 0))]
        args += [selmask, jnp.asarray(_sel_expand_matrix(seq), BF)]
    return pl.pallas_call(
        _make_gqa_kernel(select, NSA_WINDOW - 1, 2),
        out_shape=jax.ShapeDtypeStruct((bsz, seq, NSA_HEADS * HEAD_DIM), jnp.float32),
        grid=(bsz, n_pairs, seq // GQA_TQ),
        in_specs=in_specs,
        out_specs=pl.BlockSpec((None, GQA_TQ, qw), lambda b, gp, i: (b, i, gp)),
        scratch_shapes=[pltpu.VMEM((2, 1, rows), jnp.float32),
                        pltpu.VMEM((2, V7X_LANES, rows), jnp.float32)],
        compiler_params=_cparams(3),
        name="nsa_selected_attention" if select else "nsa_window_attention",
    )(*args)


DIL_TILE = 128


def _make_dil_kernel(dil, max_dist, seq):
    t = DIL_TILE
    n_sub = seq // dil // t
    n_kv = min(-(-max_dist // t) + 1, n_sub)

    def rows(first, size):
        return pl.ds(first, size, stride=dil) if dil > 1 else pl.ds(first, size)

    def kern(q_ref, k_ref, v_ref, o_ref, lse_ref):
        kw = n_kv * t
        lane = lax.broadcasted_iota(jnp.int32, (t, V7X_LANES), 1)
        for rr in range(dil):
            for i in range(n_sub):
                k_tile0 = max(i + 1 - n_kv, 0)
                dist = ((i - k_tile0) * t + lax.broadcasted_iota(jnp.int32, (t, kw), 0)
                        - lax.broadcasted_iota(jnp.int32, (t, kw), 1))
                bias = jnp.where((dist >= 0) & (dist <= max_dist), 0.0, NEG_BIG)
                q_rows = rows(rr + i * t * dil, t)
                k_rows = rows(rr + k_tile0 * t * dil, kw)
                qc = q_ref[q_rows, :] * (HEAD_DIM ** -0.5)
                kt = k_ref[k_rows, :].astype(BF)
                vt = v_ref[k_rows, :].astype(BF)
                os_, ls_ = [], []
                for half in range(2):
                    qh = jnp.where(_lane_half_mask(qc.shape, half), qc, 0.0).astype(BF)
                    s = _dot_nt(qh, kt) + bias
                    m = jnp.max(s, axis=-1, keepdims=True)
                    p = jnp.exp(s - m)
                    den = jnp.sum(p, axis=-1, keepdims=True)
                    os_.append(jnp.dot(p.astype(BF), vt, preferred_element_type=jnp.float32) / den)
                    ls_.append(jnp.broadcast_to(m + jnp.log(den), (t, V7X_LANES)))
                o_ref[q_rows, :] = jnp.where(lane < HEAD_DIM, os_[0], os_[1])
                lse_ref[q_rows, :] = jnp.where(lane < HEAD_DIM, ls_[0], ls_[1])

    return kern


def dilated_attention_pallas(u, q_col, k_col, v_col):
    bsz, seq, _ = u.shape
    gw = DIL_HEADS * HEAD_DIM
    outs, lses = [], []
    sds = jax.ShapeDtypeStruct((bsz, seq, gw), jnp.float32)
    out_spec = pl.BlockSpec((None, seq, V7X_LANES), lambda b, c: (b, 0, c))
    for gi, (window, dil) in enumerate(DIL_GROUPS):

        def cmap(col):
            blk = (col + gi * gw) // V7X_LANES
            return lambda b, c: (b, 0, blk + c)

        o, lse = pl.pallas_call(
            _make_dil_kernel(dil, window // dil, seq),
            out_shape=(sds, sds),
            grid=(bsz, gw // V7X_LANES),
            in_specs=[pl.BlockSpec((None, seq, V7X_LANES), cmap(c)) for c in (q_col, k_col, v_col)],
            out_specs=(out_spec, out_spec),
            compiler_params=_cparams(2),
            name="dilated_attention",
        )(u, u, u)
        outs.append(o)
        lses.append(lse)
    return outs, lses


GLA_SUB = 16


def _gla_kernel(q_ref, k_ref, v_ref, r_ref, lr_ref, wa_ref, ba_ref, g_ref, o_ref, state_ref):
    c, dk, dv, sub = GLA_CHUNK, GLA_DK, GLA_DV, GLA_SUB

    @pl.when(pl.program_id(1) == 0)
    def _():
        state_ref[...] = jnp.zeros(state_ref.shape, jnp.float32)

    x = _dot_split(lr_ref[...], wa_ref[...]) + ba_ref[...]
    log_a = (jnp.minimum(x, 0.0) - jnp.log1p(jnp.exp(-jnp.abs(x)))) * (1.0 / GLA_TAU)
    tri = (lax.broadcasted_iota(jnp.int32, (c, c), 0) >= lax.broadcasted_iota(jnp.int32, (c, c), 1))
    la_hi, la_lo = _split_bf16(log_a)
    tri = tri.astype(BF)
    b_all = (jnp.dot(tri, la_hi, preferred_element_type=jnp.float32)
             + jnp.dot(tri, la_lo, preferred_element_type=jnp.float32))
    col = lax.broadcasted_iota(jnp.int32, (sub, c), 1)
    row = lax.broadcasted_iota(jnp.int32, (sub, c), 0)
    for h in range(GLA_HEADS):
        q = q_ref[:, h * dk:(h + 1) * dk] * (dk ** -0.5)
        k = k_ref[:, h * dk:(h + 1) * dk]
        v = v_ref[:, h * dv:(h + 1) * dv]
        b = b_all[:, h * dk:(h + 1) * dk]
        a_rows = []
        for blk in range(c // sub):
            lo = blk * sub
            q_i, b_i, k_i = q[lo:lo + sub], b[lo:lo + sub], k[lo:lo + sub]
            if blk > 0:
                b_prev = b[lo - 1:lo]
                q_t = q_i * jnp.exp(b_i - b_prev)
                k_t = k * jnp.exp(jnp.minimum(b_prev - b, 0.0))
                acc = jnp.where(col < lo, _dot_nt(q_t.astype(BF), k_t.astype(BF)), 0.0)
            else:
                acc = jnp.zeros((sub, c), jnp.float32)
            for j in range(sub):
                t = q_i * k_i[j:j + 1] * jnp.exp(jnp.minimum(b_i - b_i[j:j + 1], 0.0))
                cs = jnp.sum(t, axis=-1, keepdims=True)
                acc = acc + jnp.where((col == lo + j) & (row >= j), cs, 0.0)
            a_rows.append(acc)
        attn = jnp.concatenate(a_rows, axis=0)
        st = state_ref[h]
        o = jnp.dot(attn.astype(BF), v.astype(BF), preferred_element_type=jnp.float32)
        o = o + _dot_nt((q * jnp.exp(b)).astype(BF), st.astype(BF))
        b_last = b[c - 1:c]
        k_d = k * jnp.exp(b_last - b)
        upd = lax.dot_general(v.astype(BF), k_d.astype(BF), (((0,), (0,)), ((), ())),
                              preferred_element_type=jnp.float32)
        state_ref[h] = st * jnp.exp(b_last) + upd
        o = o * lax.rsqrt(jnp.mean(o * o, axis=-1, keepdims=True) + LN_EPS)
        r = r_ref[:, h * dv:(h + 1) * dv]
        o_ref[:, h * dv:(h + 1) * dv] = o * g_ref[...] * (r / (1.0 + jnp.exp(-r)))


def gla_mixer_pallas(u_gla, u_small, w_alpha2, b_alpha, norm_g):
    bsz, seq, _ = u_gla.shape
    c = GLA_CHUNK

    def tok(width, blk):
        return pl.BlockSpec((None, c, width), lambda b, i: (b, i, blk))

    def whole(shape):
        return pl.BlockSpec(shape, lambda b, i: (0,) * len(shape))

    wa = jnp.pad(w_alpha2, ((0, V7X_LANES - GLA_RANK), (0, 0)))
    return pl.pallas_call(
        _gla_kernel,
        out_shape=jax.ShapeDtypeStruct((bsz, seq, B_V), jnp.float32),
        grid=(bsz, seq // c),
        in_specs=[tok(B_K, 0), tok(B_K, 1), tok(B_V, 1), tok(B_V, 2), tok(V7X_LANES, 0),
                  whole((V7X_LANES, B_K)), whole((1, B_K)), whole((1, GLA_DV))],
        out_specs=tok(B_V, 0),
        scratch_shapes=[pltpu.VMEM((GLA_HEADS, GLA_DV, GLA_DK), jnp.float32)],
        compiler_params=_cparams(2),
        name="gla_mixer",
    )(u_gla, u_gla, u_gla, u_gla, u_small, wa, b_alpha.reshape(1, -1), norm_g.reshape(1, -1))


CMP_HALF = NSA_CMP_STRIDE * HEAD_DIM


def _gelu_tanh(x):
    return 0.5 * x * (1.0 + jnp.tanh(np.sqrt(2.0 / np.pi) * (x + 0.044715 * (x * x * x))))


def _compress_kernel(hk_ref, hv_ref, pek_ref, w1k_ref, w2k_ref, pev_ref, w1v_ref, w2v_ref, kc_ref, vc_ref):
    for h_ref, pe_ref, w1_ref, w2_ref, o_ref in ((hk_ref, pek_ref, w1k_ref, w2k_ref, kc_ref),
                                                 (hv_ref, pev_ref, w1v_ref, w2v_ref, vc_ref)):
        w1a = w1_ref[0].astype(BF)
        w1b = w1_ref[1].astype(BF)
        pe = pe_ref[...].astype(BF)
        pe_term = (jnp.dot(pe[:, :CMP_HALF], w1a, preferred_element_type=jnp.float32)
                   + jnp.dot(pe[:, CMP_HALF:], w1b, preferred_element_type=jnp.float32))
        w2 = w2_ref[...].astype(BF)
        for g in range(NSA_KV_GROUPS):
            hb = h_ref[g].astype(BF)
            y1 = jnp.dot(hb, w1a, preferred_element_type=jnp.float32)
            y2 = jnp.dot(hb, w1b, preferred_element_type=jnp.float32)
            n_half = y2.shape[0]
            pre = y1 + pltpu.roll(y2, n_half - 1, axis=0) + pe_term
            out = jnp.dot(_gelu_tanh(pre).astype(BF), w2, preferred_element_type=jnp.float32)
            o_ref[g] = jnp.concatenate([out, out], axis=1)


def nsa_compress_pallas(u, k_col, v_col, pe_k, w1_k, w2_k, pe_v, w1_v, w2_v):
    bsz, seq, _ = u.shape
    n_half = seq // NSA_CMP_STRIDE

    def halves(col):
        t = u[..., col:col + C_KV].reshape(bsz, n_half, NSA_CMP_STRIDE, NSA_KV_GROUPS, HEAD_DIM)
        return t.transpose(0, 3, 1, 2, 4).reshape(bsz, NSA_KV_GROUPS, n_half, CMP_HALF)

    def whole(shape):
        return pl.BlockSpec(shape, lambda b: (0,) * len(shape))

    hspec = pl.BlockSpec((None, NSA_KV_GROUPS, n_half, CMP_HALF), lambda b: (b, 0, 0, 0))
    ospec = pl.BlockSpec((None, NSA_KV_GROUPS, n_half, V7X_LANES), lambda b: (b, 0, 0, 0))
    wspecs = [whole((1, 2 * CMP_HALF)), whole((2, CMP_HALF, NSA_CMP_HIDDEN)), whole((NSA_CMP_HIDDEN, HEAD_DIM))]
    sds = jax.ShapeDtypeStruct((bsz, NSA_KV_GROUPS, n_half, V7X_LANES), jnp.float32)
    return pl.pallas_call(
        _compress_kernel,
        out_shape=(sds, sds),
        grid=(bsz,),
        in_specs=[hspec, hspec] + wspecs + wspecs,
        out_specs=(ospec, ospec),
        compiler_params=_cparams(1),
        name="nsa_compress",
    )(halves(k_col), halves(v_col),
      pe_k.reshape(1, -1), w1_k.reshape(2, CMP_HALF, NSA_CMP_HIDDEN), w2_k,
      pe_v.reshape(1, -1), w1_v.reshape(2, CMP_HALF, NSA_CMP_HIDDEN), w2_v)


CMP_TQ = 256


def _overlap_matrix(n_cmp_pad, n_blk):
    c0 = np.arange(n_cmp_pad)[:, None] * NSA_CMP_STRIDE
    s0 = np.arange(n_blk)[None, :] * NSA_SEL_LEN
    return ((c0 < s0 + NSA_SEL_LEN) & (c0 + NSA_CMP_LEN > s0)).astype(np.float32)


def _cmp_attn_kernel(q_ref, kc_ref, vc_ref, ov_ref, o_ref, m_ref):
    i = pl.program_id(2)
    tq = CMP_TQ
    n_c = kc_ref.shape[0]
    n_blk = ov_ref.shape[0]
    kc = kc_ref[...].astype(BF)
    vc = vc_ref[...].astype(BF)
    pos = i * tq + lax.broadcasted_iota(jnp.int32, (tq, n_c), 0)
    c_end = lax.broadcasted_iota(jnp.int32, (tq, n_c), 1) * NSA_CMP_STRIDE + (NSA_CMP_LEN - 1)
    cvalid = c_end <= pos
    lane = lax.broadcasted_iota(jnp.int32, (tq, V7X_LANES), 1)
    p_sum = jnp.zeros((tq, n_c), jnp.float32)
    outs = []
    for rr in range(NSA_GROUP_HEADS):
        qc = q_ref[:, (rr // 2) * V7X_LANES:(rr // 2 + 1) * V7X_LANES].astype(jnp.float32) * (HEAD_DIM ** -0.5)
        qh = jnp.where(_lane_half_mask(qc.shape, rr % 2), qc, 0.0).astype(BF)
        s = jnp.where(cvalid, _dot_nt(qh, kc), NEG_BIG)
        m = jnp.max(s, axis=-1, keepdims=True)
        p = jnp.where(cvalid, jnp.exp(s - m), 0.0)
        p = p / jnp.maximum(jnp.sum(p, axis=-1, keepdims=True), 1e-30)
        p_sum = p_sum + p
        outs.append(jnp.dot(p.astype(BF), vc, preferred_element_type=jnp.float32))
    for c in range(NSA_GROUP_HEADS // 2):
        o_ref[:, c * V7X_LANES:(c + 1) * V7X_LANES] = jnp.where(lane < HEAD_DIM, outs[2 * c], outs[2 * c + 1])
    ps_hi, ps_lo = _split_bf16(p_sum)
    imp = _dot_nt(ov_ref[...], ps_hi) + _dot_nt(ov_ref[...], ps_lo)
    j = lax.broadcasted_iota(jnp.int32, (n_blk, tq), 0)
    q_blk = (i * tq + lax.broadcasted_iota(jnp.int32, (n_blk, tq), 1)) // NSA_SEL_LEN
    forced = (j == 0) | (j == q_blk) | (j == q_blk - 1)
    imp = jnp.where(forced, jnp.inf, jnp.where(j > q_blk, -jnp.inf, imp))
    rank = jnp.zeros((n_blk, tq), jnp.float32)
    for jj in range(n_blk):
        other = imp[jj:jj + 1, :]
        beats = (other > imp) | ((other == imp) & (jj < j))
        rank = rank + jnp.where(beats, 1.0, 0.0)
    m_ref[...] = jnp.where(rank < float(min(NSA_N_SELECT, n_blk)), 1.0, 0.0).astype(m_ref.dtype)


def nsa_cmp_attention_pallas(u, q_col, kc, vc):
    bsz, seq, _ = u.shape
    n_c = kc.shape[2]
    n_blk = seq // NSA_SEL_LEN
    gw = NSA_GROUP_HEADS * HEAD_DIM
    qb = q_col // gw
    cspec = pl.BlockSpec((None, None, n_c, V7X_LANES), lambda b, g, i: (b, g, 0, 0))
    return pl.pallas_call(
        _cmp_attn_kernel,
        out_shape=(jax.ShapeDtypeStruct((bsz, seq, NSA_HEADS * HEAD_DIM), jnp.float32),
                   jax.ShapeDtypeStruct((bsz, NSA_KV_GROUPS, n_blk, seq), BF)),
        grid=(bsz, NSA_KV_GROUPS, seq // CMP_TQ),
        in_specs=[pl.BlockSpec((None, CMP_TQ, gw), lambda b, g, i: (b, i, qb + g)),
                  cspec, cspec,
                  pl.BlockSpec((n_blk, n_c), lambda b, g, i: (0, 0))],
        out_specs=(pl.BlockSpec((None, CMP_TQ, gw), lambda b, g, i: (b, i, g)),
                   pl.BlockSpec((None, None, n_blk, CMP_TQ), lambda b, g, i: (b, g, 0, i))),
        compiler_params=_cparams(3),
        name="nsa_cmp_attention",
    )(u, kc, vc, jnp.asarray(_overlap_matrix(n_c, n_blk).T, BF))


MERGE_TM = 256


def _layer_norm_rows(x, g, b):
    mu = jnp.mean(x, axis=-1, keepdims=True)
    xc = x - mu
    var = jnp.mean(xc * xc, axis=-1, keepdims=True)
    return xc * lax.rsqrt(var + LN_EPS) * g + b


def _gate_expand_matrices():
    e = np.zeros((NSA_BRANCHES, V7X_LANES, NSA_HEADS * HEAD_DIM), np.float32)
    for br in range(NSA_BRANCHES):
        for h in range(NSA_HEADS):
            e[br, GLA_RANK + NSA_BRANCHES * h + br, h * HEAD_DIM:(h + 1) * HEAD_DIM] = 1.0
    return e


def _sigmoid(x):
    return 1.0 / (1.0 + jnp.exp(-x))


def _merge_kernel(x_ref, o0_ref, o1_ref, o2_ref, l0_ref, l1_ref, l2_ref, yb_ref, oc_ref, os_ref, ow_ref,
                  us_ref, mg_ref, ex_ref, wa_ref, wb_ref, wc_ref, wo_ref, g_ref, b_ref, out_ref):
    l0, l1, l2 = l0_ref[...], l1_ref[...], l2_ref[...]
    lm = jnp.maximum(jnp.maximum(l0, l1), l2)
    e0, e1, e2 = jnp.exp(l0 - lm), jnp.exp(l1 - lm), jnp.exp(l2 - lm)
    y_a = (e0 * o0_ref[...] + e1 * o1_ref[...] + e2 * o2_ref[...]) / (e0 + e1 + e2)
    sg = _sigmoid(us_ref[...])
    y_c = (_dot_split_lhs(sg, ex_ref[0]) * oc_ref[...] + _dot_split_lhs(sg, ex_ref[1]) * os_ref[...]
           + _dot_split_lhs(sg, ex_ref[2]) * ow_ref[...])
    d = D_MODEL
    f = functools.partial(jnp.dot, preferred_element_type=jnp.float32)
    merged = (_sigmoid(mg_ref[:, :d]) * f(y_a.astype(BF), wa_ref[...])
              + _sigmoid(mg_ref[:, d:2 * d]) * f(yb_ref[...].astype(BF), wb_ref[...])
              + _sigmoid(mg_ref[:, 2 * d:]) * f(y_c.astype(BF), wc_ref[...]))
    mix = f(merged.astype(BF), wo_ref[...])
    out_ref[...] = _layer_norm_rows(DEEPNORM_ALPHA * x_ref[...] + mix, g_ref[...], b_ref[...])


def mixer_merge_pallas(x, dil_o, dil_lse, y_b, o_cmp, o_sel, o_win, u_small, m_g, w_br_a, w_br_b, w_br_c, w_o_mix, ln_g, ln_b):
    n, d = x.shape
    tm = MERGE_TM
    aw = DIL_HEADS * HEAD_DIM

    def tok(width):
        return pl.BlockSpec((tm, width), lambda i: (i, 0))

    def whole(shape):
        return pl.BlockSpec(shape, lambda i: (0,) * len(shape))

    ex = jnp.asarray(_gate_expand_matrices(), BF)
    return pl.pallas_call(
        _merge_kernel,
        out_shape=jax.ShapeDtypeStruct((n, d), jnp.float32),
        grid=(n // tm,),
        in_specs=[tok(d)] + [tok(aw)] * 6 + [tok(B_V)] + [tok(C_Q)] * 3 + [tok(V7X_LANES), tok(N_BRANCHES * d),
                  whole(ex.shape), whole((aw, d)), whole((B_V, d)), whole((C_Q, d)), whole((d, d)),
                  whole((1, d)), whole((1, d))],
        out_specs=tok(d),
        compiler_params=_cparams(1),
        name="mixer_merge",
    )(x, *dil_o, *dil_lse, y_b, o_cmp, o_sel, o_win, u_small, m_g, ex,
      w_br_a.astype(BF), w_br_b.astype(BF), w_br_c.astype(BF), w_o_mix.astype(BF),
      ln_g.reshape(1, d), ln_b.reshape(1, d))


XATTN_TM = 256


def _xattn_kernel(x_ref, k_ref, v_ref, wq_ref, wo_ref, g_ref, b_ref, o_ref):
    x = x_ref[...]
    d = x.shape[-1]
    hd = d // XATTN_HEADS
    q = jnp.dot(x.astype(BF), wq_ref[...], preferred_element_type=jnp.float32)
    outs = []
    for h in range(XATTN_HEADS):
        sl = slice(h * hd, (h + 1) * hd)
        s = _dot_nt((q[:, sl] * (hd ** -0.5)).astype(BF), k_ref[:, sl].astype(BF))
        p = jnp.exp(s - jnp.max(s, axis=-1, keepdims=True))
        den = jnp.sum(p, axis=-1, keepdims=True)
        outs.append(jnp.dot(p.astype(BF), v_ref[:, sl].astype(BF), preferred_element_type=jnp.float32) / den)
    o = jnp.concatenate(outs, axis=1)
    xa = jnp.dot(o.astype(BF), wo_ref[...], preferred_element_type=jnp.float32)
    o_ref[...] = _layer_norm_rows(DEEPNORM_ALPHA * x + xa, g_ref[...], b_ref[...])


def cross_attention_block_pallas(x, mem, w_xq, w_xk, w_xv, w_xo, ln_g, ln_b):
    bsz, seq, d = x.shape
    mem_len = mem.shape[1]
    mem2 = mem.reshape(-1, d)
    k = _mm(mem2, w_xk)
    v = _mm(mem2, w_xv)
    n_t = seq // XATTN_TM

    def whole(shape):
        return pl.BlockSpec(shape, lambda b, i: (0,) * len(shape))

    out = pl.pallas_call(
        _xattn_kernel,
        out_shape=jax.ShapeDtypeStruct((bsz * seq, d), jnp.float32),
        grid=(bsz, n_t),
        in_specs=[pl.BlockSpec((XATTN_TM, d), lambda b, i: (b * n_t + i, 0)),
                  pl.BlockSpec((mem_len, d), lambda b, i: (b, 0)),
                  pl.BlockSpec((mem_len, d), lambda b, i: (b, 0)),
                  whole((d, d)), whole((d, d)), whole((1, d)), whole((1, d))],
        out_specs=pl.BlockSpec((XATTN_TM, d), lambda b, i: (b * n_t + i, 0)),
        compiler_params=_cparams(2),
        name="cross_attention_block",
    )(x.reshape(-1, d), k, v, w_xq.astype(BF), w_xo.astype(BF), ln_g.reshape(1, d), ln_b.reshape(1, d))
    return out.reshape(bsz, seq, d)


def _ln_kernel(x_ref, g_ref, b_ref, o_ref, obf_ref):
    y = _layer_norm_rows(x_ref[...], g_ref[...], b_ref[...])
    o_ref[...] = y
    obf_ref[...] = y.astype(BF)


def layer_norm_pallas(x, g, b):
    n, d = x.shape
    tm = _pick_tile(n, (1024, 512, 256, 128, 64, 32, 16, 8))
    spec = pl.BlockSpec((tm, d), lambda i: (i, 0))
    return pl.pallas_call(
        _ln_kernel,
        out_shape=(jax.ShapeDtypeStruct((n, d), jnp.float32), jax.ShapeDtypeStruct((n, d), BF)),
        grid=(n // tm,),
        in_specs=[spec, pl.BlockSpec((1, d), lambda i: (0, 0)), pl.BlockSpec((1, d), lambda i: (0, 0))],
        out_specs=(spec, spec),
        compiler_params=_cparams(1),
        name="layer_norm",
    )(x, g.reshape(1, d), b.reshape(1, d))


MOE_TB = 256


def _moe_ffn_kernel(layer, be_ref, nxt_ref, nused_ref, x_ref, wgu_hbm, bgu_ref, wd_hbm, bd_ref, o_ref,
                    wgu_st, wd_st, wgu_bf, wd_bf, sem):
    i = pl.program_id(0)
    e = be_ref[i]
    changed = jnp.logical_or(i == 0, e != be_ref[jnp.maximum(i - 1, 0)])

    def fetch(expert):
        return (pltpu.make_async_copy(wgu_hbm.at[layer, expert], wgu_st, sem.at[0]),
                pltpu.make_async_copy(wd_hbm.at[layer, expert], wd_st, sem.at[1]))

    @pl.when(i == 0)
    def _():
        for cp in fetch(e):
            cp.start()

    @pl.when(changed)
    def _():
        for cp in fetch(e):
            cp.wait()
        wgu_bf[...] = wgu_st[...].astype(BF)
        wd_bf[...] = wd_st[...].astype(BF)

        @pl.when(nxt_ref[i] != e)
        def _():
            for cp in fetch(nxt_ref[i]):
                cp.start()

    @pl.when(i < nused_ref[0])
    def _():
        gu = jnp.dot(x_ref[...].astype(BF), wgu_bf[...], preferred_element_type=jnp.float32) + bgu_ref[...]
        gate = jnp.minimum(gu[:, :D_FF], SWIGLU_LIMIT)
        up = jnp.clip(gu[:, D_FF:], -SWIGLU_LIMIT, SWIGLU_LIMIT)
        act = (up + 1.0) * gate * (1.0 / (1.0 + jnp.exp(-SWIGLU_ALPHA * gate)))
        o_ref[...] = jnp.dot(act.astype(BF), wd_bf[...], preferred_element_type=jnp.float32) + bd_ref[...]

    @pl.when(i >= nused_ref[0])
    def _():
        o_ref[...] = jnp.zeros(o_ref.shape, o_ref.dtype)


def moe_expert_ffn_pallas(buf, block_e, next_e, n_used, layer, w_gu, b_gu, w_down, b_down):
    n_rows, d = buf.shape
    depth, n_e = w_gu.shape[:2]
    grid_spec = pltpu.PrefetchScalarGridSpec(
        num_scalar_prefetch=3,
        grid=(n_rows // MOE_TB,),
        in_specs=[
            pl.BlockSpec((MOE_TB, d), lambda i, be, nx, nu: (i, 0)),
            pl.BlockSpec(memory_space=pl.ANY),
            pl.BlockSpec((None, None, 1, 2 * D_FF), lambda i, be, nx, nu: (layer, be[i], 0, 0)),
            pl.BlockSpec(memory_space=pl.ANY),
            pl.BlockSpec((None, None, 1, d), lambda i, be, nx, nu: (layer, be[i], 0, 0)),
        ],
        out_specs=pl.BlockSpec((MOE_TB, d), lambda i, be, nx, nu: (i, 0)),
        scratch_shapes=[pltpu.VMEM((d, 2 * D_FF), jnp.float32), pltpu.VMEM((D_FF, d), jnp.float32),
                        pltpu.VMEM((d, 2 * D_FF), BF), pltpu.VMEM((D_FF, d), BF),
                        pltpu.SemaphoreType.DMA((2,))],
    )
    return pl.pallas_call(
        functools.partial(_moe_ffn_kernel, layer),
        out_shape=jax.ShapeDtypeStruct((n_rows, d), jnp.float32),
        grid_spec=grid_spec,
        compiler_params=_cparams(1),
        name="moe_expert_ffn",
    )(block_e, next_e, n_used, buf, w_gu, b_gu.reshape(depth, n_e, 1, -1), w_down, b_down.reshape(depth, n_e, 1, -1))


MOE_ROUTE_TM = 256


def _router_kernel(x_ref, wr_ref, br_ref, route_ref, counts_ref, carry_ref):
    tm = MOE_ROUTE_TM

    @pl.when(pl.program_id(0) == 0)
    def _():
        carry_ref[...] = jnp.zeros(carry_ref.shape, jnp.float32)

    logits = _dot_split(x_ref[...], wr_ref[...]) + br_ref[...]
    lane = lax.broadcasted_iota(jnp.int32, (tm, N_EXPERTS), 1).astype(jnp.float32)
    onehots, vals, idxs = [], [], []
    rest = logits
    for _ in range(TOP_K):
        m = jnp.max(rest, axis=-1, keepdims=True)
        idx = jnp.min(jnp.where(rest == m, lane, float(N_EXPERTS)), axis=-1, keepdims=True)
        hit = lane == idx
        onehots.append(hit)
        vals.append(m)
        idxs.append(idx)
        rest = jnp.where(hit, -jnp.inf, rest)
    exps = [jnp.exp(v - vals[0]) for v in vals]
    den = exps[0] + exps[1] + exps[2] + exps[3]
    cnt = jnp.zeros((tm, N_EXPERTS), jnp.float32)
    for hit in onehots:
        cnt = cnt + jnp.where(hit, 1.0, 0.0)
    earlier = (lax.broadcasted_iota(jnp.int32, (tm, tm), 1) < lax.broadcasted_iota(jnp.int32, (tm, tm), 0))
    before = jnp.dot(earlier.astype(BF), cnt.astype(BF), preferred_element_type=jnp.float32) + carry_ref[...]
    lane_out = lax.broadcasted_iota(jnp.int32, (tm, V7X_LANES), 1)
    out = jnp.zeros((tm, V7X_LANES), jnp.float32)
    for k in range(TOP_K):
        pos = jnp.sum(jnp.where(onehots[k], before, 0.0), axis=-1, keepdims=True)
        out = jnp.where(lane_out == k, idxs[k], out)
        out = jnp.where(lane_out == TOP_K + k, exps[k] / den, out)
        out = jnp.where(lane_out == 2 * TOP_K + k, pos, out)
    route_ref[...] = out
    carry_ref[...] = carry_ref[...] + jnp.sum(cnt, axis=0, keepdims=True)
    counts_ref[...] = carry_ref[...]


def moe_router_pallas(x, w_router, b_router):
    n, d = x.shape
    tm = MOE_ROUTE_TM
    return pl.pallas_call(
        _router_kernel,
        out_shape=(jax.ShapeDtypeStruct((n, V7X_LANES), jnp.float32),
                   jax.ShapeDtypeStruct((1, N_EXPERTS), jnp.float32)),
        grid=(n // tm,),
        in_specs=[pl.BlockSpec((tm, d), lambda i: (i, 0)),
                  pl.BlockSpec((d, N_EXPERTS), lambda i: (0, 0)),
                  pl.BlockSpec((1, N_EXPERTS), lambda i: (0, 0))],
        out_specs=(pl.BlockSpec((tm, V7X_LANES), lambda i: (i, 0)),
                   pl.BlockSpec((1, N_EXPERTS), lambda i: (0, 0))),
        scratch_shapes=[pltpu.VMEM((1, N_EXPERTS), jnp.float32)],
        compiler_params=_cparams(1),
        name="moe_router",
    )(x, w_router, b_router.reshape(1, -1))


def _row_copy(src, dst, sem):
    return pltpu.make_async_copy(src, dst, sem)


def _dispatch_kernel(zrow_ref, dest_ref, x_ref, buf_ref, zero_ref, sem, zsem):
    tm = MOE_ROUTE_TM

    @pl.when(pl.program_id(0) == 0)
    def _():
        zero_ref[...] = jnp.zeros(zero_ref.shape, jnp.float32)
        for j in range(zrow_ref.shape[0]):
            @pl.when(zrow_ref[j] >= 0)
            def _():
                row0 = pl.multiple_of(zrow_ref[j], MOE_TB)
                pltpu.make_async_copy(zero_ref, buf_ref.at[pl.ds(row0, MOE_TB)], zsem).start()
        for j in range(zrow_ref.shape[0]):
            @pl.when(zrow_ref[j] >= 0)
            def _():
                pltpu.make_async_copy(zero_ref, buf_ref.at[pl.ds(0, MOE_TB)], zsem).wait()

    def body(t, carry):
        for k in range(TOP_K):
            row = dest_ref[0, TOP_K * t + k]
            _row_copy(x_ref.at[pl.ds(t, 1)], buf_ref.at[pl.ds(row, 1)], sem).start(priority=k % 2)
        return carry

    lax.fori_loop(0, tm, body, 0)
    for _ in range(TOP_K):
        _row_copy(x_ref, buf_ref.at[pl.ds(0, tm)], sem).wait()


def moe_dispatch_pallas(x, dest, zero_rows, n_rows):
    n, d = x.shape
    tm = MOE_ROUTE_TM
    grid_spec = pltpu.PrefetchScalarGridSpec(
        num_scalar_prefetch=1,
        grid=(n // tm,),
        in_specs=[pl.BlockSpec((None, 1, TOP_K * tm), lambda i, z: (i, 0, 0), memory_space=pltpu.SMEM),
                  pl.BlockSpec((tm, d), lambda i, z: (i, 0))],
        out_specs=pl.BlockSpec(memory_space=pl.ANY),
        scratch_shapes=[pltpu.VMEM((MOE_TB, d), jnp.float32), pltpu.SemaphoreType.DMA, pltpu.SemaphoreType.DMA],
    )
    return pl.pallas_call(
        _dispatch_kernel,
        out_shape=jax.ShapeDtypeStruct((n_rows, d), jnp.float32),
        grid_spec=grid_spec,
        compiler_params=_cparams(1),
        name="moe_dispatch",
    )(zero_rows, dest, x)


def _combine_kernel(dest_ref, x_ref, route_ref, obuf_ref, g_ref, b_ref, o_ref, obf_ref, rows_ref, sem):
    tm = MOE_ROUTE_TM

    def body(t, carry):
        for k in range(TOP_K):
            row = dest_ref[0, TOP_K * t + k]
            _row_copy(obuf_ref.at[pl.ds(row, 1)], rows_ref.at[k, pl.ds(t, 1)], sem).start(priority=k % 2)
        return carry

    lax.fori_loop(0, tm, body, 0)
    for k in range(TOP_K):
        _row_copy(obuf_ref.at[pl.ds(0, tm)], rows_ref.at[k], sem).wait()
    ff = jnp.zeros(x_ref.shape, jnp.float32)
    for k in range(TOP_K):
        ff = ff + route_ref[:, TOP_K + k:TOP_K + k + 1] * rows_ref[k]
    y = _layer_norm_rows(DEEPNORM_ALPHA * x_ref[...] + ff, g_ref[...], b_ref[...])
    o_ref[...] = y
    obf_ref[...] = y.astype(BF)


def moe_combine_pallas(x, route, dest, out_buf, ln_g, ln_b):
    n, d = x.shape
    tm = MOE_ROUTE_TM
    return pl.pallas_call(
        _combine_kernel,
        out_shape=(jax.ShapeDtypeStruct((n, d), jnp.float32), jax.ShapeDtypeStruct((n, d), BF)),
        grid=(n // tm,),
        in_specs=[pl.BlockSpec((None, 1, TOP_K * tm), lambda i: (i, 0, 0), memory_space=pltpu.SMEM),
                  pl.BlockSpec((tm, d), lambda i: (i, 0)),
                  pl.BlockSpec((tm, V7X_LANES), lambda i: (i, 0)),
                  pl.BlockSpec(memory_space=pl.ANY),
                  pl.BlockSpec((1, d), lambda i: (0, 0)), pl.BlockSpec((1, d), lambda i: (0, 0))],
        out_specs=(pl.BlockSpec((tm, d), lambda i: (i, 0)), pl.BlockSpec((tm, d), lambda i: (i, 0))),
        scratch_shapes=[pltpu.VMEM((TOP_K, tm, d), jnp.float32), pltpu.SemaphoreType.DMA],
        compiler_params=_cparams(1),
        name="moe_combine",
    )(dest, x, route, out_buf, ln_g.reshape(1, d), ln_b.reshape(1, d))


def token_mixer_block(h, h_bf, w_in, b_in, w_alpha2, b_alpha, gla_norm_g, cmp_pe_k, cmp_w1_k, cmp_w2_k,
                      cmp_pe_v, cmp_w1_v, cmp_w2_v, w_br_a, w_br_b, w_br_c, w_o_mix, ln_g, ln_b):
    bsz, seq, d = h.shape
    offs = np.concatenate([[0], np.cumsum(IN_WIDTHS)]).tolist()
    h2 = h.reshape(-1, d)

    def proj(lo, hi, out_dtype=jnp.float32):
        return _mm(h_bf, w_in[:, offs[lo]:offs[hi]], b_in[offs[lo]:offs[hi]], out_dtype).reshape(bsz, seq, -1)

    u_dil = proj(0, 3)
    u_gla = proj(3, 7)
    w_small = jnp.concatenate([w_in[:, offs[7]:offs[8]], w_in[:, offs[15]:offs[16]]], axis=1)
    b_small = jnp.concatenate([b_in[offs[7]:offs[8]], b_in[offs[15]:offs[16]]])
    n_small = w_small.shape[1]
    w_small = jnp.pad(w_small, ((0, 0), (0, V7X_LANES - n_small)))
    b_small = jnp.pad(b_small, (0, V7X_LANES - n_small))
    u_small = _mm(h_bf, w_small, b_small)
    u_c = proj(8, 15, BF)
    m_g = _mm(h_bf, w_in[:, offs[16]:offs[17]], b_in[offs[16]:offs[17]])
    dil_o, dil_lse = dilated_attention_pallas(u_dil, 0, A_W, 2 * A_W)
    y_b = gla_mixer_pallas(u_gla, u_small.reshape(bsz, seq, V7X_LANES), w_alpha2, b_alpha, gla_norm_g)
    kc, vc = nsa_compress_pallas(u_c, C_Q, C_Q + C_KV, cmp_pe_k, cmp_w1_k, cmp_w2_k, cmp_pe_v, cmp_w1_v, cmp_w2_v)
    o_cmp, selmask = nsa_cmp_attention_pallas(u_c, 0, kc, vc)
    o_sel = nsa_gqa_attention_pallas(u_c, 0, C_Q + 2 * C_KV, C_Q + 3 * C_KV, selmask)
    o_win = nsa_gqa_attention_pallas(u_c, 0, C_Q + 4 * C_KV, C_Q + 5 * C_KV)

    def flat(t):
        return t.reshape(bsz * seq, -1)

    out = mixer_merge_pallas(h2, [flat(t) for t in dil_o], [flat(t) for t in dil_lse], flat(y_b), flat(o_cmp),
                             flat(o_sel), flat(o_win), u_small, m_g, w_br_a, w_br_b, w_br_c, w_o_mix, ln_g, ln_b)
    return out.reshape(bsz, seq, d)


def moe_block(x, layer, w_router, b_router, w_gu, b_gu, w_down, b_down, ln_g, ln_b):
    n, d = x.shape
    route, counts = moe_router_pallas(x, w_router, b_router)
    ids = jnp.arange(N_EXPERTS, dtype=jnp.int32)
    expert = route[:, :TOP_K].astype(jnp.int32)
    pos = route[:, 2 * TOP_K:3 * TOP_K].astype(jnp.int32)
    counts = counts.reshape(-1).astype(jnp.int32)
    padded = (counts + MOE_TB - 1) // MOE_TB * MOE_TB
    pend = jnp.cumsum(padded)
    pstart = pend - padded
    dest = jnp.sum(jnp.where(expert[:, :, None] == ids, pstart, 0), axis=-1) + pos
    dest = dest.reshape(n // MOE_ROUTE_TM, 1, TOP_K * MOE_ROUTE_TM)
    n_rows = n * TOP_K + N_EXPERTS * MOE_TB
    n_blocks = n_rows // MOE_TB
    n_used = pend[-1:] // MOE_TB
    blk = jnp.minimum(jnp.arange(n_blocks, dtype=jnp.int32), n_used - 1) * MOE_TB
    block_e = jnp.minimum(jnp.sum((pend[None, :] <= blk[:, None]).astype(jnp.int32), axis=1), N_EXPERTS - 1)
    later_nonempty = (ids[None, :] > ids[:, None]) & (counts[None, :] > 0)
    next_of = jnp.min(jnp.where(later_nonempty, ids[None, :], N_EXPERTS), axis=1)
    next_of = jnp.where(next_of < N_EXPERTS, next_of, ids)
    next_e = jnp.sum(jnp.where(block_e[:, None] == ids, next_of, 0), axis=1)
    tail = pend[-1] + ids * MOE_TB
    zero_rows = jnp.concatenate([jnp.where(counts > 0, pend - MOE_TB, -1), jnp.where(tail < n_rows, tail, -1)])
    buf = moe_dispatch_pallas(x, dest, zero_rows, n_rows)
    out = moe_expert_ffn_pallas(buf, block_e, next_e, n_used, layer, w_gu, b_gu, w_down, b_down)
    return moe_combine_pallas(x, route, dest, out, ln_g, ln_b)


def kernel(x, mem, ln0_g, ln0_b, w_in, b_in, w_alpha2, b_alpha, gla_norm_g, cmp_pe_k, cmp_w1_k, cmp_w2_k, cmp_pe_v, cmp_w1_v, cmp_w2_v, w_br_a, w_br_b, w_br_c, w_o_mix, ln1_g, ln1_b, w_xq, w_xk, w_xv, w_xo, ln2_g, ln2_b, w_router, b_router, w_gu, b_gu, w_down, b_down, ln3_g, ln3_b):
    shape = x.shape
    x, x_bf = layer_norm_pallas(x.reshape(-1, shape[-1]), ln0_g, ln0_b)
    for li in range(DEPTH):
        x = token_mixer_block(x.reshape(shape), x_bf, w_in[li], b_in[li], w_alpha2[li], b_alpha[li], gla_norm_g[li],
                              cmp_pe_k[li], cmp_w1_k[li], cmp_w2_k[li], cmp_pe_v[li], cmp_w1_v[li], cmp_w2_v[li],
                              w_br_a[li], w_br_b[li], w_br_c[li], w_o_mix[li], ln1_g[li], ln1_b[li])
        x = cross_attention_block_pallas(x, mem, w_xq[li], w_xk[li], w_xv[li], w_xo[li], ln2_g[li], ln2_b[li])
        x, x_bf = moe_block(x.reshape(-1, shape[-1]), li, w_router[li], b_router[li], w_gu, b_gu, w_down, b_down,
                            ln3_g[li], ln3_b[li])
    return x.reshape(shape)
```

```python
import functools

import jax
import jax.numpy as jnp
import numpy as np
from jax import lax
from jax.experimental import pallas as pl
from jax.experimental.pallas import tpu as pltpu

D_MODEL = 1024
DEPTH = 2
HEAD_DIM = 64
N_BRANCHES = 3
LN_EPS = 1e-5

DIL_GROUPS = ((128, 1), (512, 4), (2048, 16))
DIL_HEADS = 4

GLA_HEADS = 4
GLA_DK = D_MODEL // 2 // GLA_HEADS
GLA_DV = D_MODEL // GLA_HEADS
GLA_RANK = 16
GLA_TAU = 16.0
GLA_CHUNK = 64

NSA_HEADS = 16
NSA_KV_GROUPS = 4
NSA_GROUP_HEADS = NSA_HEADS // NSA_KV_GROUPS
NSA_BRANCHES = 3
NSA_CMP_LEN = 32
NSA_CMP_STRIDE = 16
NSA_CMP_HIDDEN = 2 * HEAD_DIM
NSA_SEL_LEN = 64
NSA_N_SELECT = 16
NSA_WINDOW = 512

XATTN_HEADS = 4

N_EXPERTS = 32
TOP_K = 4
D_FF = D_MODEL
SWIGLU_LIMIT = 7.0
SWIGLU_ALPHA = 1.702

DEEPNORM_ALPHA = (2 * DEPTH) ** 0.25

A_W = len(DIL_GROUPS) * DIL_HEADS * HEAD_DIM
B_K = GLA_HEADS * GLA_DK
B_V = GLA_HEADS * GLA_DV
C_Q = NSA_HEADS * HEAD_DIM
C_KV = NSA_KV_GROUPS * HEAD_DIM
IN_WIDTHS = (A_W, A_W, A_W,
             B_K, B_K, B_V, B_V, GLA_RANK,
             C_Q, C_KV, C_KV, C_KV, C_KV, C_KV, C_KV, NSA_HEADS * NSA_BRANCHES,
             N_BRANCHES * D_MODEL)

V7X_VMEM_LIMIT_BYTES = 48 * 1024 * 1024
V7X_LANES = 128
NEG_BIG = -1e30
BF = jnp.bfloat16


def _cparams(n_axes):
    return pltpu.CompilerParams(dimension_semantics=("arbitrary",) * n_axes,
                                vmem_limit_bytes=V7X_VMEM_LIMIT_BYTES)


def _split_bf16(x):
    hi = x.astype(BF)
    lo = (x - hi.astype(jnp.float32)).astype(BF)
    return hi, lo


def _dot_split(a, b):
    a_hi, a_lo = _split_bf16(a)
    b_hi, b_lo = _split_bf16(b)
    f = functools.partial(jnp.dot, preferred_element_type=jnp.float32)
    return f(a_hi, b_hi) + (f(a_hi, b_lo) + f(a_lo, b_hi))


def _dot_split_lhs(a, b_exact):
    a_hi, a_lo = _split_bf16(a)
    f = functools.partial(jnp.dot, preferred_element_type=jnp.float32)
    return f(a_hi, b_exact) + f(a_lo, b_exact)


def _dot_nt(a, b):
    return lax.dot_general(a, b, (((1,), (1,)), ((), ())), preferred_element_type=jnp.float32)


def _mm_kernel(x_ref, w_ref, b_ref, o_ref):
    acc = jnp.dot(x_ref[...].astype(BF), w_ref[...].astype(BF), preferred_element_type=jnp.float32)
    o_ref[...] = (acc + b_ref[...]).astype(o_ref.dtype)


def _pick_tile(n, candidates):
    for c in candidates:
        if n % c == 0:
            return c
    return n


def _mm(x, w, b=None, out_dtype=jnp.float32):
    m, k = x.shape
    n = w.shape[1]
    if b is None:
        b = jnp.zeros((n,), jnp.float32)
    tm = _pick_tile(m, (1024, 512, 256, 128, 64, 32, 16, 8))
    tn = _pick_tile(n, (1536, 1280, 1152, 1024, 768, 512, 384, 256, 128))
    return pl.pallas_call(
        _mm_kernel,
        out_shape=jax.ShapeDtypeStruct((m, n), out_dtype),
        grid=(n // tn, m // tm),
        in_specs=[
            pl.BlockSpec((tm, k), lambda j, i: (i, 0)),
            pl.BlockSpec((k, tn), lambda j, i: (0, j)),
            pl.BlockSpec((1, tn), lambda j, i: (0, j)),
        ],
        out_specs=pl.BlockSpec((tm, tn), lambda j, i: (i, j)),
        compiler_params=_cparams(2),
        name="dense_proj",
    )(x, w, b.reshape(1, n))


def _mm3(x, w, b=None):
    lead = x.shape[:-1]
    return _mm(x.reshape(-1, x.shape[-1]), w, b).reshape(*lead, w.shape[1])


GQA_TQ = 256
GQA_TK_SELECT = 512
GQA_TK_BAND = 256


def _half_select(x, want_half, have_half):
    if want_half != have_half:
        x = pltpu.roll(x, HEAD_DIM, axis=1)
    return x


def _lane_half_mask(shape, half):
    lane = lax.broadcasted_iota(jnp.int32, shape, 1)
    return (lane >= half * HEAD_DIM) & (lane < (half + 1) * HEAD_DIM)


def _make_gqa_kernel(select, max_dist, n_outer):
    r, tq, tk = NSA_GROUP_HEADS, GQA_TQ, (GQA_TK_SELECT if select else GQA_TK_BAND)

    def kern(q_ref, k_ref, v_ref, *rest):
        if select:
            m_ref, e_ref, o_ref, mx_ref, acc_ref = rest
        else:
            o_ref, mx_ref, acc_ref = rest
        i = pl.program_id(n_outer)
        dist0 = i * tq + lax.broadcasted_iota(jnp.int32, (tk, tq), 1) - lax.broadcasted_iota(jnp.int32, (tk, tq), 0)
        qs = []
        for half in range(2):
            parts = []
            for rr in range(r):
                hh = half * r + rr
                qc = q_ref[:, (hh // 2) * V7X_LANES:(hh // 2 + 1) * V7X_LANES].astype(jnp.float32) * (HEAD_DIM ** -0.5)
                qc = _half_select(qc, half, hh % 2)
                parts.append(jnp.where(_lane_half_mask(qc.shape, half), qc, 0.0).astype(BF))
            qs.append(jnp.concatenate(parts, axis=0))
        mx_ref[...] = jnp.full(mx_ref.shape, NEG_BIG, jnp.float32)
        acc_ref[...] = jnp.zeros(acc_ref.shape, jnp.float32)
        v_lane = lax.broadcasted_iota(jnp.int32, (tk, V7X_LANES), 1)

        def body(j, carry):
            k0 = pl.multiple_of(j * tk, tk)
            kt = k_ref[pl.ds(k0, tk), :].astype(BF)
            vt = v_ref[pl.ds(k0, tk), :].astype(BF)
            dist = dist0 - k0
            in_band = (dist >= 0) if select else ((dist >= 0) & (dist <= max_dist))
            for half in range(2):
                if select:
                    blk = jnp.dot(e_ref[j], m_ref[half], preferred_element_type=jnp.float32)
                    valid = (blk > 0.5) & in_band
                else:
                    valid = in_band
                bias = jnp.where(valid, 0.0, NEG_BIG)
                s = _dot_nt(kt, qs[half]) + jnp.concatenate([bias] * r, axis=1)
                m_old = mx_ref[half]
                m_new = jnp.maximum(m_old, jnp.max(s, axis=0, keepdims=True))
                p = jnp.exp((s - m_new).astype(BF))
                v_aug = jnp.where((v_lane >= half * HEAD_DIM) & (v_lane < (half + 1) * HEAD_DIM), vt, 1.0)
                pv = lax.dot_general(v_aug, p, (((0,), (0,)), ((), ())), preferred_element_type=jnp.float32)
                acc_ref[half] = jnp.exp(m_old - m_new) * acc_ref[half] + pv
                mx_ref[half] = m_new
            return carry

        lo = 0 if select else jnp.maximum(i * tq - max_dist, 0) // tk
        lax.fori_loop(lo, (i * tq + tq - 1) // tk + 1, body, 0)
        outs = []
        for half in range(2):
            acc = acc_ref[half]
            den_row = (1 - half) * HEAD_DIM
            o = (acc / acc[den_row:den_row + 1, :]).T
            for rr in range(r):
                outs.append(_half_select(o[rr * tq:(rr + 1) * tq], (half * r + rr) % 2, half))
        lane = lax.broadcasted_iota(jnp.int32, (tq, V7X_LANES), 1)
        for c in range(r):
            both = jnp.where(lane < HEAD_DIM, outs[2 * c], outs[2 * c + 1])
            o_ref[:, c * V7X_LANES:(c + 1) * V7X_LANES] = both.astype(o_ref.dtype)

    return kern


def _sel_expand_matrix(seq):
    n_blk = seq // NSA_SEL_LEN
    e = (np.arange(seq)[:, None] // NSA_SEL_LEN == np.arange(n_blk)[None, :]).astype(np.float32)
    return e.reshape(seq // GQA_TK_SELECT, GQA_TK_SELECT, n_blk)


def nsa_gqa_attention_pallas(u, q_col, k_col, v_col, selmask=None):
    bsz, seq, _ = u.shape
    select = selmask is not None
    n_pairs = NSA_KV_GROUPS // 2
    qw = 2 * NSA_GROUP_HEADS * HEAD_DIM
    qb, kb, vb = q_col // qw, k_col // V7X_LANES, v_col // V7X_LANES
    rows = NSA_GROUP_HEADS * GQA_TQ
    in_specs = [
        pl.BlockSpec((None, GQA_TQ, qw), lambda b, gp, i: (b, i, qb + gp)),
        pl.BlockSpec((None, seq, V7X_LANES), lambda b, gp, i: (b, 0, kb + gp)),
        pl.BlockSpec((None, seq, V7X_LANES), lambda b, gp, i: (b, 0, vb + gp)),
    ]
    args = [u, u, u]
    if select:
        n_blk = selmask.shape[2]
        in_specs += [pl.BlockSpec((None, 2, n_blk, GQA_TQ), lambda b, gp, i: (b, gp, 0, i)),
                     pl.BlockSpec((seq // GQA_TK_SELECT, GQA_TK_SELECT, n_blk), lambda b, gp, i: (0, 0, 0))]
        args += [selmask, jnp.asarray(_sel_expand_matrix(seq), BF)]
    return pl.pallas_call(
        _make_gqa_kernel(select, NSA_WINDOW - 1, 2),
        out_shape=jax.ShapeDtypeStruct((bsz, seq, NSA_HEADS * HEAD_DIM), BF),
        grid=(bsz, n_pairs, seq // GQA_TQ),
        in_specs=in_specs,
        out_specs=pl.BlockSpec((None, GQA_TQ, qw), lambda b, gp, i: (b, i, gp)),
        scratch_shapes=[pltpu.VMEM((2, 1, rows), jnp.float32),
                        pltpu.VMEM((2, V7X_LANES, rows), jnp.float32)],
        compiler_params=_cparams(3),
        name="nsa_selected_attention" if select else "nsa_window_attention",
    )(*args)


DIL_TILE = 128


def _make_dil_kernel(dil, max_dist, seq):
    t = DIL_TILE
    n_sub = seq // dil // t
    n_kv = min(-(-max_dist // t) + 1, n_sub)

    def rows(first, size):
        return pl.ds(first, size, stride=dil) if dil > 1 else pl.ds(first, size)

    def kern(q_ref, k_ref, v_ref, o_ref, lse_ref):
        kw = n_kv * t
        lane = lax.broadcasted_iota(jnp.int32, (t, V7X_LANES), 1)
        for rr in range(dil):
            for i in range(n_sub):
                k_tile0 = max(i + 1 - n_kv, 0)
                dist = ((i - k_tile0) * t + lax.broadcasted_iota(jnp.int32, (t, kw), 0)
                        - lax.broadcasted_iota(jnp.int32, (t, kw), 1))
                bias = jnp.where((dist >= 0) & (dist <= max_dist), 0.0, NEG_BIG)
                q_rows = rows(rr + i * t * dil, t)
                k_rows = rows(rr + k_tile0 * t * dil, kw)
                qc = q_ref[q_rows, :] * (HEAD_DIM ** -0.5)
                kt = k_ref[k_rows, :].astype(BF)
                vt = v_ref[k_rows, :].astype(BF)
                os_, ls_ = [], []
                for half in range(2):
                    qh = jnp.where(_lane_half_mask(qc.shape, half), qc, 0.0).astype(BF)
                    s = _dot_nt(qh, kt) + bias
                    m = jnp.max(s, axis=-1, keepdims=True)
                    p = jnp.exp(s - m)
                    den = jnp.sum(p, axis=-1, keepdims=True)
                    os_.append(jnp.dot(p.astype(BF), vt, preferred_element_type=jnp.float32) / den)
                    ls_.append(jnp.broadcast_to(m + jnp.log(den), (t, V7X_LANES)))
                o_ref[q_rows, :] = jnp.where(lane < HEAD_DIM, os_[0], os_[1])
                lse_ref[q_rows, :] = jnp.where(lane < HEAD_DIM, ls_[0], ls_[1])

    return kern


def dilated_attention_pallas(u, q_col, k_col, v_col):
    bsz, seq, _ = u.shape
    gw = DIL_HEADS * HEAD_DIM
    outs, lses = [], []
    sds = jax.ShapeDtypeStruct((bsz, seq, gw), jnp.float32)
    out_spec = pl.BlockSpec((None, seq, V7X_LANES), lambda b, c: (b, 0, c))
    for gi, (window, dil) in enumerate(DIL_GROUPS):

        def cmap(col):
            blk = (col + gi * gw) // V7X_LANES
            return lambda b, c: (b, 0, blk + c)

        o, lse = pl.pallas_call(
            _make_dil_kernel(dil, window // dil, seq),
            out_shape=(sds, sds),
            grid=(bsz, gw // V7X_LANES),
            in_specs=[pl.BlockSpec((None, seq, V7X_LANES), cmap(c)) for c in (q_col, k_col, v_col)],
            out_specs=(out_spec, out_spec),
            compiler_params=_cparams(2),
            name="dilated_attention",
        )(u, u, u)
        outs.append(o)
        lses.append(lse)
    return outs, lses


GLA_SUB = 16


def _gla_kernel(q_ref, k_ref, v_ref, r_ref, lr_ref, wa_ref, ba_ref, g_ref, o_ref, state_ref):
    c, dk, dv, sub = GLA_CHUNK, GLA_DK, GLA_DV, GLA_SUB

    @pl.when(pl.program_id(1) == 0)
    def _():
        state_ref[...] = jnp.zeros(state_ref.shape, jnp.float32)

    x = _dot_split(lr_ref[...], wa_ref[...]) + ba_ref[...]
    log_a = (jnp.minimum(x, 0.0) - jnp.log1p(jnp.exp(-jnp.abs(x)))) * (1.0 / GLA_TAU)
    tri = (lax.broadcasted_iota(jnp.int32, (c, c), 0) >= lax.broadcasted_iota(jnp.int32, (c, c), 1))
    la_hi, la_lo = _split_bf16(log_a)
    tri = tri.astype(BF)
    b_all = (jnp.dot(tri, la_hi, preferred_element_type=jnp.float32)
             + jnp.dot(tri, la_lo, preferred_element_type=jnp.float32))
    col = lax.broadcasted_iota(jnp.int32, (sub, c), 1)
    row = lax.broadcasted_iota(jnp.int32, (sub, c), 0)
    for h in range(GLA_HEADS):
        q = q_ref[:, h * dk:(h + 1) * dk] * (dk ** -0.5)
        k = k_ref[:, h * dk:(h + 1) * dk]
        v = v_ref[:, h * dv:(h + 1) * dv]
        b = b_all[:, h * dk:(h + 1) * dk]
        a_rows = []
        for blk in range(c // sub):
            lo = blk * sub
            q_i, b_i, k_i = q[lo:lo + sub], b[lo:lo + sub], k[lo:lo + sub]
            if blk > 0:
                b_prev = b[lo - 1:lo]
                q_t = q_i * jnp.exp(b_i - b_prev)
                k_t = k * jnp.exp(jnp.minimum(b_prev - b, 0.0))
                acc = jnp.where(col < lo, _dot_nt(q_t.astype(BF), k_t.astype(BF)), 0.0)
            else:
                acc = jnp.zeros((sub, c), jnp.float32)
            for j in range(sub):
                t = q_i * k_i[j:j + 1] * jnp.exp(jnp.minimum(b_i - b_i[j:j + 1], 0.0))
                cs = jnp.sum(t, axis=-1, keepdims=True)
                acc = acc + jnp.where((col == lo + j) & (row >= j), cs, 0.0)
            a_rows.append(acc)
        attn = jnp.concatenate(a_rows, axis=0)
        st = state_ref[h]
        o = jnp.dot(attn.astype(BF), v.astype(BF), preferred_element_type=jnp.float32)
        o = o + _dot_nt((q * jnp.exp(b)).astype(BF), st.astype(BF))
        b_last = b[c - 1:c]
        k_d = k * jnp.exp(b_last - b)
        upd = lax.dot_general(v.astype(BF), k_d.astype(BF), (((0,), (0,)), ((), ())),
                              preferred_element_type=jnp.float32)
        state_ref[h] = st * jnp.exp(b_last) + upd
        o = o * lax.rsqrt(jnp.mean(o * o, axis=-1, keepdims=True) + LN_EPS)
        r = r_ref[:, h * dv:(h + 1) * dv]
        o_ref[:, h * dv:(h + 1) * dv] = o * g_ref[...] * (r / (1.0 + jnp.exp(-r)))


def gla_mixer_pallas(u_gla, u_small, w_alpha2, b_alpha, norm_g):
    bsz, seq, _ = u_gla.shape
    c = GLA_CHUNK

    def tok(width, blk):
        return pl.BlockSpec((None, c, width), lambda b, i: (b, i, blk))

    def whole(shape):
        return pl.BlockSpec(shape, lambda b, i: (0,) * len(shape))

    wa = jnp.pad(w_alpha2, ((0, V7X_LANES - GLA_RANK), (0, 0)))
    return pl.pallas_call(
        _gla_kernel,
        out_shape=jax.ShapeDtypeStruct((bsz, seq, B_V), jnp.float32),
        grid=(bsz, seq // c),
        in_specs=[tok(B_K, 0), tok(B_K, 1), tok(B_V, 1), tok(B_V, 2), tok(V7X_LANES, 0),
                  whole((V7X_LANES, B_K)), whole((1, B_K)), whole((1, GLA_DV))],
        out_specs=tok(B_V, 0),
        scratch_shapes=[pltpu.VMEM((GLA_HEADS, GLA_DV, GLA_DK), jnp.float32)],
        compiler_params=_cparams(2),
        name="gla_mixer",
    )(u_gla, u_gla, u_gla, u_gla, u_small, wa, b_alpha.reshape(1, -1), norm_g.reshape(1, -1))


CMP_HALF = NSA_CMP_STRIDE * HEAD_DIM


def _gelu_tanh(x):
    return 0.5 * x * (1.0 + jnp.tanh(np.sqrt(2.0 / np.pi) * (x + 0.044715 * (x * x * x))))


def _compress_kernel(hk_ref, hv_ref, pek_ref, w1k_ref, w2k_ref, pev_ref, w1v_ref, w2v_ref, kc_ref, vc_ref):
    for h_ref, pe_ref, w1_ref, w2_ref, o_ref in ((hk_ref, pek_ref, w1k_ref, w2k_ref, kc_ref),
                                                 (hv_ref, pev_ref, w1v_ref, w2v_ref, vc_ref)):
        w1a = w1_ref[0].astype(BF)
        w1b = w1_ref[1].astype(BF)
        pe = pe_ref[...].astype(BF)
        pe_term = (jnp.dot(pe[:, :CMP_HALF], w1a, preferred_element_type=jnp.float32)
                   + jnp.dot(pe[:, CMP_HALF:], w1b, preferred_element_type=jnp.float32))
        w2 = w2_ref[...].astype(BF)
        for g in range(NSA_KV_GROUPS):
            hb = h_ref[g].astype(BF)
            y1 = jnp.dot(hb, w1a, preferred_element_type=jnp.float32)
            y2 = jnp.dot(hb, w1b, preferred_element_type=jnp.float32)
            n_half = y2.shape[0]
            pre = y1 + pltpu.roll(y2, n_half - 1, axis=0) + pe_term
            out = jnp.dot(_gelu_tanh(pre).astype(BF), w2, preferred_element_type=jnp.float32)
            o_ref[g] = jnp.concatenate([out, out], axis=1)


def nsa_compress_pallas(u, k_col, v_col, pe_k, w1_k, w2_k, pe_v, w1_v, w2_v):
    bsz, seq, _ = u.shape
    n_half = seq // NSA_CMP_STRIDE

    def halves(col):
        t = u[..., col:col + C_KV].reshape(bsz, n_half, NSA_CMP_STRIDE, NSA_KV_GROUPS, HEAD_DIM)
        return t.transpose(0, 3, 1, 2, 4).reshape(bsz, NSA_KV_GROUPS, n_half, CMP_HALF)

    def whole(shape):
        return pl.BlockSpec(shape, lambda b: (0,) * len(shape))

    hspec = pl.BlockSpec((None, NSA_KV_GROUPS, n_half, CMP_HALF), lambda b: (b, 0, 0, 0))
    ospec = pl.BlockSpec((None, NSA_KV_GROUPS, n_half, V7X_LANES), lambda b: (b, 0, 0, 0))
    wspecs = [whole((1, 2 * CMP_HALF)), whole((2, CMP_HALF, NSA_CMP_HIDDEN)), whole((NSA_CMP_HIDDEN, HEAD_DIM))]
    sds = jax.ShapeDtypeStruct((bsz, NSA_KV_GROUPS, n_half, V7X_LANES), jnp.float32)
    return pl.pallas_call(
        _compress_kernel,
        out_shape=(sds, sds),
        grid=(bsz,),
        in_specs=[hspec, hspec] + wspecs + wspecs,
        out_specs=(ospec, ospec),
        compiler_params=_cparams(1),
        name="nsa_compress",
    )(halves(k_col), halves(v_col),
      pe_k.reshape(1, -1), w1_k.reshape(2, CMP_HALF, NSA_CMP_HIDDEN), w2_k,
      pe_v.reshape(1, -1), w1_v.reshape(2, CMP_HALF, NSA_CMP_HIDDEN), w2_v)


CMP_TQ = 1024


def _overlap_matrix(n_cmp_pad, n_blk):
    c0 = np.arange(n_cmp_pad)[:, None] * NSA_CMP_STRIDE
    s0 = np.arange(n_blk)[None, :] * NSA_SEL_LEN
    return ((c0 < s0 + NSA_SEL_LEN) & (c0 + NSA_CMP_LEN > s0)).astype(np.float32)


def _cmp_attn_kernel(q_ref, kc_ref, vc_ref, ov_ref, o_ref, m_ref):
    i = pl.program_id(2)
    tq = CMP_TQ
    n_c = kc_ref.shape[0]
    n_blk = ov_ref.shape[0]
    kc = kc_ref[...].astype(BF)
    vc = vc_ref[...].astype(BF)
    pos = i * tq + lax.broadcasted_iota(jnp.int32, (tq, n_c), 0)
    c_end = lax.broadcasted_iota(jnp.int32, (tq, n_c), 1) * NSA_CMP_STRIDE + (NSA_CMP_LEN - 1)
    cvalid = c_end <= pos
    lane = lax.broadcasted_iota(jnp.int32, (tq, V7X_LANES), 1)
    p_sum = jnp.zeros((tq, n_c), jnp.float32)
    outs = []
    for rr in range(NSA_GROUP_HEADS):
        qc = q_ref[:, (rr // 2) * V7X_LANES:(rr // 2 + 1) * V7X_LANES].astype(jnp.float32) * (HEAD_DIM ** -0.5)
        qh = jnp.where(_lane_half_mask(qc.shape, rr % 2), qc, 0.0).astype(BF)
        s = jnp.where(cvalid, _dot_nt(qh, kc), NEG_BIG)
        m = jnp.max(s, axis=-1, keepdims=True)
        p = jnp.where(cvalid, jnp.exp(s - m), 0.0)
        p = p / jnp.maximum(jnp.sum(p, axis=-1, keepdims=True), 1e-30)
        p_sum = p_sum + p
        outs.append(jnp.dot(p.astype(BF), vc, preferred_element_type=jnp.float32))
    for c in range(NSA_GROUP_HEADS // 2):
        both = jnp.where(lane < HEAD_DIM, outs[2 * c], outs[2 * c + 1])
        o_ref[:, c * V7X_LANES:(c + 1) * V7X_LANES] = both.astype(o_ref.dtype)
    ps_hi, ps_lo = _split_bf16(p_sum)
    imp = _dot_nt(ov_ref[...], ps_hi) + _dot_nt(ov_ref[...], ps_lo)
    j = lax.broadcasted_iota(jnp.int32, (n_blk, tq), 0)
    q_blk = (i * tq + lax.broadcasted_iota(jnp.int32, (n_blk, tq), 1)) // NSA_SEL_LEN
    forced = (j == 0) | (j == q_blk) | (j == q_blk - 1)
    imp = jnp.where(forced, jnp.inf, jnp.where(j > q_blk, -jnp.inf, imp))
    rank = jnp.zeros((n_blk, tq), jnp.float32)
    for jj in range(n_blk):
        other = imp[jj:jj + 1, :]
        beats = (other > imp) | ((other == imp) & (jj < j))
        rank = rank + jnp.where(beats, 1.0, 0.0)
    m_ref[...] = jnp.where(rank < float(min(NSA_N_SELECT, n_blk)), 1.0, 0.0).astype(m_ref.dtype)


def nsa_cmp_attention_pallas(u, q_col, kc, vc):
    bsz, seq, _ = u.shape
    n_c = kc.shape[2]
    n_blk = seq // NSA_SEL_LEN
    gw = NSA_GROUP_HEADS * HEAD_DIM
    qb = q_col // gw
    cspec = pl.BlockSpec((None, None, n_c, V7X_LANES), lambda b, g, i: (b, g, 0, 0))
    return pl.pallas_call(
        _cmp_attn_kernel,
        out_shape=(jax.ShapeDtypeStruct((bsz, seq, NSA_HEADS * HEAD_DIM), BF),
                   jax.ShapeDtypeStruct((bsz, NSA_KV_GROUPS, n_blk, seq), BF)),
        grid=(bsz, NSA_KV_GROUPS, seq // CMP_TQ),
        in_specs=[pl.BlockSpec((None, CMP_TQ, gw), lambda b, g, i: (b, i, qb + g)),
                  cspec, cspec,
                  pl.BlockSpec((n_blk, n_c), lambda b, g, i: (0, 0))],
        out_specs=(pl.BlockSpec((None, CMP_TQ, gw), lambda b, g, i: (b, i, g)),
                   pl.BlockSpec((None, None, n_blk, CMP_TQ), lambda b, g, i: (b, g, 0, i))),
        compiler_params=_cparams(3),
        name="nsa_cmp_attention",
    )(u, kc, vc, jnp.asarray(_overlap_matrix(n_c, n_blk).T, BF))


MERGE_TM = 256


def _layer_norm_rows(x, g, b):
    mu = jnp.mean(x, axis=-1, keepdims=True)
    xc = x - mu
    var = jnp.mean(xc * xc, axis=-1, keepdims=True)
    return xc * lax.rsqrt(var + LN_EPS) * g + b


def _gate_expand_matrices():
    e = np.zeros((NSA_BRANCHES, V7X_LANES, NSA_HEADS * HEAD_DIM), np.float32)
    for br in range(NSA_BRANCHES):
        for h in range(NSA_HEADS):
            e[br, GLA_RANK + NSA_BRANCHES * h + br, h * HEAD_DIM:(h + 1) * HEAD_DIM] = 1.0
    return e


def _sigmoid(x):
    return 1.0 / (1.0 + jnp.exp(-x))


def _merge_kernel(x_ref, o0_ref, o1_ref, o2_ref, l0_ref, l1_ref, l2_ref, yb_ref, oc_ref, os_ref, ow_ref,
                  us_ref, mg_ref, ex_ref, wa_ref, wb_ref, wc_ref, wo_ref, g_ref, b_ref, out_ref):
    l0, l1, l2 = l0_ref[...], l1_ref[...], l2_ref[...]
    lm = jnp.maximum(jnp.maximum(l0, l1), l2)
    e0, e1, e2 = jnp.exp(l0 - lm), jnp.exp(l1 - lm), jnp.exp(l2 - lm)
    y_a = (e0 * o0_ref[...] + e1 * o1_ref[...] + e2 * o2_ref[...]) / (e0 + e1 + e2)
    sg = _sigmoid(us_ref[...])
    f32 = jnp.float32
    y_c = (_dot_split_lhs(sg, ex_ref[0]) * oc_ref[...].astype(f32) + _dot_split_lhs(sg, ex_ref[1]) * os_ref[...].astype(f32)
           + _dot_split_lhs(sg, ex_ref[2]) * ow_ref[...].astype(f32))
    d = D_MODEL
    f = functools.partial(jnp.dot, preferred_element_type=jnp.float32)
    merged = (_sigmoid(mg_ref[:, :d].astype(f32)) * f(y_a.astype(BF), wa_ref[...])
              + _sigmoid(mg_ref[:, d:2 * d].astype(f32)) * f(yb_ref[...].astype(BF), wb_ref[...])
              + _sigmoid(mg_ref[:, 2 * d:].astype(f32)) * f(y_c.astype(BF), wc_ref[...]))
    mix = f(merged.astype(BF), wo_ref[...])
    out_ref[...] = _layer_norm_rows(DEEPNORM_ALPHA * x_ref[...] + mix, g_ref[...], b_ref[...])


def mixer_merge_pallas(x, dil_o, dil_lse, y_b, o_cmp, o_sel, o_win, u_small, m_g, w_br_a, w_br_b, w_br_c, w_o_mix, ln_g, ln_b):
    n, d = x.shape
    tm = MERGE_TM
    aw = DIL_HEADS * HEAD_DIM

    def tok(width):
        return pl.BlockSpec((tm, width), lambda i: (i, 0))

    def whole(shape):
        return pl.BlockSpec(shape, lambda i: (0,) * len(shape))

    ex = jnp.asarray(_gate_expand_matrices(), BF)
    return pl.pallas_call(
        _merge_kernel,
        out_shape=jax.ShapeDtypeStruct((n, d), jnp.float32),
        grid=(n // tm,),
        in_specs=[tok(d)] + [tok(aw)] * 6 + [tok(B_V)] + [tok(C_Q)] * 3 + [tok(V7X_LANES), tok(N_BRANCHES * d),
                  whole(ex.shape), whole((aw, d)), whole((B_V, d)), whole((C_Q, d)), whole((d, d)),
                  whole((1, d)), whole((1, d))],
        out_specs=tok(d),
        compiler_params=_cparams(1),
        name="mixer_merge",
    )(x, *dil_o, *dil_lse, y_b, o_cmp, o_sel, o_win, u_small, m_g, ex,
      w_br_a.astype(BF), w_br_b.astype(BF), w_br_c.astype(BF), w_o_mix.astype(BF),
      ln_g.reshape(1, d), ln_b.reshape(1, d))


XATTN_TM = 256


def _xattn_kernel(x_ref, k_ref, v_ref, wq_ref, wo_ref, g_ref, b_ref, o_ref):
    x = x_ref[...]
    d = x.shape[-1]
    hd = d // XATTN_HEADS
    q = jnp.dot(x.astype(BF), wq_ref[...], preferred_element_type=jnp.float32)
    outs = []
    for h in range(XATTN_HEADS):
        sl = slice(h * hd, (h + 1) * hd)
        s = _dot_nt((q[:, sl] * (hd ** -0.5)).astype(BF), k_ref[:, sl].astype(BF))
        p = jnp.exp(s - jnp.max(s, axis=-1, keepdims=True))
        den = jnp.sum(p, axis=-1, keepdims=True)
        outs.append(jnp.dot(p.astype(BF), v_ref[:, sl].astype(BF), preferred_element_type=jnp.float32) / den)
    o = jnp.concatenate(outs, axis=1)
    xa = jnp.dot(o.astype(BF), wo_ref[...], preferred_element_type=jnp.float32)
    o_ref[...] = _layer_norm_rows(DEEPNORM_ALPHA * x + xa, g_ref[...], b_ref[...])


def cross_attention_block_pallas(x, mem, w_xq, w_xk, w_xv, w_xo, ln_g, ln_b):
    bsz, seq, d = x.shape
    mem_len = mem.shape[1]
    mem2 = mem.reshape(-1, d)
    k = _mm(mem2, w_xk)
    v = _mm(mem2, w_xv)
    n_t = seq // XATTN_TM

    def whole(shape):
        return pl.BlockSpec(shape, lambda b, i: (0,) * len(shape))

    out = pl.pallas_call(
        _xattn_kernel,
        out_shape=jax.ShapeDtypeStruct((bsz * seq, d), jnp.float32),
        grid=(bsz, n_t),
        in_specs=[pl.BlockSpec((XATTN_TM, d), lambda b, i: (b * n_t + i, 0)),
                  pl.BlockSpec((mem_len, d), lambda b, i: (b, 0)),
                  pl.BlockSpec((mem_len, d), lambda b, i: (b, 0)),
                  whole((d, d)), whole((d, d)), whole((1, d)), whole((1, d))],
        out_specs=pl.BlockSpec((XATTN_TM, d), lambda b, i: (b * n_t + i, 0)),
        compiler_params=_cparams(2),
        name="cross_attention_block",
    )(x.reshape(-1, d), k, v, w_xq.astype(BF), w_xo.astype(BF), ln_g.reshape(1, d), ln_b.reshape(1, d))
    return out.reshape(bsz, seq, d)


def _ln_kernel(x_ref, g_ref, b_ref, o_ref, obf_ref):
    y = _layer_norm_rows(x_ref[...], g_ref[...], b_ref[...])
    o_ref[...] = y
    obf_ref[...] = y.astype(BF)


def layer_norm_pallas(x, g, b):
    n, d = x.shape
    tm = _pick_tile(n, (1024, 512, 256, 128, 64, 32, 16, 8))
    spec = pl.BlockSpec((tm, d), lambda i: (i, 0))
    return pl.pallas_call(
        _ln_kernel,
        out_shape=(jax.ShapeDtypeStruct((n, d), jnp.float32), jax.ShapeDtypeStruct((n, d), BF)),
        grid=(n // tm,),
        in_specs=[spec, pl.BlockSpec((1, d), lambda i: (0, 0)), pl.BlockSpec((1, d), lambda i: (0, 0))],
        out_specs=(spec, spec),
        compiler_params=_cparams(1),
        name="layer_norm",
    )(x, g.reshape(1, d), b.reshape(1, d))


MOE_TB = 256


def _moe_ffn_kernel(layer, be_ref, nxt_ref, nused_ref, x_ref, wgu_hbm, bgu_ref, wd_hbm, bd_ref, o_ref,
                    wgu_st, wd_st, wgu_bf, wd_bf, sem):
    i = pl.program_id(0)
    e = be_ref[i]
    changed = jnp.logical_or(i == 0, e != be_ref[jnp.maximum(i - 1, 0)])

    def fetch(expert):
        return (pltpu.make_async_copy(wgu_hbm.at[layer, expert], wgu_st, sem.at[0]),
                pltpu.make_async_copy(wd_hbm.at[layer, expert], wd_st, sem.at[1]))

    @pl.when(i == 0)
    def _():
        for cp in fetch(e):
            cp.start()

    @pl.when(changed)
    def _():
        for cp in fetch(e):
            cp.wait()
        wgu_bf[...] = wgu_st[...].astype(BF)
        wd_bf[...] = wd_st[...].astype(BF)

        @pl.when(nxt_ref[i] != e)
        def _():
            for cp in fetch(nxt_ref[i]):
                cp.start()

    @pl.when(i < nused_ref[0])
    def _():
        gu = jnp.dot(x_ref[...].astype(BF), wgu_bf[...], preferred_element_type=jnp.float32) + bgu_ref[...]
        gate = jnp.minimum(gu[:, :D_FF], SWIGLU_LIMIT)
        up = jnp.clip(gu[:, D_FF:], -SWIGLU_LIMIT, SWIGLU_LIMIT)
        act = (up + 1.0) * gate * (1.0 / (1.0 + jnp.exp(-SWIGLU_ALPHA * gate)))
        o_ref[...] = jnp.dot(act.astype(BF), wd_bf[...], preferred_element_type=jnp.float32) + bd_ref[...]

    @pl.when(i >= nused_ref[0])
    def _():
        o_ref[...] = jnp.zeros(o_ref.shape, o_ref.dtype)


def moe_expert_ffn_pallas(buf, block_e, next_e, n_used, layer, w_gu, b_gu, w_down, b_down):
    n_rows, d = buf.shape
    depth, n_e = w_gu.shape[:2]
    grid_spec = pltpu.PrefetchScalarGridSpec(
        num_scalar_prefetch=3,
        grid=(n_rows // MOE_TB,),
        in_specs=[
            pl.BlockSpec((MOE_TB, d), lambda i, be, nx, nu: (i, 0)),
            pl.BlockSpec(memory_space=pl.ANY),
            pl.BlockSpec((None, None, 1, 2 * D_FF), lambda i, be, nx, nu: (layer, be[i], 0, 0)),
            pl.BlockSpec(memory_space=pl.ANY),
            pl.BlockSpec((None, None, 1, d), lambda i, be, nx, nu: (layer, be[i], 0, 0)),
        ],
        out_specs=pl.BlockSpec((MOE_TB, d), lambda i, be, nx, nu: (i, 0)),
        scratch_shapes=[pltpu.VMEM((d, 2 * D_FF), jnp.float32), pltpu.VMEM((D_FF, d), jnp.float32),
                        pltpu.VMEM((d, 2 * D_FF), BF), pltpu.VMEM((D_FF, d), BF),
                        pltpu.SemaphoreType.DMA((2,))],
    )
    return pl.pallas_call(
        functools.partial(_moe_ffn_kernel, layer),
        out_shape=jax.ShapeDtypeStruct((n_rows, d), jnp.float32),
        grid_spec=grid_spec,
        compiler_params=_cparams(1),
        name="moe_expert_ffn",
    )(block_e, next_e, n_used, buf, w_gu, b_gu.reshape(depth, n_e, 1, -1), w_down, b_down.reshape(depth, n_e, 1, -1))


MOE_ROUTE_TM = 256


def _router_kernel(x_ref, wr_ref, br_ref, route_ref, counts_ref, carry_ref):
    tm = MOE_ROUTE_TM

    @pl.when(pl.program_id(0) == 0)
    def _():
        carry_ref[...] = jnp.zeros(carry_ref.shape, jnp.float32)

    logits = _dot_split(x_ref[...], wr_ref[...]) + br_ref[...]
    lane = lax.broadcasted_iota(jnp.int32, (tm, N_EXPERTS), 1).astype(jnp.float32)
    onehots, vals, idxs = [], [], []
    rest = logits
    for _ in range(TOP_K):
        m = jnp.max(rest, axis=-1, keepdims=True)
        idx = jnp.min(jnp.where(rest == m, lane, float(N_EXPERTS)), axis=-1, keepdims=True)
        hit = lane == idx
        onehots.append(hit)
        vals.append(m)
        idxs.append(idx)
        rest = jnp.where(hit, -jnp.inf, rest)
    exps = [jnp.exp(v - vals[0]) for v in vals]
    den = exps[0] + exps[1] + exps[2] + exps[3]
    cnt = jnp.zeros((tm, N_EXPERTS), jnp.float32)
    for hit in onehots:
        cnt = cnt + jnp.where(hit, 1.0, 0.0)
    earlier = (lax.broadcasted_iota(jnp.int32, (tm, tm), 1) < lax.broadcasted_iota(jnp.int32, (tm, tm), 0))
    before = jnp.dot(earlier.astype(BF), cnt.astype(BF), preferred_element_type=jnp.float32) + carry_ref[...]
    lane_out = lax.broadcasted_iota(jnp.int32, (tm, V7X_LANES), 1)
    out = jnp.zeros((tm, V7X_LANES), jnp.float32)
    for k in range(TOP_K):
        pos = jnp.sum(jnp.where(onehots[k], before, 0.0), axis=-1, keepdims=True)
        out = jnp.where(lane_out == k, idxs[k], out)
        out = jnp.where(lane_out == TOP_K + k, exps[k] / den, out)
        out = jnp.where(lane_out == 2 * TOP_K + k, pos, out)
    route_ref[...] = out
    carry_ref[...] = carry_ref[...] + jnp.sum(cnt, axis=0, keepdims=True)
    counts_ref[...] = carry_ref[...]


def moe_router_pallas(x, w_router, b_router):
    n, d = x.shape
    tm = MOE_ROUTE_TM
    return pl.pallas_call(
        _router_kernel,
        out_shape=(jax.ShapeDtypeStruct((n, V7X_LANES), jnp.float32),
                   jax.ShapeDtypeStruct((1, N_EXPERTS), jnp.float32)),
        grid=(n // tm,),
        in_specs=[pl.BlockSpec((tm, d), lambda i: (i, 0)),
                  pl.BlockSpec((d, N_EXPERTS), lambda i: (0, 0)),
                  pl.BlockSpec((1, N_EXPERTS), lambda i: (0, 0))],
        out_specs=(pl.BlockSpec((tm, V7X_LANES), lambda i: (i, 0)),
                   pl.BlockSpec((1, N_EXPERTS), lambda i: (0, 0))),
        scratch_shapes=[pltpu.VMEM((1, N_EXPERTS), jnp.float32)],
        compiler_params=_cparams(1),
        name="moe_router",
    )(x, w_router, b_router.reshape(1, -1))


def _row_copy(src, dst, sem):
    return pltpu.make_async_copy(src, dst, sem)


def _dispatch_kernel(zrow_ref, dest_ref, x_ref, buf_ref, zero_ref, sem, zsem):
    tm = MOE_ROUTE_TM

    @pl.when(pl.program_id(0) == 0)
    def _():
        zero_ref[...] = jnp.zeros(zero_ref.shape, jnp.float32)
        for j in range(zrow_ref.shape[0]):
            @pl.when(zrow_ref[j] >= 0)
            def _():
                row0 = pl.multiple_of(zrow_ref[j], MOE_TB)
                pltpu.make_async_copy(zero_ref, buf_ref.at[pl.ds(row0, MOE_TB)], zsem).start()
        for j in range(zrow_ref.shape[0]):
            @pl.when(zrow_ref[j] >= 0)
            def _():
                pltpu.make_async_copy(zero_ref, buf_ref.at[pl.ds(0, MOE_TB)], zsem).wait()

    def body(t, carry):
        for k in range(TOP_K):
            row = dest_ref[0, TOP_K * t + k]
            _row_copy(x_ref.at[pl.ds(t, 1)], buf_ref.at[pl.ds(row, 1)], sem).start(priority=k % 2)
        return carry

    lax.fori_loop(0, tm, body, 0)
    for _ in range(TOP_K):
        _row_copy(x_ref, buf_ref.at[pl.ds(0, tm)], sem).wait()


def moe_dispatch_pallas(x, dest, zero_rows, n_rows):
    n, d = x.shape
    tm = MOE_ROUTE_TM
    grid_spec = pltpu.PrefetchScalarGridSpec(
        num_scalar_prefetch=1,
        grid=(n // tm,),
        in_specs=[pl.BlockSpec((None, 1, TOP_K * tm), lambda i, z: (i, 0, 0), memory_space=pltpu.SMEM),
                  pl.BlockSpec((tm, d), lambda i, z: (i, 0))],
        out_specs=pl.BlockSpec(memory_space=pl.ANY),
        scratch_shapes=[pltpu.VMEM((MOE_TB, d), jnp.float32), pltpu.SemaphoreType.DMA, pltpu.SemaphoreType.DMA],
    )
    return pl.pallas_call(
        _dispatch_kernel,
        out_shape=jax.ShapeDtypeStruct((n_rows, d), jnp.float32),
        grid_spec=grid_spec,
        compiler_params=_cparams(1),
        name="moe_dispatch",
    )(zero_rows, dest, x)


def _combine_kernel(dest_ref, x_ref, route_ref, obuf_ref, g_ref, b_ref, o_ref, obf_ref, rows_ref, sem):
    tm = MOE_ROUTE_TM

    def body(t, carry):
        for k in range(TOP_K):
            row = dest_ref[0, TOP_K * t + k]
            _row_copy(obuf_ref.at[pl.ds(row, 1)], rows_ref.at[k, pl.ds(t, 1)], sem).start(priority=k % 2)
        return carry

    lax.fori_loop(0, tm, body, 0)
    for k in range(TOP_K):
        _row_copy(obuf_ref.at[pl.ds(0, tm)], rows_ref.at[k], sem).wait()
    ff = jnp.zeros(x_ref.shape, jnp.float32)
    for k in range(TOP_K):
        ff = ff + route_ref[:, TOP_K + k:TOP_K + k + 1] * rows_ref[k]
    y = _layer_norm_rows(DEEPNORM_ALPHA * x_ref[...] + ff, g_ref[...], b_ref[...])
    o_ref[...] = y
    obf_ref[...] = y.astype(BF)


def moe_combine_pallas(x, route, dest, out_buf, ln_g, ln_b):
    n, d = x.shape
    tm = MOE_ROUTE_TM
    return pl.pallas_call(
        _combine_kernel,
        out_shape=(jax.ShapeDtypeStruct((n, d), jnp.float32), jax.ShapeDtypeStruct((n, d), BF)),
        grid=(n // tm,),
        in_specs=[pl.BlockSpec((None, 1, TOP_K * tm), lambda i: (i, 0, 0), memory_space=pltpu.SMEM),
                  pl.BlockSpec((tm, d), lambda i: (i, 0)),
                  pl.BlockSpec((tm, V7X_LANES), lambda i: (i, 0)),
                  pl.BlockSpec(memory_space=pl.ANY),
                  pl.BlockSpec((1, d), lambda i: (0, 0)), pl.BlockSpec((1, d), lambda i: (0, 0))],
        out_specs=(pl.BlockSpec((tm, d), lambda i: (i, 0)), pl.BlockSpec((tm, d), lambda i: (i, 0))),
        scratch_shapes=[pltpu.VMEM((TOP_K, tm, d), jnp.float32), pltpu.SemaphoreType.DMA],
        compiler_params=_cparams(1),
        name="moe_combine",
    )(dest, x, route, out_buf, ln_g.reshape(1, d), ln_b.reshape(1, d))


def token_mixer_block(h, h_bf, w_in, b_in, w_alpha2, b_alpha, gla_norm_g, cmp_pe_k, cmp_w1_k, cmp_w2_k,
                      cmp_pe_v, cmp_w1_v, cmp_w2_v, w_br_a, w_br_b, w_br_c, w_o_mix, ln_g, ln_b):
    bsz, seq, d = h.shape
    offs = np.concatenate([[0], np.cumsum(IN_WIDTHS)]).tolist()
    h2 = h.reshape(-1, d)

    def proj(lo, hi, out_dtype=jnp.float32):
        return _mm(h_bf, w_in[:, offs[lo]:offs[hi]], b_in[offs[lo]:offs[hi]], out_dtype).reshape(bsz, seq, -1)

    u_dil = proj(0, 3)
    u_gla = proj(3, 7)
    w_small = jnp.concatenate([w_in[:, offs[7]:offs[8]], w_in[:, offs[15]:offs[16]]], axis=1)
    b_small = jnp.concatenate([b_in[offs[7]:offs[8]], b_in[offs[15]:offs[16]]])
    n_small = w_small.shape[1]
    w_small = jnp.pad(w_small, ((0, 0), (0, V7X_LANES - n_small)))
    b_small = jnp.pad(b_small, (0, V7X_LANES - n_small))
    u_small = _mm(h_bf, w_small, b_small)
    u_c = proj(8, 15, BF)
    m_g = _mm(h_bf, w_in[:, offs[16]:offs[17]], b_in[offs[16]:offs[17]], BF)
    dil_o, dil_lse = dilated_attention_pallas(u_dil, 0, A_W, 2 * A_W)
    y_b = gla_mixer_pallas(u_gla, u_small.reshape(bsz, seq, V7X_LANES), w_alpha2, b_alpha, gla_norm_g)
    kc, vc = nsa_compress_pallas(u_c, C_Q, C_Q + C_KV, cmp_pe_k, cmp_w1_k, cmp_w2_k, cmp_pe_v, cmp_w1_v, cmp_w2_v)
    o_cmp, selmask = nsa_cmp_attention_pallas(u_c, 0, kc, vc)
    o_sel = nsa_gqa_attention_pallas(u_c, 0, C_Q + 2 * C_KV, C_Q + 3 * C_KV, selmask)
    o_win = nsa_gqa_attention_pallas(u_c, 0, C_Q + 4 * C_KV, C_Q + 5 * C_KV)

    def flat(t):
        return t.reshape(bsz * seq, -1)

    out = mixer_merge_pallas(h2, [flat(t) for t in dil_o], [flat(t) for t in dil_lse], flat(y_b), flat(o_cmp),
                             flat(o_sel), flat(o_win), u_small, m_g, w_br_a, w_br_b, w_br_c, w_o_mix, ln_g, ln_b)
    return out.reshape(bsz, seq, d)


def moe_block(x, layer, w_router, b_router, w_gu, b_gu, w_down, b_down, ln_g, ln_b):
    n, d = x.shape
    route, counts = moe_router_pallas(x, w_router, b_router)
    ids = jnp.arange(N_EXPERTS, dtype=jnp.int32)
    expert = route[:, :TOP_K].astype(jnp.int32)
    pos = route[:, 2 * TOP_K:3 * TOP_K].astype(jnp.int32)
    counts = counts.reshape(-1).astype(jnp.int32)
    padded = (counts + MOE_TB - 1) // MOE_TB * MOE_TB
    pend = jnp.cumsum(padded)
    pstart = pend - padded
    dest = jnp.sum(jnp.where(expert[:, :, None] == ids, pstart, 0), axis=-1) + pos
    dest = dest.reshape(n // MOE_ROUTE_TM, 1, TOP_K * MOE_ROUTE_TM)
    n_rows = n * TOP_K + N_EXPERTS * MOE_TB
    n_blocks = n_rows // MOE_TB
    n_used = pend[-1:] // MOE_TB
    blk = jnp.minimum(jnp.arange(n_blocks, dtype=jnp.int32), n_used - 1) * MOE_TB
    block_e = jnp.minimum(jnp.sum((pend[None, :] <= blk[:, None]).astype(jnp.int32), axis=1), N_EXPERTS - 1)
    later_nonempty = (ids[None, :] > ids[:, None]) & (counts[None, :] > 0)
    next_of = jnp.min(jnp.where(later_nonempty, ids[None, :], N_EXPERTS), axis=1)
    next_of = jnp.where(next_of < N_EXPERTS, next_of, ids)
    next_e = jnp.sum(jnp.where(block_e[:, None] == ids, next_of, 0), axis=1)
    tail = pend[-1] + ids * MOE_TB
    zero_rows = jnp.concatenate([jnp.where(counts > 0, pend - MOE_TB, -1), jnp.where(tail < n_rows, tail, -1)])
    buf = moe_dispatch_pallas(x, dest, zero_rows, n_rows)
    out = moe_expert_ffn_pallas(buf, block_e, next_e, n_used, layer, w_gu, b_gu, w_down, b_down)
    return moe_combine_pallas(x, route, dest, out, ln_g, ln_b)


def kernel(x, mem, ln0_g, ln0_b, w_in, b_in, w_alpha2, b_alpha, gla_norm_g, cmp_pe_k, cmp_w1_k, cmp_w2_k, cmp_pe_v, cmp_w1_v, cmp_w2_v, w_br_a, w_br_b, w_br_c, w_o_mix, ln1_g, ln1_b, w_xq, w_xk, w_xv, w_xo, ln2_g, ln2_b, w_router, b_router, w_gu, b_gu, w_down, b_down, ln3_g, ln3_b):
    shape = x.shape
    x, x_bf = layer_norm_pallas(x.reshape(-1, shape[-1]), ln0_g, ln0_b)
    for li in range(DEPTH):
        x = token_mixer_block(x.reshape(shape), x_bf, w_in[li], b_in[li], w_alpha2[li], b_alpha[li], gla_norm_g[li],
                              cmp_pe_k[li], cmp_w1_k[li], cmp_w2_k[li], cmp_pe_v[li], cmp_w1_v[li], cmp_w2_v[li],
                              w_br_a[li], w_br_b[li], w_br_c[li], w_o_mix[li], ln1_g[li], ln1_b[li])
        x = cross_attention_block_pallas(x, mem, w_xq[li], w_xk[li], w_xv[li], w_xo[li], ln2_g[li], ln2_b[li])
        x, x_bf = moe_block(x.reshape(-1, shape[-1]), li, w_router[li], b_router[li], w_gu, b_gu, w_down, b_down,
                            ln3_g[li], ln3_b[li])
    return x.reshape(shape)
```

```python
import functools

import jax
import jax.numpy as jnp
import numpy as np
from jax import lax
from jax.experimental import pallas as pl
from jax.experimental.pallas import tpu as pltpu

D_MODEL = 1024
DEPTH = 2
HEAD_DIM = 64
N_BRANCHES = 3
LN_EPS = 1e-5

DIL_GROUPS = ((128, 1), (512, 4), (2048, 16))
DIL_HEADS = 4

GLA_HEADS = 4
GLA_DK = D_MODEL // 2 // GLA_HEADS
GLA_DV = D_MODEL // GLA_HEADS
GLA_RANK = 16
GLA_TAU = 16.0
GLA_CHUNK = 64

NSA_HEADS = 16
NSA_KV_GROUPS = 4
NSA_GROUP_HEADS = NSA_HEADS // NSA_KV_GROUPS
NSA_BRANCHES = 3
NSA_CMP_LEN = 32
NSA_CMP_STRIDE = 16
NSA_CMP_HIDDEN = 2 * HEAD_DIM
NSA_SEL_LEN = 64
NSA_N_SELECT = 16
NSA_WINDOW = 512

XATTN_HEADS = 4

N_EXPERTS = 32
TOP_K = 4
D_FF = D_MODEL
SWIGLU_LIMIT = 7.0
SWIGLU_ALPHA = 1.702

DEEPNORM_ALPHA = (2 * DEPTH) ** 0.25

A_W = len(DIL_GROUPS) * DIL_HEADS * HEAD_DIM
B_K = GLA_HEADS * GLA_DK
B_V = GLA_HEADS * GLA_DV
C_Q = NSA_HEADS * HEAD_DIM
C_KV = NSA_KV_GROUPS * HEAD_DIM
IN_WIDTHS = (A_W, A_W, A_W,
             B_K, B_K, B_V, B_V, GLA_RANK,
             C_Q, C_KV, C_KV, C_KV, C_KV, C_KV, C_KV, NSA_HEADS * NSA_BRANCHES,
             N_BRANCHES * D_MODEL)

V7X_VMEM_LIMIT_BYTES = 48 * 1024 * 1024
V7X_LANES = 128
NEG_BIG = -1e30
BF = jnp.bfloat16


def _cparams(n_axes):
    return pltpu.CompilerParams(dimension_semantics=("arbitrary",) * n_axes,
                                vmem_limit_bytes=V7X_VMEM_LIMIT_BYTES)


def _split_bf16(x):
    hi = x.astype(BF)
    lo = (x - hi.astype(jnp.float32)).astype(BF)
    return hi, lo


def _dot_split(a, b):
    a_hi, a_lo = _split_bf16(a)
    b_hi, b_lo = _split_bf16(b)
    f = functools.partial(jnp.dot, preferred_element_type=jnp.float32)
    return f(a_hi, b_hi) + (f(a_hi, b_lo) + f(a_lo, b_hi))


def _dot_split_lhs(a, b_exact):
    a_hi, a_lo = _split_bf16(a)
    f = functools.partial(jnp.dot, preferred_element_type=jnp.float32)
    return f(a_hi, b_exact) + f(a_lo, b_exact)


def _dot_nt(a, b):
    return lax.dot_general(a, b, (((1,), (1,)), ((), ())), preferred_element_type=jnp.float32)


def _mm_kernel(x_ref, w_ref, b_ref, o_ref):
    acc = jnp.dot(x_ref[...].astype(BF), w_ref[...].astype(BF), preferred_element_type=jnp.float32)
    o_ref[...] = (acc + b_ref[...]).astype(o_ref.dtype)


def _pick_tile(n, candidates):
    for c in candidates:
        if n % c == 0:
            return c
    return n


def _mm(x, w, b=None, out_dtype=jnp.float32):
    m, k = x.shape
    n = w.shape[1]
    if b is None:
        b = jnp.zeros((n,), jnp.float32)
    tm = _pick_tile(m, (1024, 512, 256, 128, 64, 32, 16, 8))
    tn = _pick_tile(n, (1536, 1280, 1152, 1024, 768, 512, 384, 256, 128))
    return pl.pallas_call(
        _mm_kernel,
        out_shape=jax.ShapeDtypeStruct((m, n), out_dtype),
        grid=(n // tn, m // tm),
        in_specs=[
            pl.BlockSpec((tm, k), lambda j, i: (i, 0)),
            pl.BlockSpec((k, tn), lambda j, i: (0, j)),
            pl.BlockSpec((1, tn), lambda j, i: (0, j)),
        ],
        out_specs=pl.BlockSpec((tm, tn), lambda j, i: (i, j)),
        compiler_params=_cparams(2),
        name="dense_proj",
    )(x, w, b.reshape(1, n))


def _mm3(x, w, b=None):
    lead = x.shape[:-1]
    return _mm(x.reshape(-1, x.shape[-1]), w, b).reshape(*lead, w.shape[1])


GQA_TQ = 256
GQA_TK_SELECT = 512
GQA_TK_BAND = 256


def _half_select(x, want_half, have_half):
    if want_half != have_half:
        x = pltpu.roll(x, HEAD_DIM, axis=1)
    return x


def _lane_half_mask(shape, half):
    lane = lax.broadcasted_iota(jnp.int32, shape, 1)
    return (lane >= half * HEAD_DIM) & (lane < (half + 1) * HEAD_DIM)


def _make_gqa_kernel(select, max_dist, n_outer):
    r, tq, tk = NSA_GROUP_HEADS, GQA_TQ, (GQA_TK_SELECT if select else GQA_TK_BAND)

    def kern(q_ref, k_ref, v_ref, *rest):
        if select:
            m_ref, e_ref, o_ref, mx_ref, acc_ref = rest
        else:
            o_ref, mx_ref, acc_ref = rest
        i = pl.program_id(n_outer)
        dist0 = i * tq + lax.broadcasted_iota(jnp.int32, (tk, tq), 1) - lax.broadcasted_iota(jnp.int32, (tk, tq), 0)
        qs = []
        for half in range(2):
            parts = []
            for rr in range(r):
                hh = half * r + rr
                qc = q_ref[:, (hh // 2) * V7X_LANES:(hh // 2 + 1) * V7X_LANES].astype(jnp.float32) * (HEAD_DIM ** -0.5)
                qc = _half_select(qc, half, hh % 2)
                parts.append(jnp.where(_lane_half_mask(qc.shape, half), qc, 0.0).astype(BF))
            qs.append(jnp.concatenate(parts, axis=0))
        mx_ref[...] = jnp.full(mx_ref.shape, NEG_BIG, jnp.float32)
        acc_ref[...] = jnp.zeros(acc_ref.shape, jnp.float32)
        v_lane = lax.broadcasted_iota(jnp.int32, (tk, V7X_LANES), 1)

        def body(j, carry):
            k0 = pl.multiple_of(j * tk, tk)
            kt = k_ref[pl.ds(k0, tk), :].astype(BF)
            vt = v_ref[pl.ds(k0, tk), :].astype(BF)
            dist = dist0 - k0
            in_band = (dist >= 0) if select else ((dist >= 0) & (dist <= max_dist))
            for half in range(2):
                if select:
                    blk = jnp.dot(e_ref[j], m_ref[half], preferred_element_type=jnp.float32)
                    valid = (blk > 0.5) & in_band
                else:
                    valid = in_band
                bias = jnp.where(valid, 0.0, NEG_BIG)
                s = _dot_nt(kt, qs[half]) + jnp.concatenate([bias] * r, axis=1)
                m_old = mx_ref[half]
                m_new = jnp.maximum(m_old, jnp.max(s, axis=0, keepdims=True))
                p = jnp.exp((s - m_new).astype(BF))
                v_aug = jnp.where((v_lane >= half * HEAD_DIM) & (v_lane < (half + 1) * HEAD_DIM), vt, 1.0)
                pv = lax.dot_general(v_aug, p, (((0,), (0,)), ((), ())), preferred_element_type=jnp.float32)
                acc_ref[half] = jnp.exp(m_old - m_new) * acc_ref[half] + pv
                mx_ref[half] = m_new
            return carry

        lo = 0 if select else jnp.maximum(i * tq - max_dist, 0) // tk
        lax.fori_loop(lo, (i * tq + tq - 1) // tk + 1, body, 0)
        outs = []
        for half in range(2):
            acc = acc_ref[half]
            den_row = (1 - half) * HEAD_DIM
            o = (acc / acc[den_row:den_row + 1, :]).T
            for rr in range(r):
                outs.append(_half_select(o[rr * tq:(rr + 1) * tq], (half * r + rr) % 2, half))
        lane = lax.broadcasted_iota(jnp.int32, (tq, V7X_LANES), 1)
        for c in range(r):
            both = jnp.where(lane < HEAD_DIM, outs[2 * c], outs[2 * c + 1])
            o_ref[:, c * V7X_LANES:(c + 1) * V7X_LANES] = both.astype(o_ref.dtype)

    return kern


def _sel_expand_matrix(seq):
    n_blk = seq // NSA_SEL_LEN
    e = (np.arange(seq)[:, None] // NSA_SEL_LEN == np.arange(n_blk)[None, :]).astype(np.float32)
    return e.reshape(seq // GQA_TK_SELECT, GQA_TK_SELECT, n_blk)


def nsa_gqa_attention_pallas(u, q_col, k_col, v_col, selmask=None):
    bsz, seq, _ = u.shape
    select = selmask is not None
    n_pairs = NSA_KV_GROUPS // 2
    qw = 2 * NSA_GROUP_HEADS * HEAD_DIM
    qb, kb, vb = q_col // qw, k_col // V7X_LANES, v_col // V7X_LANES
    rows = NSA_GROUP_HEADS * GQA_TQ
    in_specs = [
        pl.BlockSpec((None, GQA_TQ, qw), lambda b, gp, i: (b, i, qb + gp)),
        pl.BlockSpec((None, seq, V7X_LANES), lambda b, gp, i: (b, 0, kb + gp)),
        pl.BlockSpec((None, seq, V7X_LANES), lambda b, gp, i: (b, 0, vb + gp)),
    ]
    args = [u, u, u]
    if select:
        n_blk = selmask.shape[2]
        in_specs += [pl.BlockSpec((None, 2, n_blk, GQA_TQ), lambda b, gp, i: (b, gp, 0, i)),
                     pl.BlockSpec((seq // GQA_TK_SELECT, GQA_TK_SELECT, n_blk), lambda b, gp, i: (0, 0, 0))]
        args += [selmask, jnp.asarray(_sel_expand_matrix(seq), BF)]
    return pl.pallas_call(
        _make_gqa_kernel(select, NSA_WINDOW - 1, 2),
        out_shape=jax.ShapeDtypeStruct((bsz, seq, NSA_HEADS * HEAD_DIM), BF),
        grid=(bsz, n_pairs, seq // GQA_TQ),
        in_specs=in_specs,
        out_specs=pl.BlockSpec((None, GQA_TQ, qw), lambda b, gp, i: (b, i, gp)),
        scratch_shapes=[pltpu.VMEM((2, 1, rows), jnp.float32),
                        pltpu.VMEM((2, V7X_LANES, rows), jnp.float32)],
        compiler_params=_cparams(3),
        name="nsa_selected_attention" if select else "nsa_window_attention",
    )(*args)


DIL_TILE = 128


def _make_dil_kernel(dil, max_dist, seq):
    t = DIL_TILE
    n_sub = seq // dil // t
    n_kv = min(-(-max_dist // t) + 1, n_sub)

    def rows(first, size):
        return pl.ds(first, size, stride=dil) if dil > 1 else pl.ds(first, size)

    def kern(q_ref, k_ref, v_ref, o_ref, lse_ref):
        kw = n_kv * t
        lane = lax.broadcasted_iota(jnp.int32, (t, V7X_LANES), 1)
        for rr in range(dil):
            for i in range(n_sub):
                k_tile0 = max(i + 1 - n_kv, 0)
                dist = ((i - k_tile0) * t + lax.broadcasted_iota(jnp.int32, (t, kw), 0)
                        - lax.broadcasted_iota(jnp.int32, (t, kw), 1))
                bias = jnp.where((dist >= 0) & (dist <= max_dist), 0.0, NEG_BIG)
                q_rows = rows(rr + i * t * dil, t)
                k_rows = rows(rr + k_tile0 * t * dil, kw)
                qc = q_ref[q_rows, :] * (HEAD_DIM ** -0.5)
                kt = k_ref[k_rows, :].astype(BF)
                vt = v_ref[k_rows, :].astype(BF)
                os_, ls_ = [], []
                for half in range(2):
                    qh = jnp.where(_lane_half_mask(qc.shape, half), qc, 0.0).astype(BF)
                    s = _dot_nt(qh, kt) + bias
                    m = jnp.max(s, axis=-1, keepdims=True)
                    p = jnp.exp(s - m)
                    den = jnp.sum(p, axis=-1, keepdims=True)
                    os_.append(jnp.dot(p.astype(BF), vt, preferred_element_type=jnp.float32) / den)
                    ls_.append(jnp.broadcast_to(m + jnp.log(den), (t, V7X_LANES)))
                o_ref[q_rows, :] = jnp.where(lane < HEAD_DIM, os_[0], os_[1])
                lse_ref[q_rows, :] = jnp.where(lane < HEAD_DIM, ls_[0], ls_[1])

    return kern


def dilated_attention_pallas(u, q_col, k_col, v_col):
    bsz, seq, _ = u.shape
    gw = DIL_HEADS * HEAD_DIM
    outs, lses = [], []
    sds = jax.ShapeDtypeStruct((bsz, seq, gw), jnp.float32)
    out_spec = pl.BlockSpec((None, seq, V7X_LANES), lambda b, c: (b, 0, c))
    for gi, (window, dil) in enumerate(DIL_GROUPS):

        def cmap(col):
            blk = (col + gi * gw) // V7X_LANES
            return lambda b, c: (b, 0, blk + c)

        o, lse = pl.pallas_call(
            _make_dil_kernel(dil, window // dil, seq),
            out_shape=(sds, sds),
            grid=(bsz, gw // V7X_LANES),
            in_specs=[pl.BlockSpec((None, seq, V7X_LANES), cmap(c)) for c in (q_col, k_col, v_col)],
            out_specs=(out_spec, out_spec),
            compiler_params=_cparams(2),
            name="dilated_attention",
        )(u, u, u)
        outs.append(o)
        lses.append(lse)
    return outs, lses


GLA_SUB = 16


def _gla_kernel(q_ref, k_ref, v_ref, r_ref, lr_ref, wa_ref, ba_ref, g_ref, o_ref, state_ref):
    c, dk, dv, sub = GLA_CHUNK, GLA_DK, GLA_DV, GLA_SUB

    @pl.when(pl.program_id(1) == 0)
    def _():
        state_ref[...] = jnp.zeros(state_ref.shape, jnp.float32)

    x = _dot_split(lr_ref[...], wa_ref[...]) + ba_ref[...]
    log_a = (jnp.minimum(x, 0.0) - jnp.log1p(jnp.exp(-jnp.abs(x)))) * (1.0 / GLA_TAU)
    tri = (lax.broadcasted_iota(jnp.int32, (c, c), 0) >= lax.broadcasted_iota(jnp.int32, (c, c), 1))
    la_hi, la_lo = _split_bf16(log_a)
    tri = tri.astype(BF)
    b_all = (jnp.dot(tri, la_hi, preferred_element_type=jnp.float32)
             + jnp.dot(tri, la_lo, preferred_element_type=jnp.float32))
    col = lax.broadcasted_iota(jnp.int32, (sub, c), 1)
    row = lax.broadcasted_iota(jnp.int32, (sub, c), 0)
    for h in range(GLA_HEADS):
        q = q_ref[:, h * dk:(h + 1) * dk] * (dk ** -0.5)
        k = k_ref[:, h * dk:(h + 1) * dk]
        v = v_ref[:, h * dv:(h + 1) * dv]
        b = b_all[:, h * dk:(h + 1) * dk]
        a_rows = []
        for blk in range(c // sub):
            lo = blk * sub
            q_i, b_i, k_i = q[lo:lo + sub], b[lo:lo + sub], k[lo:lo + sub]
            if blk > 0:
                b_prev = b[lo - 1:lo]
                q_t = q_i * jnp.exp(b_i - b_prev)
                k_t = k * jnp.exp(jnp.minimum(b_prev - b, 0.0))
                acc = jnp.where(col < lo, _dot_nt(q_t.astype(BF), k_t.astype(BF)), 0.0)
            else:
                acc = jnp.zeros((sub, c), jnp.float32)
            for j in range(sub):
                t = q_i * k_i[j:j + 1] * jnp.exp(jnp.minimum(b_i - b_i[j:j + 1], 0.0))
                cs = jnp.sum(t, axis=-1, keepdims=True)
                acc = acc + jnp.where((col == lo + j) & (row >= j), cs, 0.0)
            a_rows.append(acc)
        attn = jnp.concatenate(a_rows, axis=0)
        st = state_ref[h]
        o = jnp.dot(attn.astype(BF), v.astype(BF), preferred_element_type=jnp.float32)
        o = o + _dot_nt((q * jnp.exp(b)).astype(BF), st.astype(BF))
        b_last = b[c - 1:c]
        k_d = k * jnp.exp(b_last - b)
        upd = lax.dot_general(v.astype(BF), k_d.astype(BF), (((0,), (0,)), ((), ())),
                              preferred_element_type=jnp.float32)
        state_ref[h] = st * jnp.exp(b_last) + upd
        o = o * lax.rsqrt(jnp.mean(o * o, axis=-1, keepdims=True) + LN_EPS)
        r = r_ref[:, h * dv:(h + 1) * dv]
        o_ref[:, h * dv:(h + 1) * dv] = o * g_ref[...] * (r / (1.0 + jnp.exp(-r)))


def gla_mixer_pallas(u_gla, u_small, w_alpha2, b_alpha, norm_g):
    bsz, seq, _ = u_gla.shape
    c = GLA_CHUNK

    def tok(width, blk):
        return pl.BlockSpec((None, c, width), lambda b, i: (b, i, blk))

    def whole(shape):
        return pl.BlockSpec(shape, lambda b, i: (0,) * len(shape))

    wa = jnp.pad(w_alpha2, ((0, V7X_LANES - GLA_RANK), (0, 0)))
    return pl.pallas_call(
        _gla_kernel,
        out_shape=jax.ShapeDtypeStruct((bsz, seq, B_V), jnp.float32),
        grid=(bsz, seq // c),
        in_specs=[tok(B_K, 0), tok(B_K, 1), tok(B_V, 1), tok(B_V, 2), tok(V7X_LANES, 0),
                  whole((V7X_LANES, B_K)), whole((1, B_K)), whole((1, GLA_DV))],
        out_specs=tok(B_V, 0),
        scratch_shapes=[pltpu.VMEM((GLA_HEADS, GLA_DV, GLA_DK), jnp.float32)],
        compiler_params=_cparams(2),
        name="gla_mixer",
    )(u_gla, u_gla, u_gla, u_gla, u_small, wa, b_alpha.reshape(1, -1), norm_g.reshape(1, -1))


CMP_HALF = NSA_CMP_STRIDE * HEAD_DIM


def _gelu_tanh(x):
    return 0.5 * x * (1.0 + jnp.tanh(np.sqrt(2.0 / np.pi) * (x + 0.044715 * (x * x * x))))


def _compress_kernel(hk_ref, hv_ref, pek_ref, w1k_ref, w2k_ref, pev_ref, w1v_ref, w2v_ref, kc_ref, vc_ref):
    for h_ref, pe_ref, w1_ref, w2_ref, o_ref in ((hk_ref, pek_ref, w1k_ref, w2k_ref, kc_ref),
                                                 (hv_ref, pev_ref, w1v_ref, w2v_ref, vc_ref)):
        w1a = w1_ref[0].astype(BF)
        w1b = w1_ref[1].astype(BF)
        pe = pe_ref[...].astype(BF)
        pe_term = (jnp.dot(pe[:, :CMP_HALF], w1a, preferred_element_type=jnp.float32)
                   + jnp.dot(pe[:, CMP_HALF:], w1b, preferred_element_type=jnp.float32))
        w2 = w2_ref[...].astype(BF)
        for g in range(NSA_KV_GROUPS):
            hb = h_ref[g].astype(BF)
            y1 = jnp.dot(hb, w1a, preferred_element_type=jnp.float32)
            y2 = jnp.dot(hb, w1b, preferred_element_type=jnp.float32)
            n_half = y2.shape[0]
            pre = y1 + pltpu.roll(y2, n_half - 1, axis=0) + pe_term
            out = jnp.dot(_gelu_tanh(pre).astype(BF), w2, preferred_element_type=jnp.float32)
            o_ref[g] = jnp.concatenate([out, out], axis=1)


def nsa_compress_pallas(u, k_col, v_col, pe_k, w1_k, w2_k, pe_v, w1_v, w2_v):
    bsz, seq, _ = u.shape
    n_half = seq // NSA_CMP_STRIDE

    def halves(col):
        t = u[..., col:col + C_KV].reshape(bsz, n_half, NSA_CMP_STRIDE, NSA_KV_GROUPS, HEAD_DIM)
        return t.transpose(0, 3, 1, 2, 4).reshape(bsz, NSA_KV_GROUPS, n_half, CMP_HALF)

    def whole(shape):
        return pl.BlockSpec(shape, lambda b: (0,) * len(shape))

    hspec = pl.BlockSpec((None, NSA_KV_GROUPS, n_half, CMP_HALF), lambda b: (b, 0, 0, 0))
    ospec = pl.BlockSpec((None, NSA_KV_GROUPS, n_half, V7X_LANES), lambda b: (b, 0, 0, 0))
    wspecs = [whole((1, 2 * CMP_HALF)), whole((2, CMP_HALF, NSA_CMP_HIDDEN)), whole((NSA_CMP_HIDDEN, HEAD_DIM))]
    sds = jax.ShapeDtypeStruct((bsz, NSA_KV_GROUPS, n_half, V7X_LANES), jnp.float32)
    return pl.pallas_call(
        _compress_kernel,
        out_shape=(sds, sds),
        grid=(bsz,),
        in_specs=[hspec, hspec] + wspecs + wspecs,
        out_specs=(ospec, ospec),
        compiler_params=_cparams(1),
        name="nsa_compress",
    )(halves(k_col), halves(v_col),
      pe_k.reshape(1, -1), w1_k.reshape(2, CMP_HALF, NSA_CMP_HIDDEN), w2_k,
      pe_v.reshape(1, -1), w1_v.reshape(2, CMP_HALF, NSA_CMP_HIDDEN), w2_v)


CMP_TQ = 1024


def _overlap_matrix(n_cmp_pad, n_blk):
    c0 = np.arange(n_cmp_pad)[:, None] * NSA_CMP_STRIDE
    s0 = np.arange(n_blk)[None, :] * NSA_SEL_LEN
    return ((c0 < s0 + NSA_SEL_LEN) & (c0 + NSA_CMP_LEN > s0)).astype(np.float32)


def _cmp_attn_kernel(q_ref, kc_ref, vc_ref, ov_ref, o_ref, m_ref):
    i = pl.program_id(2)
    tq = CMP_TQ
    n_c = kc_ref.shape[0]
    n_blk = ov_ref.shape[0]
    kc = kc_ref[...].astype(BF)
    vc = vc_ref[...].astype(BF)
    pos = i * tq + lax.broadcasted_iota(jnp.int32, (tq, n_c), 0)
    c_end = lax.broadcasted_iota(jnp.int32, (tq, n_c), 1) * NSA_CMP_STRIDE + (NSA_CMP_LEN - 1)
    cvalid = c_end <= pos
    lane = lax.broadcasted_iota(jnp.int32, (tq, V7X_LANES), 1)
    p_sum = jnp.zeros((tq, n_c), jnp.float32)
    outs = []
    for rr in range(NSA_GROUP_HEADS):
        qc = q_ref[:, (rr // 2) * V7X_LANES:(rr // 2 + 1) * V7X_LANES].astype(jnp.float32) * (HEAD_DIM ** -0.5)
        qh = jnp.where(_lane_half_mask(qc.shape, rr % 2), qc, 0.0).astype(BF)
        s = jnp.where(cvalid, _dot_nt(qh, kc), NEG_BIG)
        m = jnp.max(s, axis=-1, keepdims=True)
        p = jnp.where(cvalid, jnp.exp(s - m), 0.0)
        p = p / jnp.maximum(jnp.sum(p, axis=-1, keepdims=True), 1e-30)
        p_sum = p_sum + p
        outs.append(jnp.dot(p.astype(BF), vc, preferred_element_type=jnp.float32))
    for c in range(NSA_GROUP_HEADS // 2):
        both = jnp.where(lane < HEAD_DIM, outs[2 * c], outs[2 * c + 1])
        o_ref[:, c * V7X_LANES:(c + 1) * V7X_LANES] = both.astype(o_ref.dtype)
    ps_hi, ps_lo = _split_bf16(p_sum)
    imp = _dot_nt(ov_ref[...], ps_hi) + _dot_nt(ov_ref[...], ps_lo)
    j = lax.broadcasted_iota(jnp.int32, (n_blk, tq), 0)
    q_blk = (i * tq + lax.broadcasted_iota(jnp.int32, (n_blk, tq), 1)) // NSA_SEL_LEN
    forced = (j == 0) | (j == q_blk) | (j == q_blk - 1)
    imp = jnp.where(forced, jnp.inf, jnp.where(j > q_blk, -jnp.inf, imp))
    rank = jnp.zeros((n_blk, tq), jnp.float32)
    for jj in range(n_blk):
        other = imp[jj:jj + 1, :]
        beats = (other > imp) | ((other == imp) & (jj < j))
        rank = rank + jnp.where(beats, 1.0, 0.0)
    m_ref[...] = jnp.where(rank < float(min(NSA_N_SELECT, n_blk)), 1.0, 0.0).astype(m_ref.dtype)


def nsa_cmp_attention_pallas(u, q_col, kc, vc):
    bsz, seq, _ = u.shape
    n_c = kc.shape[2]
    n_blk = seq // NSA_SEL_LEN
    gw = NSA_GROUP_HEADS * HEAD_DIM
    qb = q_col // gw
    cspec = pl.BlockSpec((None, None, n_c, V7X_LANES), lambda b, g, i: (b, g, 0, 0))
    return pl.pallas_call(
        _cmp_attn_kernel,
        out_shape=(jax.ShapeDtypeStruct((bsz, seq, NSA_HEADS * HEAD_DIM), BF),
                   jax.ShapeDtypeStruct((bsz, NSA_KV_GROUPS, n_blk, seq), BF)),
        grid=(bsz, NSA_KV_GROUPS, seq // CMP_TQ),
        in_specs=[pl.BlockSpec((None, CMP_TQ, gw), lambda b, g, i: (b, i, qb + g)),
                  cspec, cspec,
                  pl.BlockSpec((n_blk, n_c), lambda b, g, i: (0, 0))],
        out_specs=(pl.BlockSpec((None, CMP_TQ, gw), lambda b, g, i: (b, i, g)),
                   pl.BlockSpec((None, None, n_blk, CMP_TQ), lambda b, g, i: (b, g, 0, i))),
        compiler_params=_cparams(3),
        name="nsa_cmp_attention",
    )(u, kc, vc, jnp.asarray(_overlap_matrix(n_c, n_blk).T, BF))


MERGE_TM = 256


def _layer_norm_rows(x, g, b):
    mu = jnp.mean(x, axis=-1, keepdims=True)
    xc = x - mu
    var = jnp.mean(xc * xc, axis=-1, keepdims=True)
    return xc * lax.rsqrt(var + LN_EPS) * g + b


def _gate_expand_matrices():
    e = np.zeros((NSA_BRANCHES, V7X_LANES, NSA_HEADS * HEAD_DIM), np.float32)
    for br in range(NSA_BRANCHES):
        for h in range(NSA_HEADS):
            e[br, GLA_RANK + NSA_BRANCHES * h + br, h * HEAD_DIM:(h + 1) * HEAD_DIM] = 1.0
    return e


def _sigmoid(x):
    return 1.0 / (1.0 + jnp.exp(-x))


def _merge_kernel(x_ref, o0_ref, o1_ref, o2_ref, l0_ref, l1_ref, l2_ref, yb_ref, oc_ref, os_ref, ow_ref,
                  us_ref, mg_ref, ex_ref, wa_ref, wb_ref, wc_ref, wo_ref, g_ref, b_ref, out_ref):
    l0, l1, l2 = l0_ref[...], l1_ref[...], l2_ref[...]
    lm = jnp.maximum(jnp.maximum(l0, l1), l2)
    e0, e1, e2 = jnp.exp(l0 - lm), jnp.exp(l1 - lm), jnp.exp(l2 - lm)
    y_a = (e0 * o0_ref[...] + e1 * o1_ref[...] + e2 * o2_ref[...]) / (e0 + e1 + e2)
    sg = _sigmoid(us_ref[...])
    f32 = jnp.float32
    y_c = (_dot_split_lhs(sg, ex_ref[0]) * oc_ref[...].astype(f32) + _dot_split_lhs(sg, ex_ref[1]) * os_ref[...].astype(f32)
           + _dot_split_lhs(sg, ex_ref[2]) * ow_ref[...].astype(f32))
    d = D_MODEL
    f = functools.partial(jnp.dot, preferred_element_type=jnp.float32)
    merged = (_sigmoid(mg_ref[:, :d].astype(f32)) * f(y_a.astype(BF), wa_ref[...])
              + _sigmoid(mg_ref[:, d:2 * d].astype(f32)) * f(yb_ref[...].astype(BF), wb_ref[...])
              + _sigmoid(mg_ref[:, 2 * d:].astype(f32)) * f(y_c.astype(BF), wc_ref[...]))
    mix = f(merged.astype(BF), wo_ref[...])
    out_ref[...] = _layer_norm_rows(DEEPNORM_ALPHA * x_ref[...] + mix, g_ref[...], b_ref[...])


def mixer_merge_pallas(x, dil_o, dil_lse, y_b, o_cmp, o_sel, o_win, u_small, m_g, w_br_a, w_br_b, w_br_c, w_o_mix, ln_g, ln_b):
    n, d = x.shape
    tm = MERGE_TM
    aw = DIL_HEADS * HEAD_DIM

    def tok(width):
        return pl.BlockSpec((tm, width), lambda i: (i, 0))

    def whole(shape):
        return pl.BlockSpec(shape, lambda i: (0,) * len(shape))

    ex = jnp.asarray(_gate_expand_matrices(), BF)
    return pl.pallas_call(
        _merge_kernel,
        out_shape=jax.ShapeDtypeStruct((n, d), jnp.float32),
        grid=(n // tm,),
        in_specs=[tok(d)] + [tok(aw)] * 6 + [tok(B_V)] + [tok(C_Q)] * 3 + [tok(V7X_LANES), tok(N_BRANCHES * d),
                  whole(ex.shape), whole((aw, d)), whole((B_V, d)), whole((C_Q, d)), whole((d, d)),
                  whole((1, d)), whole((1, d))],
        out_specs=tok(d),
        compiler_params=_cparams(1),
        name="mixer_merge",
    )(x, *dil_o, *dil_lse, y_b, o_cmp, o_sel, o_win, u_small, m_g, ex,
      w_br_a.astype(BF), w_br_b.astype(BF), w_br_c.astype(BF), w_o_mix.astype(BF),
      ln_g.reshape(1, d), ln_b.reshape(1, d))


XATTN_TM = 256


def _xattn_kernel(x_ref, k_ref, v_ref, wq_ref, wo_ref, g_ref, b_ref, o_ref):
    x = x_ref[...]
    d = x.shape[-1]
    hd = d // XATTN_HEADS
    q = jnp.dot(x.astype(BF), wq_ref[...], preferred_element_type=jnp.float32)
    outs = []
    for h in range(XATTN_HEADS):
        sl = slice(h * hd, (h + 1) * hd)
        s = _dot_nt((q[:, sl] * (hd ** -0.5)).astype(BF), k_ref[:, sl].astype(BF))
        p = jnp.exp(s - jnp.max(s, axis=-1, keepdims=True))
        den = jnp.sum(p, axis=-1, keepdims=True)
        outs.append(jnp.dot(p.astype(BF), v_ref[:, sl].astype(BF), preferred_element_type=jnp.float32) / den)
    o = jnp.concatenate(outs, axis=1)
    xa = jnp.dot(o.astype(BF), wo_ref[...], preferred_element_type=jnp.float32)
    o_ref[...] = _layer_norm_rows(DEEPNORM_ALPHA * x + xa, g_ref[...], b_ref[...])


def cross_attention_block_pallas(x, mem, w_xq, w_xk, w_xv, w_xo, ln_g, ln_b):
    bsz, seq, d = x.shape
    mem_len = mem.shape[1]
    mem2 = mem.reshape(-1, d)
    k = _mm(mem2, w_xk)
    v = _mm(mem2, w_xv)
    n_t = seq // XATTN_TM

    def whole(shape):
        return pl.BlockSpec(shape, lambda b, i: (0,) * len(shape))

    out = pl.pallas_call(
        _xattn_kernel,
        out_shape=jax.ShapeDtypeStruct((bsz * seq, d), jnp.float32),
        grid=(bsz, n_t),
        in_specs=[pl.BlockSpec((XATTN_TM, d), lambda b, i: (b * n_t + i, 0)),
                  pl.BlockSpec((mem_len, d), lambda b, i: (b, 0)),
                  pl.BlockSpec((mem_len, d), lambda b, i: (b, 0)),
                  whole((d, d)), whole((d, d)), whole((1, d)), whole((1, d))],
        out_specs=pl.BlockSpec((XATTN_TM, d), lambda b, i: (b * n_t + i, 0)),
        compiler_params=_cparams(2),
        name="cross_attention_block",
    )(x.reshape(-1, d), k, v, w_xq.astype(BF), w_xo.astype(BF), ln_g.reshape(1, d), ln_b.reshape(1, d))
    return out.reshape(bsz, seq, d)


def _ln_kernel(x_ref, g_ref, b_ref, o_ref, obf_ref):
    y = _layer_norm_rows(x_ref[...], g_ref[...], b_ref[...])
    o_ref[...] = y
    obf_ref[...] = y.astype(BF)


def layer_norm_pallas(x, g, b):
    n, d = x.shape
    tm = _pick_tile(n, (1024, 512, 256, 128, 64, 32, 16, 8))
    spec = pl.BlockSpec((tm, d), lambda i: (i, 0))
    return pl.pallas_call(
        _ln_kernel,
        out_shape=(jax.ShapeDtypeStruct((n, d), jnp.float32), jax.ShapeDtypeStruct((n, d), BF)),
        grid=(n // tm,),
        in_specs=[spec, pl.BlockSpec((1, d), lambda i: (0, 0)), pl.BlockSpec((1, d), lambda i: (0, 0))],
        out_specs=(spec, spec),
        compiler_params=_cparams(1),
        name="layer_norm",
    )(x, g.reshape(1, d), b.reshape(1, d))


MOE_TB = 256


def _moe_ffn_kernel(layer, be_ref, nxt_ref, nused_ref, x_ref, wgu_hbm, bgu_ref, wd_hbm, bd_ref, o_ref,
                    wgu_st, wd_st, wgu_bf, wd_bf, sem):
    i = pl.program_id(0)
    e = be_ref[i]
    changed = jnp.logical_or(i == 0, e != be_ref[jnp.maximum(i - 1, 0)])

    def fetch(expert):
        return (pltpu.make_async_copy(wgu_hbm.at[layer, expert], wgu_st, sem.at[0]),
                pltpu.make_async_copy(wd_hbm.at[layer, expert], wd_st, sem.at[1]))

    @pl.when(i == 0)
    def _():
        for cp in fetch(e):
            cp.start()

    @pl.when(changed)
    def _():
        for cp in fetch(e):
            cp.wait()
        wgu_bf[...] = wgu_st[...].astype(BF)
        wd_bf[...] = wd_st[...].astype(BF)

        @pl.when(nxt_ref[i] != e)
        def _():
            for cp in fetch(nxt_ref[i]):
                cp.start()

    @pl.when(i < nused_ref[0])
    def _():
        gu = jnp.dot(x_ref[...].astype(BF), wgu_bf[...], preferred_element_type=jnp.float32) + bgu_ref[...]
        gate = jnp.minimum(gu[:, :D_FF], SWIGLU_LIMIT)
        up = jnp.clip(gu[:, D_FF:], -SWIGLU_LIMIT, SWIGLU_LIMIT)
        act = (up + 1.0) * gate * (1.0 / (1.0 + jnp.exp(-SWIGLU_ALPHA * gate)))
        o_ref[...] = jnp.dot(act.astype(BF), wd_bf[...], preferred_element_type=jnp.float32) + bd_ref[...]

    @pl.when(i >= nused_ref[0])
    def _():
        o_ref[...] = jnp.zeros(o_ref.shape, o_ref.dtype)


def moe_expert_ffn_pallas(buf, block_e, next_e, n_used, layer, w_gu, b_gu, w_down, b_down):
    n_rows, d = buf.shape
    depth, n_e = w_gu.shape[:2]
    grid_spec = pltpu.PrefetchScalarGridSpec(
        num_scalar_prefetch=3,
        grid=(n_rows // MOE_TB,),
        in_specs=[
            pl.BlockSpec((MOE_TB, d), lambda i, be, nx, nu: (i, 0)),
            pl.BlockSpec(memory_space=pl.ANY),
            pl.BlockSpec((None, None, 1, 2 * D_FF), lambda i, be, nx, nu: (layer, be[i], 0, 0)),
            pl.BlockSpec(memory_space=pl.ANY),
            pl.BlockSpec((None, None, 1, d), lambda i, be, nx, nu: (layer, be[i], 0, 0)),
        ],
        out_specs=pl.BlockSpec((MOE_TB, d), lambda i, be, nx, nu: (i, 0)),
        scratch_shapes=[pltpu.VMEM((d, 2 * D_FF), jnp.float32), pltpu.VMEM((D_FF, d), jnp.float32),
                        pltpu.VMEM((d, 2 * D_FF), BF), pltpu.VMEM((D_FF, d), BF),
                        pltpu.SemaphoreType.DMA((2,))],
    )
    return pl.pallas_call(
        functools.partial(_moe_ffn_kernel, layer),
        out_shape=jax.ShapeDtypeStruct((n_rows, d), jnp.float32),
        grid_spec=grid_spec,
        compiler_params=_cparams(1),
        name="moe_expert_ffn",
    )(block_e, next_e, n_used, buf, w_gu, b_gu.reshape(depth, n_e, 1, -1), w_down, b_down.reshape(depth, n_e, 1, -1))


MOE_ROUTE_TM = 256
MOE_RUN_ALIGN = 8


def _router_kernel(x_ref, wr_ref, br_ref, route_ref, counts_ref, tile_n_ref, tile_c_ref, carry_ref):
    tm = MOE_ROUTE_TM

    @pl.when(pl.program_id(0) == 0)
    def _():
        carry_ref[...] = jnp.zeros(carry_ref.shape, jnp.float32)

    logits = _dot_split(x_ref[...], wr_ref[...]) + br_ref[...]
    lane = lax.broadcasted_iota(jnp.int32, (tm, N_EXPERTS), 1).astype(jnp.float32)
    onehots, vals, idxs = [], [], []
    rest = logits
    for _ in range(TOP_K):
        m = jnp.max(rest, axis=-1, keepdims=True)
        idx = jnp.min(jnp.where(rest == m, lane, float(N_EXPERTS)), axis=-1, keepdims=True)
        hit = lane == idx
        onehots.append(hit)
        vals.append(m)
        idxs.append(idx)
        rest = jnp.where(hit, -jnp.inf, rest)
    exps = [jnp.exp(v - vals[0]) for v in vals]
    den = exps[0] + exps[1] + exps[2] + exps[3]
    cnt = jnp.zeros((tm, N_EXPERTS), jnp.float32)
    for hit in onehots:
        cnt = cnt + jnp.where(hit, 1.0, 0.0)
    earlier = (lax.broadcasted_iota(jnp.int32, (tm, tm), 1) < lax.broadcasted_iota(jnp.int32, (tm, tm), 0))
    local_before = jnp.dot(earlier.astype(BF), cnt.astype(BF), preferred_element_type=jnp.float32)
    carry = carry_ref[...]
    tile_n = jnp.sum(cnt, axis=0, keepdims=True)
    tile_n = jnp.floor((tile_n + (MOE_RUN_ALIGN - 1.0)) * (1.0 / MOE_RUN_ALIGN)) * MOE_RUN_ALIGN
    lower = (lax.broadcasted_iota(jnp.int32, (N_EXPERTS, N_EXPERTS), 0)
             < lax.broadcasted_iota(jnp.int32, (N_EXPERTS, N_EXPERTS), 1))
    tile_off = jnp.dot(jnp.broadcast_to(tile_n, (8, N_EXPERTS)).astype(BF), lower.astype(BF),
                       preferred_element_type=jnp.float32)[:1]
    lane_out = lax.broadcasted_iota(jnp.int32, (tm, V7X_LANES), 1)
    out = jnp.zeros((tm, V7X_LANES), jnp.float32)
    for k in range(TOP_K):
        pos = jnp.sum(jnp.where(onehots[k], local_before + carry, 0.0), axis=-1, keepdims=True)
        rank = jnp.sum(jnp.where(onehots[k], local_before + tile_off, 0.0), axis=-1, keepdims=True)
        out = jnp.where(lane_out == k, idxs[k], out)
        out = jnp.where(lane_out == TOP_K + k, exps[k] / den, out)
        out = jnp.where(lane_out == 2 * TOP_K + k, pos, out)
        out = jnp.where(lane_out == 3 * TOP_K + k, rank, out)
    route_ref[...] = out
    tile_n_ref[...] = tile_n
    tile_c_ref[...] = carry
    carry_ref[...] = carry + tile_n
    counts_ref[...] = carry + tile_n


def moe_router_pallas(x, w_router, b_router):
    n, d = x.shape
    tm = MOE_ROUTE_TM
    return pl.pallas_call(
        _router_kernel,
        out_shape=(jax.ShapeDtypeStruct((n, V7X_LANES), jnp.float32),
                   jax.ShapeDtypeStruct((1, N_EXPERTS), jnp.float32),
                   jax.ShapeDtypeStruct((n // tm, 1, N_EXPERTS), jnp.float32),
                   jax.ShapeDtypeStruct((n // tm, 1, N_EXPERTS), jnp.float32)),
        grid=(n // tm,),
        in_specs=[pl.BlockSpec((tm, d), lambda i: (i, 0)),
                  pl.BlockSpec((d, N_EXPERTS), lambda i: (0, 0)),
                  pl.BlockSpec((1, N_EXPERTS), lambda i: (0, 0))],
        out_specs=(pl.BlockSpec((tm, V7X_LANES), lambda i: (i, 0)),
                   pl.BlockSpec((1, N_EXPERTS), lambda i: (0, 0)),
                   pl.BlockSpec((None, 1, N_EXPERTS), lambda i: (i, 0, 0)),
                   pl.BlockSpec((None, 1, N_EXPERTS), lambda i: (i, 0, 0))),
        scratch_shapes=[pltpu.VMEM((1, N_EXPERTS), jnp.float32)],
        compiler_params=_cparams(1),
        name="moe_router",
    )(x, w_router, b_router.reshape(1, -1))


MOE_RUN_BITS = tuple(range(MOE_ROUTE_TM.bit_length() - 1, MOE_RUN_ALIGN.bit_length() - 2, -1))
MOE_SORT_ROWS = TOP_K * MOE_ROUTE_TM + N_EXPERTS * MOE_RUN_ALIGN
MOE_RUNS_PER_TILE = N_EXPERTS + 1


def _for_each_run_chunk(tile, n_ref, sorted_off_ref, buf_off_ref, make_copy):
    for e in range(MOE_RUNS_PER_TILE):
        n = n_ref[tile * MOE_RUNS_PER_TILE + e]
        s_row = sorted_off_ref[tile * MOE_RUNS_PER_TILE + e]
        b_row = buf_off_ref[tile * MOE_RUNS_PER_TILE + e]
        for bit in MOE_RUN_BITS:
            size = 1 << bit
            hit = (n & size) != 0

            @pl.when(hit)
            def _(s_row=s_row, b_row=b_row, size=size):
                make_copy(pl.multiple_of(s_row, MOE_RUN_ALIGN), pl.multiple_of(b_row, MOE_RUN_ALIGN), size).start()

            step = jnp.where(hit, size, 0)
            s_row = s_row + step
            b_row = b_row + step


def _rank_onehot(route_ref, k):
    tm = MOE_ROUTE_TM
    ranks = lax.broadcasted_iota(jnp.int32, (tm, MOE_SORT_ROWS), 1).astype(jnp.float32)
    return ranks == route_ref[:, 3 * TOP_K + k:3 * TOP_K + k + 1]


def _dispatch_kernel(zrow_ref, n_ref, soff_ref, boff_ref, x_ref, route_ref, buf_ref, zero_ref, sorted_ref, sem, zsem):
    tm = MOE_ROUTE_TM
    tile = pl.program_id(0)

    @pl.when(pl.program_id(0) == 0)
    def _():
        zero_ref[...] = jnp.zeros(zero_ref.shape, jnp.float32)
        for j in range(zrow_ref.shape[0]):
            @pl.when(zrow_ref[j] >= 0)
            def _():
                row0 = pl.multiple_of(zrow_ref[j], MOE_TB)
                pltpu.make_async_copy(zero_ref, buf_ref.at[pl.ds(row0, MOE_TB)], zsem).start()
        for j in range(zrow_ref.shape[0]):
            @pl.when(zrow_ref[j] >= 0)
            def _():
                pltpu.make_async_copy(zero_ref, buf_ref.at[pl.ds(0, MOE_TB)], zsem).wait()

    perm = _rank_onehot(route_ref, 0)
    for k in range(1, TOP_K):
        perm = perm | _rank_onehot(route_ref, k)
    sorted_ref[...] = lax.dot_general(jnp.where(perm, 1.0, 0.0).astype(BF), x_ref[...].astype(BF), (((0,), (0,)), ((), ())),
                                      preferred_element_type=jnp.float32)
    _for_each_run_chunk(tile, n_ref, soff_ref, boff_ref, lambda s_row, b_row, size: pltpu.make_async_copy(
        sorted_ref.at[pl.ds(s_row, size)], buf_ref.at[pl.ds(b_row, size)], sem))
    pltpu.make_async_copy(sorted_ref, buf_ref.at[pl.ds(0, MOE_SORT_ROWS)], sem).wait()


def _run_tables(tile_n, tile_c, pstart, trash_row):
    n_t = tile_n.reshape(-1, N_EXPERTS).astype(jnp.int32)
    c_t = tile_c.reshape(-1, N_EXPERTS).astype(jnp.int32)
    ends = jnp.cumsum(n_t, axis=1)
    total = ends[:, -1:]
    n_all = jnp.concatenate([n_t, MOE_SORT_ROWS - total], axis=1)
    sorted_off = jnp.concatenate([ends - n_t, total], axis=1)
    buf_off = jnp.concatenate([pstart[None, :] + c_t, jnp.full_like(total, trash_row)], axis=1)
    return n_all.reshape(-1), sorted_off.reshape(-1), buf_off.reshape(-1)


def moe_dispatch_pallas(x, route, runs, zero_rows, n_rows):
    n, d = x.shape
    tm = MOE_ROUTE_TM
    grid_spec = pltpu.PrefetchScalarGridSpec(
        num_scalar_prefetch=4,
        grid=(n // tm,),
        in_specs=[pl.BlockSpec((tm, d), lambda i, *_: (i, 0)),
                  pl.BlockSpec((tm, V7X_LANES), lambda i, *_: (i, 0))],
        out_specs=pl.BlockSpec(memory_space=pl.ANY),
        scratch_shapes=[pltpu.VMEM((MOE_TB, d), jnp.float32), pltpu.VMEM((MOE_SORT_ROWS, d), jnp.float32),
                        pltpu.SemaphoreType.DMA, pltpu.SemaphoreType.DMA],
    )
    return pl.pallas_call(
        _dispatch_kernel,
        out_shape=jax.ShapeDtypeStruct((n_rows, d), jnp.float32),
        grid_spec=grid_spec,
        compiler_params=_cparams(1),
        name="moe_dispatch",
    )(zero_rows, *runs, x, route)


def _combine_kernel(n_ref, soff_ref, boff_ref, x_ref, route_ref, obuf_ref, g_ref, b_ref, o_ref, obf_ref, sorted_ref, sem):
    tm = MOE_ROUTE_TM
    tile = pl.program_id(0)
    _for_each_run_chunk(tile, n_ref, soff_ref, boff_ref, lambda s_row, b_row, size: pltpu.make_async_copy(
        obuf_ref.at[pl.ds(b_row, size)], sorted_ref.at[pl.ds(s_row, size)], sem))
    gperm = jnp.zeros((tm, MOE_SORT_ROWS), jnp.float32)
    for k in range(TOP_K):
        gperm = jnp.where(_rank_onehot(route_ref, k), route_ref[:, TOP_K + k:TOP_K + k + 1], gperm)
    pltpu.make_async_copy(obuf_ref.at[pl.ds(0, MOE_SORT_ROWS)], sorted_ref, sem).wait()
    ff = _dot_split(gperm, sorted_ref[...])
    y = _layer_norm_rows(DEEPNORM_ALPHA * x_ref[...] + ff, g_ref[...], b_ref[...])
    o_ref[...] = y
    obf_ref[...] = y.astype(BF)


def moe_combine_pallas(x, route, runs, out_buf, ln_g, ln_b):
    n, d = x.shape
    tm = MOE_ROUTE_TM
    spec = pl.BlockSpec((tm, d), lambda i, *_: (i, 0))
    grid_spec = pltpu.PrefetchScalarGridSpec(
        num_scalar_prefetch=3,
        grid=(n // tm,),
        in_specs=[spec,
                  pl.BlockSpec((tm, V7X_LANES), lambda i, *_: (i, 0)),
                  pl.BlockSpec(memory_space=pl.ANY),
                  pl.BlockSpec((1, d), lambda i, *_: (0, 0)), pl.BlockSpec((1, d), lambda i, *_: (0, 0))],
        out_specs=(spec, spec),
        scratch_shapes=[pltpu.VMEM((MOE_SORT_ROWS, d), jnp.float32), pltpu.SemaphoreType.DMA],
    )
    return pl.pallas_call(
        _combine_kernel,
        out_shape=(jax.ShapeDtypeStruct((n, d), jnp.float32), jax.ShapeDtypeStruct((n, d), BF)),
        grid_spec=grid_spec,
        compiler_params=_cparams(1),
        name="moe_combine",
    )(*runs, x, route, out_buf, ln_g.reshape(1, d), ln_b.reshape(1, d))


def token_mixer_block(h, h_bf, w_in, b_in, w_alpha2, b_alpha, gla_norm_g, cmp_pe_k, cmp_w1_k, cmp_w2_k,
                      cmp_pe_v, cmp_w1_v, cmp_w2_v, w_br_a, w_br_b, w_br_c, w_o_mix, ln_g, ln_b):
    bsz, seq, d = h.shape
    offs = np.concatenate([[0], np.cumsum(IN_WIDTHS)]).tolist()
    h2 = h.reshape(-1, d)

    def proj(lo, hi, out_dtype=jnp.float32):
        return _mm(h_bf, w_in[:, offs[lo]:offs[hi]], b_in[offs[lo]:offs[hi]], out_dtype).reshape(bsz, seq, -1)

    u_dil = proj(0, 3)
    u_gla = proj(3, 7)
    w_small = jnp.concatenate([w_in[:, offs[7]:offs[8]], w_in[:, offs[15]:offs[16]]], axis=1)
    b_small = jnp.concatenate([b_in[offs[7]:offs[8]], b_in[offs[15]:offs[16]]])
    n_small = w_small.shape[1]
    w_small = jnp.pad(w_small, ((0, 0), (0, V7X_LANES - n_small)))
    b_small = jnp.pad(b_small, (0, V7X_LANES - n_small))
    u_small = _mm(h_bf, w_small, b_small)
    u_c = proj(8, 15, BF)
    m_g = _mm(h_bf, w_in[:, offs[16]:offs[17]], b_in[offs[16]:offs[17]], BF)
    dil_o, dil_lse = dilated_attention_pallas(u_dil, 0, A_W, 2 * A_W)
    y_b = gla_mixer_pallas(u_gla, u_small.reshape(bsz, seq, V7X_LANES), w_alpha2, b_alpha, gla_norm_g)
    kc, vc = nsa_compress_pallas(u_c, C_Q, C_Q + C_KV, cmp_pe_k, cmp_w1_k, cmp_w2_k, cmp_pe_v, cmp_w1_v, cmp_w2_v)
    o_cmp, selmask = nsa_cmp_attention_pallas(u_c, 0, kc, vc)
    o_sel = nsa_gqa_attention_pallas(u_c, 0, C_Q + 2 * C_KV, C_Q + 3 * C_KV, selmask)
    o_win = nsa_gqa_attention_pallas(u_c, 0, C_Q + 4 * C_KV, C_Q + 5 * C_KV)

    def flat(t):
        return t.reshape(bsz * seq, -1)

    out = mixer_merge_pallas(h2, [flat(t) for t in dil_o], [flat(t) for t in dil_lse], flat(y_b), flat(o_cmp),
                             flat(o_sel), flat(o_win), u_small, m_g, w_br_a, w_br_b, w_br_c, w_o_mix, ln_g, ln_b)
    return out.reshape(bsz, seq, d)


def moe_block(x, layer, w_router, b_router, w_gu, b_gu, w_down, b_down, ln_g, ln_b):
    n, d = x.shape
    route, counts, tile_n, tile_c = moe_router_pallas(x, w_router, b_router)
    ids = jnp.arange(N_EXPERTS, dtype=jnp.int32)
    counts = counts.reshape(-1).astype(jnp.int32)
    padded = (counts + MOE_TB - 1) // MOE_TB * MOE_TB
    pend = jnp.cumsum(padded)
    pstart = pend - padded
    n_rows = n * TOP_K + (n // MOE_ROUTE_TM) * N_EXPERTS * MOE_RUN_ALIGN + (N_EXPERTS + 1) * MOE_TB
    runs = _run_tables(tile_n, tile_c, pstart, n_rows - MOE_TB)
    n_blocks = n_rows // MOE_TB
    n_used = pend[-1:] // MOE_TB
    blk = jnp.minimum(jnp.arange(n_blocks, dtype=jnp.int32), n_used - 1) * MOE_TB
    block_e = jnp.minimum(jnp.sum((pend[None, :] <= blk[:, None]).astype(jnp.int32), axis=1), N_EXPERTS - 1)
    later_nonempty = (ids[None, :] > ids[:, None]) & (counts[None, :] > 0)
    next_of = jnp.min(jnp.where(later_nonempty, ids[None, :], N_EXPERTS), axis=1)
    next_of = jnp.where(next_of < N_EXPERTS, next_of, ids)
    next_e = jnp.sum(jnp.where(block_e[:, None] == ids, next_of, 0), axis=1)
    tail = pend[-1] + jnp.arange((n_rows - n * TOP_K) // MOE_TB, dtype=jnp.int32) * MOE_TB
    zero_rows = jnp.concatenate([jnp.where(counts > 0, pend - MOE_TB, -1), jnp.where(tail < n_rows, tail, -1)])
    buf = moe_dispatch_pallas(x, route, runs, zero_rows, n_rows)
    out = moe_expert_ffn_pallas(buf, block_e, next_e, n_used, layer, w_gu, b_gu, w_down, b_down)
    return moe_combine_pallas(x, route, runs, out, ln_g, ln_b)


def kernel(x, mem, ln0_g, ln0_b, w_in, b_in, w_alpha2, b_alpha, gla_norm_g, cmp_pe_k, cmp_w1_k, cmp_w2_k, cmp_pe_v, cmp_w1_v, cmp_w2_v, w_br_a, w_br_b, w_br_c, w_o_mix, ln1_g, ln1_b, w_xq, w_xk, w_xv, w_xo, ln2_g, ln2_b, w_router, b_router, w_gu, b_gu, w_down, b_down, ln3_g, ln3_b):
    shape = x.shape
    x, x_bf = layer_norm_pallas(x.reshape(-1, shape[-1]), ln0_g, ln0_b)
    for li in range(DEPTH):
        x = token_mixer_block(x.reshape(shape), x_bf, w_in[li], b_in[li], w_alpha2[li], b_alpha[li], gla_norm_g[li],
                              cmp_pe_k[li], cmp_w1_k[li], cmp_w2_k[li], cmp_pe_v[li], cmp_w1_v[li], cmp_w2_v[li],
                              w_br_a[li], w_br_b[li], w_br_c[li], w_o_mix[li], ln1_g[li], ln1_b[li])
        x = cross_attention_block_pallas(x, mem, w_xq[li], w_xk[li], w_xv[li], w_xo[li], ln2_g[li], ln2_b[li])
        x, x_bf = moe_block(x.reshape(-1, shape[-1]), li, w_router[li], b_router[li], w_gu, b_gu, w_down, b_down,
                            ln3_g[li], ln3_b[li])
    return x.reshape(shape)
```

```python
import functools

import jax
import jax.numpy as jnp
import numpy as np
from jax import lax
from jax.experimental import pallas as pl
from jax.experimental.pallas import tpu as pltpu

D_MODEL = 1024
DEPTH = 2
HEAD_DIM = 64
N_BRANCHES = 3
LN_EPS = 1e-5

DIL_GROUPS = ((128, 1), (512, 4), (2048, 16))
DIL_HEADS = 4

GLA_HEADS = 4
GLA_DK = D_MODEL // 2 // GLA_HEADS
GLA_DV = D_MODEL // GLA_HEADS
GLA_RANK = 16
GLA_TAU = 16.0
GLA_CHUNK = 64

NSA_HEADS = 16
NSA_KV_GROUPS = 4
NSA_GROUP_HEADS = NSA_HEADS // NSA_KV_GROUPS
NSA_BRANCHES = 3
NSA_CMP_LEN = 32
NSA_CMP_STRIDE = 16
NSA_CMP_HIDDEN = 2 * HEAD_DIM
NSA_SEL_LEN = 64
NSA_N_SELECT = 16
NSA_WINDOW = 512

XATTN_HEADS = 4

N_EXPERTS = 32
TOP_K = 4
D_FF = D_MODEL
SWIGLU_LIMIT = 7.0
SWIGLU_ALPHA = 1.702

DEEPNORM_ALPHA = (2 * DEPTH) ** 0.25

A_W = len(DIL_GROUPS) * DIL_HEADS * HEAD_DIM
B_K = GLA_HEADS * GLA_DK
B_V = GLA_HEADS * GLA_DV
C_Q = NSA_HEADS * HEAD_DIM
C_KV = NSA_KV_GROUPS * HEAD_DIM
IN_WIDTHS = (A_W, A_W, A_W,
             B_K, B_K, B_V, B_V, GLA_RANK,
             C_Q, C_KV, C_KV, C_KV, C_KV, C_KV, C_KV, NSA_HEADS * NSA_BRANCHES,
             N_BRANCHES * D_MODEL)

V7X_VMEM_LIMIT_BYTES = 48 * 1024 * 1024
V7X_LANES = 128
NEG_BIG = -1e30
BF = jnp.bfloat16


def _cparams(n_axes):
    return pltpu.CompilerParams(dimension_semantics=("arbitrary",) * n_axes,
                                vmem_limit_bytes=V7X_VMEM_LIMIT_BYTES)


def _split_bf16(x):
    hi = x.astype(BF)
    lo = (x - hi.astype(jnp.float32)).astype(BF)
    return hi, lo


def _dot_split(a, b):
    a_hi, a_lo = _split_bf16(a)
    b_hi, b_lo = _split_bf16(b)
    f = functools.partial(jnp.dot, preferred_element_type=jnp.float32)
    return f(a_hi, b_hi) + (f(a_hi, b_lo) + f(a_lo, b_hi))


def _dot_split_lhs(a, b_exact):
    a_hi, a_lo = _split_bf16(a)
    f = functools.partial(jnp.dot, preferred_element_type=jnp.float32)
    return f(a_hi, b_exact) + f(a_lo, b_exact)


def _dot_nt(a, b):
    return lax.dot_general(a, b, (((1,), (1,)), ((), ())), preferred_element_type=jnp.float32)


def _mm_kernel(x_ref, w_ref, b_ref, o_ref):
    acc = jnp.dot(x_ref[...].astype(BF), w_ref[...].astype(BF), preferred_element_type=jnp.float32)
    o_ref[...] = (acc + b_ref[...]).astype(o_ref.dtype)


def _pick_tile(n, candidates):
    for c in candidates:
        if n % c == 0:
            return c
    return n


def _mm(x, w, b=None, out_dtype=jnp.float32):
    m, k = x.shape
    n = w.shape[1]
    if b is None:
        b = jnp.zeros((n,), jnp.float32)
    tm = _pick_tile(m, (1024, 512, 256, 128, 64, 32, 16, 8))
    tn = _pick_tile(n, (1536, 1280, 1152, 1024, 768, 512, 384, 256, 128))
    return pl.pallas_call(
        _mm_kernel,
        out_shape=jax.ShapeDtypeStruct((m, n), out_dtype),
        grid=(n // tn, m // tm),
        in_specs=[
            pl.BlockSpec((tm, k), lambda j, i: (i, 0)),
            pl.BlockSpec((k, tn), lambda j, i: (0, j)),
            pl.BlockSpec((1, tn), lambda j, i: (0, j)),
        ],
        out_specs=pl.BlockSpec((tm, tn), lambda j, i: (i, j)),
        compiler_params=_cparams(2),
        name="dense_proj",
    )(x, w, b.reshape(1, n))


def _mm3(x, w, b=None):
    lead = x.shape[:-1]
    return _mm(x.reshape(-1, x.shape[-1]), w, b).reshape(*lead, w.shape[1])


GQA_TQ = 256
GQA_TK_SELECT = 512
GQA_TK_BAND = 256


def _half_select(x, want_half, have_half):
    if want_half != have_half:
        x = pltpu.roll(x, HEAD_DIM, axis=1)
    return x


def _lane_half_mask(shape, half):
    lane = lax.broadcasted_iota(jnp.int32, shape, 1)
    return (lane >= half * HEAD_DIM) & (lane < (half + 1) * HEAD_DIM)


def _make_gqa_kernel(select, max_dist, n_outer):
    r, tq, tk = NSA_GROUP_HEADS, GQA_TQ, (GQA_TK_SELECT if select else GQA_TK_BAND)

    def kern(q_ref, k_ref, v_ref, *rest):
        if select:
            m_ref, e_ref, o_ref, mx_ref, acc_ref = rest
        else:
            o_ref, mx_ref, acc_ref = rest
        i = pl.program_id(n_outer)
        dist0 = i * tq + lax.broadcasted_iota(jnp.int32, (tk, tq), 1) - lax.broadcasted_iota(jnp.int32, (tk, tq), 0)
        qs = []
        for half in range(2):
            parts = []
            for rr in range(r):
                hh = half * r + rr
                qc = q_ref[:, (hh // 2) * V7X_LANES:(hh // 2 + 1) * V7X_LANES].astype(jnp.float32) * (HEAD_DIM ** -0.5)
                qc = _half_select(qc, half, hh % 2)
                parts.append(jnp.where(_lane_half_mask(qc.shape, half), qc, 0.0).astype(BF))
            qs.append(jnp.concatenate(parts, axis=0))
        mx_ref[...] = jnp.full(mx_ref.shape, NEG_BIG, jnp.float32)
        acc_ref[...] = jnp.zeros(acc_ref.shape, jnp.float32)
        v_lane = lax.broadcasted_iota(jnp.int32, (tk, V7X_LANES), 1)

        def body(j, carry):
            k0 = pl.multiple_of(j * tk, tk)
            kt = k_ref[pl.ds(k0, tk), :].astype(BF)
            vt = v_ref[pl.ds(k0, tk), :].astype(BF)
            dist = dist0 - k0
            in_band = (dist >= 0) if select else ((dist >= 0) & (dist <= max_dist))
            for half in range(2):
                if select:
                    blk = jnp.dot(e_ref[j], m_ref[half], preferred_element_type=jnp.float32)
                    valid = (blk > 0.5) & in_band
                else:
                    valid = in_band
                bias = jnp.where(valid, 0.0, NEG_BIG)
                s = _dot_nt(kt, qs[half]) + jnp.concatenate([bias] * r, axis=1)
                m_old = mx_ref[half]
                m_new = jnp.maximum(m_old, jnp.max(s, axis=0, keepdims=True))
                p = jnp.exp((s - m_new).astype(BF))
                v_aug = jnp.where((v_lane >= half * HEAD_DIM) & (v_lane < (half + 1) * HEAD_DIM), vt, 1.0)
                pv = lax.dot_general(v_aug, p, (((0,), (0,)), ((), ())), preferred_element_type=jnp.float32)
                acc_ref[half] = jnp.exp(m_old - m_new) * acc_ref[half] + pv
                mx_ref[half] = m_new
            return carry

        lo = 0 if select else jnp.maximum(i * tq - max_dist, 0) // tk
        lax.fori_loop(lo, (i * tq + tq - 1) // tk + 1, body, 0)
        outs = []
        for half in range(2):
            acc = acc_ref[half]
            den_row = (1 - half) * HEAD_DIM
            o = (acc / acc[den_row:den_row + 1, :]).T
            for rr in range(r):
                outs.append(_half_select(o[rr * tq:(rr + 1) * tq], (half * r + rr) % 2, half))
        lane = lax.broadcasted_iota(jnp.int32, (tq, V7X_LANES), 1)
        for c in range(r):
            both = jnp.where(lane < HEAD_DIM, outs[2 * c], outs[2 * c + 1])
            o_ref[:, c * V7X_LANES:(c + 1) * V7X_LANES] = both.astype(o_ref.dtype)

    return kern


def _sel_expand_matrix(seq):
    n_blk = seq // NSA_SEL_LEN
    e = (np.arange(seq)[:, None] // NSA_SEL_LEN == np.arange(n_blk)[None, :]).astype(np.float32)
    return e.reshape(seq // GQA_TK_SELECT, GQA_TK_SELECT, n_blk)


def nsa_gqa_attention_pallas(u, q_col, k_col, v_col, selmask=None):
    bsz, seq, _ = u.shape
    select = selmask is not None
    n_pairs = NSA_KV_GROUPS // 2
    qw = 2 * NSA_GROUP_HEADS * HEAD_DIM
    qb, kb, vb = q_col // qw, k_col // V7X_LANES, v_col // V7X_LANES
    rows = NSA_GROUP_HEADS * GQA_TQ
    in_specs = [
        pl.BlockSpec((None, GQA_TQ, qw), lambda b, gp, i: (b, i, qb + gp)),
        pl.BlockSpec((None, seq, V7X_LANES), lambda b, gp, i: (b, 0, kb + gp)),
        pl.BlockSpec((None, seq, V7X_LANES), lambda b, gp, i: (b, 0, vb + gp)),
    ]
    args = [u, u, u]
    if select:
        n_blk = selmask.shape[2]
        in_specs += [pl.BlockSpec((None, 2, n_blk, GQA_TQ), lambda b, gp, i: (b, gp, 0, i)),
                     pl.BlockSpec((seq // GQA_TK_SELECT, GQA_TK_SELECT, n_blk), lambda b, gp, i: (0, 0, 0))]
        args += [selmask, jnp.asarray(_sel_expand_matrix(seq), BF)]
    return pl.pallas_call(
        _make_gqa_kernel(select, NSA_WINDOW - 1, 2),
        out_shape=jax.ShapeDtypeStruct((bsz, seq, NSA_HEADS * HEAD_DIM), BF),
        grid=(bsz, n_pairs, seq // GQA_TQ),
        in_specs=in_specs,
        out_specs=pl.BlockSpec((None, GQA_TQ, qw), lambda b, gp, i: (b, i, gp)),
        scratch_shapes=[pltpu.VMEM((2, 1, rows), jnp.float32),
                        pltpu.VMEM((2, V7X_LANES, rows), jnp.float32)],
        compiler_params=_cparams(3),
        name="nsa_selected_attention" if select else "nsa_window_attention",
    )(*args)


DIL_TILE = 128


def _make_dil_kernel(dil, max_dist, seq):
    t = DIL_TILE
    n_sub = seq // dil // t
    n_kv = min(-(-max_dist // t) + 1, n_sub)

    def rows(first, size):
        return pl.ds(first, size, stride=dil) if dil > 1 else pl.ds(first, size)

    def kern(q_ref, k_ref, v_ref, o_ref, lse_ref):
        kw = n_kv * t
        lane = lax.broadcasted_iota(jnp.int32, (t, V7X_LANES), 1)
        for rr in range(dil):
            for i in range(n_sub):
                k_tile0 = max(i + 1 - n_kv, 0)
                dist = ((i - k_tile0) * t + lax.broadcasted_iota(jnp.int32, (t, kw), 0)
                        - lax.broadcasted_iota(jnp.int32, (t, kw), 1))
                bias = jnp.where((dist >= 0) & (dist <= max_dist), 0.0, NEG_BIG)
                q_rows = rows(rr + i * t * dil, t)
                k_rows = rows(rr + k_tile0 * t * dil, kw)
                qc = q_ref[q_rows, :] * (HEAD_DIM ** -0.5)
                kt = k_ref[k_rows, :].astype(BF)
                vt = v_ref[k_rows, :].astype(BF)
                os_, ls_ = [], []
                for half in range(2):
                    qh = jnp.where(_lane_half_mask(qc.shape, half), qc, 0.0).astype(BF)
                    s = _dot_nt(qh, kt) + bias
                    m = jnp.max(s, axis=-1, keepdims=True)
                    p = jnp.exp(s - m)
                    den = jnp.sum(p, axis=-1, keepdims=True)
                    os_.append(jnp.dot(p.astype(BF), vt, preferred_element_type=jnp.float32) / den)
                    ls_.append(jnp.broadcast_to(m + jnp.log(den), (t, V7X_LANES)))
                o_ref[q_rows, :] = jnp.where(lane < HEAD_DIM, os_[0], os_[1])
                lse_ref[q_rows, :] = jnp.where(lane < HEAD_DIM, ls_[0], ls_[1])

    return kern


def dilated_attention_pallas(u, q_col, k_col, v_col):
    bsz, seq, _ = u.shape
    gw = DIL_HEADS * HEAD_DIM
    outs, lses = [], []
    sds = jax.ShapeDtypeStruct((bsz, seq, gw), jnp.float32)
    out_spec = pl.BlockSpec((None, seq, V7X_LANES), lambda b, c: (b, 0, c))
    for gi, (window, dil) in enumerate(DIL_GROUPS):

        def cmap(col):
            blk = (col + gi * gw) // V7X_LANES
            return lambda b, c: (b, 0, blk + c)

        o, lse = pl.pallas_call(
            _make_dil_kernel(dil, window // dil, seq),
            out_shape=(sds, sds),
            grid=(bsz, gw // V7X_LANES),
            in_specs=[pl.BlockSpec((None, seq, V7X_LANES), cmap(c)) for c in (q_col, k_col, v_col)],
            out_specs=(out_spec, out_spec),
            compiler_params=_cparams(2),
            name="dilated_attention",
        )(u, u, u)
        outs.append(o)
        lses.append(lse)
    return outs, lses


GLA_SUB = 16


def _gla_kernel(q_ref, k_ref, v_ref, r_ref, lr_ref, wa_ref, ba_ref, g_ref, o_ref, state_ref):
    c, dk, dv, sub = GLA_CHUNK, GLA_DK, GLA_DV, GLA_SUB

    @pl.when(pl.program_id(1) == 0)
    def _():
        state_ref[...] = jnp.zeros(state_ref.shape, jnp.float32)

    x = _dot_split(lr_ref[...], wa_ref[...]) + ba_ref[...]
    log_a = (jnp.minimum(x, 0.0) - jnp.log1p(jnp.exp(-jnp.abs(x)))) * (1.0 / GLA_TAU)
    tri = (lax.broadcasted_iota(jnp.int32, (c, c), 0) >= lax.broadcasted_iota(jnp.int32, (c, c), 1))
    la_hi, la_lo = _split_bf16(log_a)
    tri = tri.astype(BF)
    b_all = (jnp.dot(tri, la_hi, preferred_element_type=jnp.float32)
             + jnp.dot(tri, la_lo, preferred_element_type=jnp.float32))
    col = lax.broadcasted_iota(jnp.int32, (sub, c), 1)
    row = lax.broadcasted_iota(jnp.int32, (sub, c), 0)
    for h in range(GLA_HEADS):
        q = q_ref[:, h * dk:(h + 1) * dk] * (dk ** -0.5)
        k = k_ref[:, h * dk:(h + 1) * dk]
        v = v_ref[:, h * dv:(h + 1) * dv]
        b = b_all[:, h * dk:(h + 1) * dk]
        a_rows = []
        for blk in range(c // sub):
            lo = blk * sub
            q_i, b_i, k_i = q[lo:lo + sub], b[lo:lo + sub], k[lo:lo + sub]
            if blk > 0:
                b_prev = b[lo - 1:lo]
                q_t = q_i * jnp.exp(b_i - b_prev)
                k_t = k * jnp.exp(jnp.minimum(b_prev - b, 0.0))
                acc = jnp.where(col < lo, _dot_nt(q_t.astype(BF), k_t.astype(BF)), 0.0)
            else:
                acc = jnp.zeros((sub, c), jnp.float32)
            for j in range(sub):
                t = q_i * k_i[j:j + 1] * jnp.exp(jnp.minimum(b_i - b_i[j:j + 1], 0.0))
                cs = jnp.sum(t, axis=-1, keepdims=True)
                acc = acc + jnp.where((col == lo + j) & (row >= j), cs, 0.0)
            a_rows.append(acc)
        attn = jnp.concatenate(a_rows, axis=0)
        st = state_ref[h]
        o = jnp.dot(attn.astype(BF), v.astype(BF), preferred_element_type=jnp.float32)
        o = o + _dot_nt((q * jnp.exp(b)).astype(BF), st.astype(BF))
        b_last = b[c - 1:c]
        k_d = k * jnp.exp(b_last - b)
        upd = lax.dot_general(v.astype(BF), k_d.astype(BF), (((0,), (0,)), ((), ())),
                              preferred_element_type=jnp.float32)
        state_ref[h] = st * jnp.exp(b_last) + upd
        o = o * lax.rsqrt(jnp.mean(o * o, axis=-1, keepdims=True) + LN_EPS)
        r = r_ref[:, h * dv:(h + 1) * dv]
        o_ref[:, h * dv:(h + 1) * dv] = o * g_ref[...] * (r / (1.0 + jnp.exp(-r)))


def gla_mixer_pallas(u_gla, u_small, w_alpha2, b_alpha, norm_g):
    bsz, seq, _ = u_gla.shape
    c = GLA_CHUNK

    def tok(width, blk):
        return pl.BlockSpec((None, c, width), lambda b, i: (b, i, blk))

    def whole(shape):
        return pl.BlockSpec(shape, lambda b, i: (0,) * len(shape))

    wa = jnp.pad(w_alpha2, ((0, V7X_LANES - GLA_RANK), (0, 0)))
    return pl.pallas_call(
        _gla_kernel,
        out_shape=jax.ShapeDtypeStruct((bsz, seq, B_V), jnp.float32),
        grid=(bsz, seq // c),
        in_specs=[tok(B_K, 0), tok(B_K, 1), tok(B_V, 1), tok(B_V, 2), tok(V7X_LANES, 0),
                  whole((V7X_LANES, B_K)), whole((1, B_K)), whole((1, GLA_DV))],
        out_specs=tok(B_V, 0),
        scratch_shapes=[pltpu.VMEM((GLA_HEADS, GLA_DV, GLA_DK), jnp.float32)],
        compiler_params=_cparams(2),
        name="gla_mixer",
    )(u_gla, u_gla, u_gla, u_gla, u_small, wa, b_alpha.reshape(1, -1), norm_g.reshape(1, -1))


CMP_HALF = NSA_CMP_STRIDE * HEAD_DIM


def _gelu_tanh(x):
    return 0.5 * x * (1.0 + jnp.tanh(np.sqrt(2.0 / np.pi) * (x + 0.044715 * (x * x * x))))


def _compress_kernel(hk_ref, hv_ref, pek_ref, w1k_ref, w2k_ref, pev_ref, w1v_ref, w2v_ref, kc_ref, vc_ref):
    for h_ref, pe_ref, w1_ref, w2_ref, o_ref in ((hk_ref, pek_ref, w1k_ref, w2k_ref, kc_ref),
                                                 (hv_ref, pev_ref, w1v_ref, w2v_ref, vc_ref)):
        w1a = w1_ref[0].astype(BF)
        w1b = w1_ref[1].astype(BF)
        pe = pe_ref[...].astype(BF)
        pe_term = (jnp.dot(pe[:, :CMP_HALF], w1a, preferred_element_type=jnp.float32)
                   + jnp.dot(pe[:, CMP_HALF:], w1b, preferred_element_type=jnp.float32))
        w2 = w2_ref[...].astype(BF)
        for g in range(NSA_KV_GROUPS):
            hb = h_ref[g].astype(BF)
            y1 = jnp.dot(hb, w1a, preferred_element_type=jnp.float32)
            y2 = jnp.dot(hb, w1b, preferred_element_type=jnp.float32)
            n_half = y2.shape[0]
            pre = y1 + pltpu.roll(y2, n_half - 1, axis=0) + pe_term
            out = jnp.dot(_gelu_tanh(pre).astype(BF), w2, preferred_element_type=jnp.float32)
            o_ref[g] = jnp.concatenate([out, out], axis=1)


def nsa_compress_pallas(u, k_col, v_col, pe_k, w1_k, w2_k, pe_v, w1_v, w2_v):
    bsz, seq, _ = u.shape
    n_half = seq // NSA_CMP_STRIDE

    def halves(col):
        t = u[..., col:col + C_KV].reshape(bsz, n_half, NSA_CMP_STRIDE, NSA_KV_GROUPS, HEAD_DIM)
        return t.transpose(0, 3, 1, 2, 4).reshape(bsz, NSA_KV_GROUPS, n_half, CMP_HALF)

    def whole(shape):
        return pl.BlockSpec(shape, lambda b: (0,) * len(shape))

    hspec = pl.BlockSpec((None, NSA_KV_GROUPS, n_half, CMP_HALF), lambda b: (b, 0, 0, 0))
    ospec = pl.BlockSpec((None, NSA_KV_GROUPS, n_half, V7X_LANES), lambda b: (b, 0, 0, 0))
    wspecs = [whole((1, 2 * CMP_HALF)), whole((2, CMP_HALF, NSA_CMP_HIDDEN)), whole((NSA_CMP_HIDDEN, HEAD_DIM))]
    sds = jax.ShapeDtypeStruct((bsz, NSA_KV_GROUPS, n_half, V7X_LANES), jnp.float32)
    return pl.pallas_call(
        _compress_kernel,
        out_shape=(sds, sds),
        grid=(bsz,),
        in_specs=[hspec, hspec] + wspecs + wspecs,
        out_specs=(ospec, ospec),
        compiler_params=_cparams(1),
        name="nsa_compress",
    )(halves(k_col), halves(v_col),
      pe_k.reshape(1, -1), w1_k.reshape(2, CMP_HALF, NSA_CMP_HIDDEN), w2_k,
      pe_v.reshape(1, -1), w1_v.reshape(2, CMP_HALF, NSA_CMP_HIDDEN), w2_v)


CMP_TQ = 1024


def _overlap_matrix(n_cmp_pad, n_blk):
    c0 = np.arange(n_cmp_pad)[:, None] * NSA_CMP_STRIDE
    s0 = np.arange(n_blk)[None, :] * NSA_SEL_LEN
    return ((c0 < s0 + NSA_SEL_LEN) & (c0 + NSA_CMP_LEN > s0)).astype(np.float32)


def _cmp_attn_kernel(q_ref, kc_ref, vc_ref, ov_ref, o_ref, m_ref):
    i = pl.program_id(2)
    tq = CMP_TQ
    n_c = kc_ref.shape[0]
    n_blk = ov_ref.shape[0]
    kc = kc_ref[...].astype(BF)
    vc = vc_ref[...].astype(BF)
    pos = i * tq + lax.broadcasted_iota(jnp.int32, (tq, n_c), 0)
    c_end = lax.broadcasted_iota(jnp.int32, (tq, n_c), 1) * NSA_CMP_STRIDE + (NSA_CMP_LEN - 1)
    cvalid = c_end <= pos
    lane = lax.broadcasted_iota(jnp.int32, (tq, V7X_LANES), 1)
    p_sum = jnp.zeros((tq, n_c), jnp.float32)
    outs = []
    for rr in range(NSA_GROUP_HEADS):
        qc = q_ref[:, (rr // 2) * V7X_LANES:(rr // 2 + 1) * V7X_LANES].astype(jnp.float32) * (HEAD_DIM ** -0.5)
        qh = jnp.where(_lane_half_mask(qc.shape, rr % 2), qc, 0.0).astype(BF)
        s = jnp.where(cvalid, _dot_nt(qh, kc), NEG_BIG)
        m = jnp.max(s, axis=-1, keepdims=True)
        p = jnp.where(cvalid, jnp.exp(s - m), 0.0)
        p = p / jnp.maximum(jnp.sum(p, axis=-1, keepdims=True), 1e-30)
        p_sum = p_sum + p
        outs.append(jnp.dot(p.astype(BF), vc, preferred_element_type=jnp.float32))
    for c in range(NSA_GROUP_HEADS // 2):
        both = jnp.where(lane < HEAD_DIM, outs[2 * c], outs[2 * c + 1])
        o_ref[:, c * V7X_LANES:(c + 1) * V7X_LANES] = both.astype(o_ref.dtype)
    ps_hi, ps_lo = _split_bf16(p_sum)
    imp = _dot_nt(ov_ref[...], ps_hi) + _dot_nt(ov_ref[...], ps_lo)
    j = lax.broadcasted_iota(jnp.int32, (n_blk, tq), 0)
    q_blk = (i * tq + lax.broadcasted_iota(jnp.int32, (n_blk, tq), 1)) // NSA_SEL_LEN
    forced = (j == 0) | (j == q_blk) | (j == q_blk - 1)
    imp = jnp.where(forced, jnp.inf, jnp.where(j > q_blk, -jnp.inf, imp))
    rank = jnp.zeros((n_blk, tq), jnp.float32)
    for jj in range(n_blk):
        other = imp[jj:jj + 1, :]
        beats = (other > imp) | ((other == imp) & (jj < j))
        rank = rank + jnp.where(beats, 1.0, 0.0)
    m_ref[...] = jnp.where(rank < float(min(NSA_N_SELECT, n_blk)), 1.0, 0.0).astype(m_ref.dtype)


def nsa_cmp_attention_pallas(u, q_col, kc, vc):
    bsz, seq, _ = u.shape
    n_c = kc.shape[2]
    n_blk = seq // NSA_SEL_LEN
    gw = NSA_GROUP_HEADS * HEAD_DIM
    qb = q_col // gw
    cspec = pl.BlockSpec((None, None, n_c, V7X_LANES), lambda b, g, i: (b, g, 0, 0))
    return pl.pallas_call(
        _cmp_attn_kernel,
        out_shape=(jax.ShapeDtypeStruct((bsz, seq, NSA_HEADS * HEAD_DIM), BF),
                   jax.ShapeDtypeStruct((bsz, NSA_KV_GROUPS, n_blk, seq), BF)),
        grid=(bsz, NSA_KV_GROUPS, seq // CMP_TQ),
        in_specs=[pl.BlockSpec((None, CMP_TQ, gw), lambda b, g, i: (b, i, qb + g)),
                  cspec, cspec,
                  pl.BlockSpec((n_blk, n_c), lambda b, g, i: (0, 0))],
        out_specs=(pl.BlockSpec((None, CMP_TQ, gw), lambda b, g, i: (b, i, g)),
                   pl.BlockSpec((None, None, n_blk, CMP_TQ), lambda b, g, i: (b, g, 0, i))),
        compiler_params=_cparams(3),
        name="nsa_cmp_attention",
    )(u, kc, vc, jnp.asarray(_overlap_matrix(n_c, n_blk).T, BF))


MERGE_TM = 256


def _layer_norm_rows(x, g, b):
    mu = jnp.mean(x, axis=-1, keepdims=True)
    xc = x - mu
    var = jnp.mean(xc * xc, axis=-1, keepdims=True)
    return xc * lax.rsqrt(var + LN_EPS) * g + b


def _gate_expand_matrices():
    e = np.zeros((NSA_BRANCHES, V7X_LANES, NSA_HEADS * HEAD_DIM), np.float32)
    for br in range(NSA_BRANCHES):
        for h in range(NSA_HEADS):
            e[br, GLA_RANK + NSA_BRANCHES * h + br, h * HEAD_DIM:(h + 1) * HEAD_DIM] = 1.0
    return e


def _sigmoid(x):
    return 1.0 / (1.0 + jnp.exp(-x))


def _merge_kernel(x_ref, o0_ref, o1_ref, o2_ref, l0_ref, l1_ref, l2_ref, yb_ref, oc_ref, os_ref, ow_ref,
                  us_ref, mg_ref, ex_ref, wa_ref, wb_ref, wc_ref, wo_ref, g_ref, b_ref, out_ref):
    l0, l1, l2 = l0_ref[...], l1_ref[...], l2_ref[...]
    lm = jnp.maximum(jnp.maximum(l0, l1), l2)
    e0, e1, e2 = jnp.exp(l0 - lm), jnp.exp(l1 - lm), jnp.exp(l2 - lm)
    y_a = (e0 * o0_ref[...] + e1 * o1_ref[...] + e2 * o2_ref[...]) / (e0 + e1 + e2)
    sg = _sigmoid(us_ref[...])
    f32 = jnp.float32
    y_c = (_dot_split_lhs(sg, ex_ref[0]) * oc_ref[...].astype(f32) + _dot_split_lhs(sg, ex_ref[1]) * os_ref[...].astype(f32)
           + _dot_split_lhs(sg, ex_ref[2]) * ow_ref[...].astype(f32))
    d = D_MODEL
    f = functools.partial(jnp.dot, preferred_element_type=jnp.float32)
    merged = (_sigmoid(mg_ref[:, :d].astype(f32)) * f(y_a.astype(BF), wa_ref[...])
              + _sigmoid(mg_ref[:, d:2 * d].astype(f32)) * f(yb_ref[...].astype(BF), wb_ref[...])
              + _sigmoid(mg_ref[:, 2 * d:].astype(f32)) * f(y_c.astype(BF), wc_ref[...]))
    mix = f(merged.astype(BF), wo_ref[...])
    out_ref[...] = _layer_norm_rows(DEEPNORM_ALPHA * x_ref[...] + mix, g_ref[...], b_ref[...])


def mixer_merge_pallas(x, dil_o, dil_lse, y_b, o_cmp, o_sel, o_win, u_small, m_g, w_br_a, w_br_b, w_br_c, w_o_mix, ln_g, ln_b):
    n, d = x.shape
    tm = MERGE_TM
    aw = DIL_HEADS * HEAD_DIM

    def tok(width):
        return pl.BlockSpec((tm, width), lambda i: (i, 0))

    def whole(shape):
        return pl.BlockSpec(shape, lambda i: (0,) * len(shape))

    ex = jnp.asarray(_gate_expand_matrices(), BF)
    return pl.pallas_call(
        _merge_kernel,
        out_shape=jax.ShapeDtypeStruct((n, d), jnp.float32),
        grid=(n // tm,),
        in_specs=[tok(d)] + [tok(aw)] * 6 + [tok(B_V)] + [tok(C_Q)] * 3 + [tok(V7X_LANES), tok(N_BRANCHES * d),
                  whole(ex.shape), whole((aw, d)), whole((B_V, d)), whole((C_Q, d)), whole((d, d)),
                  whole((1, d)), whole((1, d))],
        out_specs=tok(d),
        compiler_params=_cparams(1),
        name="mixer_merge",
    )(x, *dil_o, *dil_lse, y_b, o_cmp, o_sel, o_win, u_small, m_g, ex,
      w_br_a.astype(BF), w_br_b.astype(BF), w_br_c.astype(BF), w_o_mix.astype(BF),
      ln_g.reshape(1, d), ln_b.reshape(1, d))


XATTN_TM = 256


def _xattn_kernel(x_ref, k_ref, v_ref, wq_ref, wo_ref, g_ref, b_ref, o_ref):
    x = x_ref[...]
    d = x.shape[-1]
    hd = d // XATTN_HEADS
    q = jnp.dot(x.astype(BF), wq_ref[...], preferred_element_type=jnp.float32)
    outs = []
    for h in range(XATTN_HEADS):
        sl = slice(h * hd, (h + 1) * hd)
        s = _dot_nt((q[:, sl] * (hd ** -0.5)).astype(BF), k_ref[:, sl].astype(BF))
        p = jnp.exp(s - jnp.max(s, axis=-1, keepdims=True))
        den = jnp.sum(p, axis=-1, keepdims=True)
        outs.append(jnp.dot(p.astype(BF), v_ref[:, sl].astype(BF), preferred_element_type=jnp.float32) / den)
    o = jnp.concatenate(outs, axis=1)
    xa = jnp.dot(o.astype(BF), wo_ref[...], preferred_element_type=jnp.float32)
    o_ref[...] = _layer_norm_rows(DEEPNORM_ALPHA * x + xa, g_ref[...], b_ref[...])


def cross_attention_block_pallas(x, mem, w_xq, w_xk, w_xv, w_xo, ln_g, ln_b):
    bsz, seq, d = x.shape
    mem_len = mem.shape[1]
    mem2 = mem.reshape(-1, d)
    k = _mm(mem2, w_xk)
    v = _mm(mem2, w_xv)
    n_t = seq // XATTN_TM

    def whole(shape):
        return pl.BlockSpec(shape, lambda b, i: (0,) * len(shape))

    out = pl.pallas_call(
        _xattn_kernel,
        out_shape=jax.ShapeDtypeStruct((bsz * seq, d), jnp.float32),
        grid=(bsz, n_t),
        in_specs=[pl.BlockSpec((XATTN_TM, d), lambda b, i: (b * n_t + i, 0)),
                  pl.BlockSpec((mem_len, d), lambda b, i: (b, 0)),
                  pl.BlockSpec((mem_len, d), lambda b, i: (b, 0)),
                  whole((d, d)), whole((d, d)), whole((1, d)), whole((1, d))],
        out_specs=pl.BlockSpec((XATTN_TM, d), lambda b, i: (b * n_t + i, 0)),
        compiler_params=_cparams(2),
        name="cross_attention_block",
    )(x.reshape(-1, d), k, v, w_xq.astype(BF), w_xo.astype(BF), ln_g.reshape(1, d), ln_b.reshape(1, d))
    return out.reshape(bsz, seq, d)


def _ln_kernel(x_ref, g_ref, b_ref, o_ref, obf_ref):
    y = _layer_norm_rows(x_ref[...], g_ref[...], b_ref[...])
    o_ref[...] = y
    obf_ref[...] = y.astype(BF)


def layer_norm_pallas(x, g, b):
    n, d = x.shape
    tm = _pick_tile(n, (1024, 512, 256, 128, 64, 32, 16, 8))
    spec = pl.BlockSpec((tm, d), lambda i: (i, 0))
    return pl.pallas_call(
        _ln_kernel,
        out_shape=(jax.ShapeDtypeStruct((n, d), jnp.float32), jax.ShapeDtypeStruct((n, d), BF)),
        grid=(n // tm,),
        in_specs=[spec, pl.BlockSpec((1, d), lambda i: (0, 0)), pl.BlockSpec((1, d), lambda i: (0, 0))],
        out_specs=(spec, spec),
        compiler_params=_cparams(1),
        name="layer_norm",
    )(x, g.reshape(1, d), b.reshape(1, d))


MOE_TB = 256


def _moe_ffn_kernel(layer, be_ref, nxt_ref, nused_ref, x_ref, wgu_hbm, bgu_ref, wd_hbm, bd_ref, o_ref,
                    wgu_st, wd_st, wgu_bf, wd_bf, sem):
    i = pl.program_id(0)
    e = be_ref[i]
    changed = jnp.logical_or(i == 0, e != be_ref[jnp.maximum(i - 1, 0)])

    def fetch(expert):
        return (pltpu.make_async_copy(wgu_hbm.at[layer, expert], wgu_st, sem.at[0]),
                pltpu.make_async_copy(wd_hbm.at[layer, expert], wd_st, sem.at[1]))

    @pl.when(i == 0)
    def _():
        for cp in fetch(e):
            cp.start()

    @pl.when(changed)
    def _():
        for cp in fetch(e):
            cp.wait()
        wgu_bf[...] = wgu_st[...].astype(BF)
        wd_bf[...] = wd_st[...].astype(BF)

        @pl.when(nxt_ref[i] != e)
        def _():
            for cp in fetch(nxt_ref[i]):
                cp.start()

    @pl.when(i < nused_ref[0])
    def _():
        gu = jnp.dot(x_ref[...].astype(BF), wgu_bf[...], preferred_element_type=jnp.float32) + bgu_ref[...]
        gate = jnp.minimum(gu[:, :D_FF], SWIGLU_LIMIT)
        up = jnp.clip(gu[:, D_FF:], -SWIGLU_LIMIT, SWIGLU_LIMIT)
        act = (up + 1.0) * gate * (1.0 / (1.0 + jnp.exp(-SWIGLU_ALPHA * gate)))
        o_ref[...] = jnp.dot(act.astype(BF), wd_bf[...], preferred_element_type=jnp.float32) + bd_ref[...]

    @pl.when(i >= nused_ref[0])
    def _():
        o_ref[...] = jnp.zeros(o_ref.shape, o_ref.dtype)


def moe_expert_ffn_pallas(buf, block_e, next_e, n_used, layer, w_gu, b_gu, w_down, b_down):
    n_rows, d = buf.shape
    depth, n_e = w_gu.shape[:2]
    grid_spec = pltpu.PrefetchScalarGridSpec(
        num_scalar_prefetch=3,
        grid=(n_rows // MOE_TB,),
        in_specs=[
            pl.BlockSpec((MOE_TB, d), lambda i, be, nx, nu: (i, 0)),
            pl.BlockSpec(memory_space=pl.ANY),
            pl.BlockSpec((None, None, 1, 2 * D_FF), lambda i, be, nx, nu: (layer, be[i], 0, 0)),
            pl.BlockSpec(memory_space=pl.ANY),
            pl.BlockSpec((None, None, 1, d), lambda i, be, nx, nu: (layer, be[i], 0, 0)),
        ],
        out_specs=pl.BlockSpec((MOE_TB, d), lambda i, be, nx, nu: (i, 0)),
        scratch_shapes=[pltpu.VMEM((d, 2 * D_FF), jnp.float32), pltpu.VMEM((D_FF, d), jnp.float32),
                        pltpu.VMEM((d, 2 * D_FF), BF), pltpu.VMEM((D_FF, d), BF),
                        pltpu.SemaphoreType.DMA((2,))],
    )
    return pl.pallas_call(
        functools.partial(_moe_ffn_kernel, layer),
        out_shape=jax.ShapeDtypeStruct((n_rows, d), jnp.float32),
        grid_spec=grid_spec,
        compiler_params=_cparams(1),
        name="moe_expert_ffn",
    )(block_e, next_e, n_used, buf, w_gu, b_gu.reshape(depth, n_e, 1, -1), w_down, b_down.reshape(depth, n_e, 1, -1))


MOE_ROUTE_TM = 256
MOE_RUN_ALIGN = 8


def _router_kernel(x_ref, wr_ref, br_ref, route_ref, counts_ref, tile_n_ref, tile_c_ref, carry_ref):
    tm = MOE_ROUTE_TM

    @pl.when(pl.program_id(0) == 0)
    def _():
        carry_ref[...] = jnp.zeros(carry_ref.shape, jnp.float32)

    logits = _dot_split(x_ref[...], wr_ref[...]) + br_ref[...]
    lane = lax.broadcasted_iota(jnp.int32, (tm, N_EXPERTS), 1).astype(jnp.float32)
    onehots, vals, idxs = [], [], []
    rest = logits
    for _ in range(TOP_K):
        m = jnp.max(rest, axis=-1, keepdims=True)
        idx = jnp.min(jnp.where(rest == m, lane, float(N_EXPERTS)), axis=-1, keepdims=True)
        hit = lane == idx
        onehots.append(hit)
        vals.append(m)
        idxs.append(idx)
        rest = jnp.where(hit, -jnp.inf, rest)
    exps = [jnp.exp(v - vals[0]) for v in vals]
    den = exps[0] + exps[1] + exps[2] + exps[3]
    cnt = jnp.zeros((tm, N_EXPERTS), jnp.float32)
    for hit in onehots:
        cnt = cnt + jnp.where(hit, 1.0, 0.0)
    earlier = (lax.broadcasted_iota(jnp.int32, (tm, tm), 1) < lax.broadcasted_iota(jnp.int32, (tm, tm), 0))
    local_before = jnp.dot(earlier.astype(BF), cnt.astype(BF), preferred_element_type=jnp.float32)
    carry = carry_ref[...]
    tile_n = jnp.sum(cnt, axis=0, keepdims=True)
    tile_n = jnp.floor((tile_n + (MOE_RUN_ALIGN - 1.0)) * (1.0 / MOE_RUN_ALIGN)) * MOE_RUN_ALIGN
    lower = (lax.broadcasted_iota(jnp.int32, (N_EXPERTS, N_EXPERTS), 0)
             < lax.broadcasted_iota(jnp.int32, (N_EXPERTS, N_EXPERTS), 1))
    tile_off = jnp.dot(jnp.broadcast_to(tile_n, (8, N_EXPERTS)).astype(BF), lower.astype(BF),
                       preferred_element_type=jnp.float32)[:1]
    lane_out = lax.broadcasted_iota(jnp.int32, (tm, V7X_LANES), 1)
    out = jnp.zeros((tm, V7X_LANES), jnp.float32)
    for k in range(TOP_K):
        pos = jnp.sum(jnp.where(onehots[k], local_before + carry, 0.0), axis=-1, keepdims=True)
        rank = jnp.sum(jnp.where(onehots[k], local_before + tile_off, 0.0), axis=-1, keepdims=True)
        out = jnp.where(lane_out == k, idxs[k], out)
        out = jnp.where(lane_out == TOP_K + k, exps[k] / den, out)
        out = jnp.where(lane_out == 2 * TOP_K + k, pos, out)
        out = jnp.where(lane_out == 3 * TOP_K + k, rank, out)
    route_ref[...] = out
    tile_n_ref[...] = tile_n
    tile_c_ref[...] = carry
    carry_ref[...] = carry + tile_n
    counts_ref[...] = carry + tile_n


def moe_router_pallas(x, w_router, b_router):
    n, d = x.shape
    tm = MOE_ROUTE_TM
    return pl.pallas_call(
        _router_kernel,
        out_shape=(jax.ShapeDtypeStruct((n, V7X_LANES), jnp.float32),
                   jax.ShapeDtypeStruct((1, N_EXPERTS), jnp.float32),
                   jax.ShapeDtypeStruct((n // tm, 1, N_EXPERTS), jnp.float32),
                   jax.ShapeDtypeStruct((n // tm, 1, N_EXPERTS), jnp.float32)),
        grid=(n // tm,),
        in_specs=[pl.BlockSpec((tm, d), lambda i: (i, 0)),
                  pl.BlockSpec((d, N_EXPERTS), lambda i: (0, 0)),
                  pl.BlockSpec((1, N_EXPERTS), lambda i: (0, 0))],
        out_specs=(pl.BlockSpec((tm, V7X_LANES), lambda i: (i, 0)),
                   pl.BlockSpec((1, N_EXPERTS), lambda i: (0, 0)),
                   pl.BlockSpec((None, 1, N_EXPERTS), lambda i: (i, 0, 0)),
                   pl.BlockSpec((None, 1, N_EXPERTS), lambda i: (i, 0, 0))),
        scratch_shapes=[pltpu.VMEM((1, N_EXPERTS), jnp.float32)],
        compiler_params=_cparams(1),
        name="moe_router",
    )(x, w_router, b_router.reshape(1, -1))


MOE_RUN_BITS = tuple(range(MOE_ROUTE_TM.bit_length() - 1, MOE_RUN_ALIGN.bit_length() - 2, -1))
MOE_SORT_ROWS = TOP_K * MOE_ROUTE_TM + N_EXPERTS * MOE_RUN_ALIGN
MOE_RUNS_PER_TILE = N_EXPERTS + 1


def _for_each_run_chunk(tile, n_ref, sorted_off_ref, buf_off_ref, make_copy):
    for e in range(MOE_RUNS_PER_TILE):
        n = n_ref[tile * MOE_RUNS_PER_TILE + e]
        s_row = sorted_off_ref[tile * MOE_RUNS_PER_TILE + e]
        b_row = buf_off_ref[tile * MOE_RUNS_PER_TILE + e]
        for bit in MOE_RUN_BITS:
            size = 1 << bit
            hit = (n & size) != 0

            @pl.when(hit)
            def _(s_row=s_row, b_row=b_row, size=size):
                make_copy(pl.multiple_of(s_row, MOE_RUN_ALIGN), pl.multiple_of(b_row, MOE_RUN_ALIGN), size).start()

            step = jnp.where(hit, size, 0)
            s_row = s_row + step
            b_row = b_row + step


def _rank_onehot(route_ref, k):
    tm = MOE_ROUTE_TM
    ranks = lax.broadcasted_iota(jnp.int32, (tm, MOE_SORT_ROWS), 1).astype(jnp.float32)
    return ranks == route_ref[:, 3 * TOP_K + k:3 * TOP_K + k + 1]


def _dispatch_kernel(zrow_ref, n_ref, soff_ref, boff_ref, x_ref, route_ref, buf_ref, zero_ref, sorted_ref, sem, zsem):
    tm = MOE_ROUTE_TM
    tile = pl.program_id(0)

    @pl.when(pl.program_id(0) == 0)
    def _():
        zero_ref[...] = jnp.zeros(zero_ref.shape, jnp.float32)
        for j in range(zrow_ref.shape[0]):
            @pl.when(zrow_ref[j] >= 0)
            def _():
                row0 = pl.multiple_of(zrow_ref[j], MOE_TB)
                pltpu.make_async_copy(zero_ref, buf_ref.at[pl.ds(row0, MOE_TB)], zsem).start()
        for j in range(zrow_ref.shape[0]):
            @pl.when(zrow_ref[j] >= 0)
            def _():
                pltpu.make_async_copy(zero_ref, buf_ref.at[pl.ds(0, MOE_TB)], zsem).wait()

    slot = tile % 2
    last = pl.num_programs(0) - 1

    def drain(s):
        pltpu.make_async_copy(sorted_ref.at[s], buf_ref.at[pl.ds(0, MOE_SORT_ROWS)], sem.at[s]).wait()

    @pl.when(tile >= 2)
    def _():
        drain(slot)

    perm = _rank_onehot(route_ref, 0)
    for k in range(1, TOP_K):
        perm = perm | _rank_onehot(route_ref, k)
    sorted_ref[slot] = lax.dot_general(jnp.where(perm, 1.0, 0.0).astype(BF), x_ref[...].astype(BF), (((0,), (0,)), ((), ())),
                                       preferred_element_type=jnp.float32)
    _for_each_run_chunk(tile, n_ref, soff_ref, boff_ref, lambda s_row, b_row, size: pltpu.make_async_copy(
        sorted_ref.at[slot, pl.ds(s_row, size)], buf_ref.at[pl.ds(b_row, size)], sem.at[slot]))

    @pl.when(tile == last)
    def _():
        @pl.when(tile >= 1)
        def _():
            drain(1 - slot)
        drain(slot)


def _run_tables(tile_n, tile_c, pstart, trash_row):
    n_t = tile_n.reshape(-1, N_EXPERTS).astype(jnp.int32)
    c_t = tile_c.reshape(-1, N_EXPERTS).astype(jnp.int32)
    ends = jnp.cumsum(n_t, axis=1)
    total = ends[:, -1:]
    n_all = jnp.concatenate([n_t, MOE_SORT_ROWS - total], axis=1)
    sorted_off = jnp.concatenate([ends - n_t, total], axis=1)
    parity = (jnp.arange(n_t.shape[0], dtype=jnp.int32) % 2)[:, None]
    buf_off = jnp.concatenate([pstart[None, :] + c_t, trash_row - parity * MOE_TB], axis=1)
    return n_all.reshape(-1), sorted_off.reshape(-1), buf_off.reshape(-1)


def moe_dispatch_pallas(x, route, runs, zero_rows, n_rows):
    n, d = x.shape
    tm = MOE_ROUTE_TM
    grid_spec = pltpu.PrefetchScalarGridSpec(
        num_scalar_prefetch=4,
        grid=(n // tm,),
        in_specs=[pl.BlockSpec((tm, d), lambda i, *_: (i, 0)),
                  pl.BlockSpec((tm, V7X_LANES), lambda i, *_: (i, 0))],
        out_specs=pl.BlockSpec(memory_space=pl.ANY),
        scratch_shapes=[pltpu.VMEM((MOE_TB, d), jnp.float32), pltpu.VMEM((2, MOE_SORT_ROWS, d), jnp.float32),
                        pltpu.SemaphoreType.DMA((2,)), pltpu.SemaphoreType.DMA],
    )
    return pl.pallas_call(
        _dispatch_kernel,
        out_shape=jax.ShapeDtypeStruct((n_rows, d), jnp.float32),
        grid_spec=grid_spec,
        compiler_params=_cparams(1),
        name="moe_dispatch",
    )(zero_rows, *runs, x, route)


def _combine_kernel(n_ref, soff_ref, boff_ref, x_ref, route_ref, obuf_ref, g_ref, b_ref, o_ref, obf_ref, sorted_ref, sem):
    tm = MOE_ROUTE_TM
    tile = pl.program_id(0)
    slot = tile % 2

    def gather(t, s):
        _for_each_run_chunk(t, n_ref, soff_ref, boff_ref, lambda s_row, b_row, size: pltpu.make_async_copy(
            obuf_ref.at[pl.ds(b_row, size)], sorted_ref.at[s, pl.ds(s_row, size)], sem.at[s]))

    @pl.when(tile == 0)
    def _():
        gather(0, 0)

    @pl.when(tile + 1 < pl.num_programs(0))
    def _():
        gather(tile + 1, 1 - slot)

    gperm = jnp.zeros((tm, MOE_SORT_ROWS), jnp.float32)
    for k in range(TOP_K):
        gperm = jnp.where(_rank_onehot(route_ref, k), route_ref[:, TOP_K + k:TOP_K + k + 1], gperm)
    pltpu.make_async_copy(obuf_ref.at[pl.ds(0, MOE_SORT_ROWS)], sorted_ref.at[slot], sem.at[slot]).wait()
    ff = _dot_split(gperm, sorted_ref[slot])
    y = _layer_norm_rows(DEEPNORM_ALPHA * x_ref[...] + ff, g_ref[...], b_ref[...])
    o_ref[...] = y
    obf_ref[...] = y.astype(BF)


def moe_combine_pallas(x, route, runs, out_buf, ln_g, ln_b):
    n, d = x.shape
    tm = MOE_ROUTE_TM
    spec = pl.BlockSpec((tm, d), lambda i, *_: (i, 0))
    grid_spec = pltpu.PrefetchScalarGridSpec(
        num_scalar_prefetch=3,
        grid=(n // tm,),
        in_specs=[spec,
                  pl.BlockSpec((tm, V7X_LANES), lambda i, *_: (i, 0)),
                  pl.BlockSpec(memory_space=pl.ANY),
                  pl.BlockSpec((1, d), lambda i, *_: (0, 0)), pl.BlockSpec((1, d), lambda i, *_: (0, 0))],
        out_specs=(spec, spec),
        scratch_shapes=[pltpu.VMEM((2, MOE_SORT_ROWS, d), jnp.float32), pltpu.SemaphoreType.DMA((2,))],
    )
    return pl.pallas_call(
        _combine_kernel,
        out_shape=(jax.ShapeDtypeStruct((n, d), jnp.float32), jax.ShapeDtypeStruct((n, d), BF)),
        grid_spec=grid_spec,
        compiler_params=_cparams(1),
        name="moe_combine",
    )(*runs, x, route, out_buf, ln_g.reshape(1, d), ln_b.reshape(1, d))


def token_mixer_block(h, h_bf, w_in, b_in, w_alpha2, b_alpha, gla_norm_g, cmp_pe_k, cmp_w1_k, cmp_w2_k,
                      cmp_pe_v, cmp_w1_v, cmp_w2_v, w_br_a, w_br_b, w_br_c, w_o_mix, ln_g, ln_b):
    bsz, seq, d = h.shape
    offs = np.concatenate([[0], np.cumsum(IN_WIDTHS)]).tolist()
    h2 = h.reshape(-1, d)

    def proj(lo, hi, out_dtype=jnp.float32):
        return _mm(h_bf, w_in[:, offs[lo]:offs[hi]], b_in[offs[lo]:offs[hi]], out_dtype).reshape(bsz, seq, -1)

    u_dil = proj(0, 3)
    u_gla = proj(3, 7)
    w_small = jnp.concatenate([w_in[:, offs[7]:offs[8]], w_in[:, offs[15]:offs[16]]], axis=1)
    b_small = jnp.concatenate([b_in[offs[7]:offs[8]], b_in[offs[15]:offs[16]]])
    n_small = w_small.shape[1]
    w_small = jnp.pad(w_small, ((0, 0), (0, V7X_LANES - n_small)))
    b_small = jnp.pad(b_small, (0, V7X_LANES - n_small))
    u_small = _mm(h_bf, w_small, b_small)
    u_c = proj(8, 15, BF)
    m_g = _mm(h_bf, w_in[:, offs[16]:offs[17]], b_in[offs[16]:offs[17]], BF)
    dil_o, dil_lse = dilated_attention_pallas(u_dil, 0, A_W, 2 * A_W)
    y_b = gla_mixer_pallas(u_gla, u_small.reshape(bsz, seq, V7X_LANES), w_alpha2, b_alpha, gla_norm_g)
    kc, vc = nsa_compress_pallas(u_c, C_Q, C_Q + C_KV, cmp_pe_k, cmp_w1_k, cmp_w2_k, cmp_pe_v, cmp_w1_v, cmp_w2_v)
    o_cmp, selmask = nsa_cmp_attention_pallas(u_c, 0, kc, vc)
    o_sel = nsa_gqa_attention_pallas(u_c, 0, C_Q + 2 * C_KV, C_Q + 3 * C_KV, selmask)
    o_win = nsa_gqa_attention_pallas(u_c, 0, C_Q + 4 * C_KV, C_Q + 5 * C_KV)

    def flat(t):
        return t.reshape(bsz * seq, -1)

    out = mixer_merge_pallas(h2, [flat(t) for t in dil_o], [flat(t) for t in dil_lse], flat(y_b), flat(o_cmp),
                             flat(o_sel), flat(o_win), u_small, m_g, w_br_a, w_br_b, w_br_c, w_o_mix, ln_g, ln_b)
    return out.reshape(bsz, seq, d)


def moe_block(x, layer, w_router, b_router, w_gu, b_gu, w_down, b_down, ln_g, ln_b):
    n, d = x.shape
    route, counts, tile_n, tile_c = moe_router_pallas(x, w_router, b_router)
    ids = jnp.arange(N_EXPERTS, dtype=jnp.int32)
    counts = counts.reshape(-1).astype(jnp.int32)
    padded = (counts + MOE_TB - 1) // MOE_TB * MOE_TB
    pend = jnp.cumsum(padded)
    pstart = pend - padded
    n_rows = n * TOP_K + (n // MOE_ROUTE_TM) * N_EXPERTS * MOE_RUN_ALIGN + (N_EXPERTS + 2) * MOE_TB
    runs = _run_tables(tile_n, tile_c, pstart, n_rows - MOE_TB)
    n_blocks = n_rows // MOE_TB
    n_used = pend[-1:] // MOE_TB
    blk = jnp.minimum(jnp.arange(n_blocks, dtype=jnp.int32), n_used - 1) * MOE_TB
    block_e = jnp.minimum(jnp.sum((pend[None, :] <= blk[:, None]).astype(jnp.int32), axis=1), N_EXPERTS - 1)
    later_nonempty = (ids[None, :] > ids[:, None]) & (counts[None, :] > 0)
    next_of = jnp.min(jnp.where(later_nonempty, ids[None, :], N_EXPERTS), axis=1)
    next_of = jnp.where(next_of < N_EXPERTS, next_of, ids)
    next_e = jnp.sum(jnp.where(block_e[:, None] == ids, next_of, 0), axis=1)
    tail = pend[-1] + jnp.arange((n_rows - n * TOP_K) // MOE_TB, dtype=jnp.int32) * MOE_TB
    zero_rows = jnp.concatenate([jnp.where(counts > 0, pend - MOE_TB, -1), jnp.where(tail < n_rows, tail, -1)])
    buf = moe_dispatch_pallas(x, route, runs, zero_rows, n_rows)
    out = moe_expert_ffn_pallas(buf, block_e, next_e, n_used, layer, w_gu, b_gu, w_down, b_down)
    return moe_combine_pallas(x, route, runs, out, ln_g, ln_b)


def kernel(x, mem, ln0_g, ln0_b, w_in, b_in, w_alpha2, b_alpha, gla_norm_g, cmp_pe_k, cmp_w1_k, cmp_w2_k, cmp_pe_v, cmp_w1_v, cmp_w2_v, w_br_a, w_br_b, w_br_c, w_o_mix, ln1_g, ln1_b, w_xq, w_xk, w_xv, w_xo, ln2_g, ln2_b, w_router, b_router, w_gu, b_gu, w_down, b_down, ln3_g, ln3_b):
    shape = x.shape
    x, x_bf = layer_norm_pallas(x.reshape(-1, shape[-1]), ln0_g, ln0_b)
    for li in range(DEPTH):
        x = token_mixer_block(x.reshape(shape), x_bf, w_in[li], b_in[li], w_alpha2[li], b_alpha[li], gla_norm_g[li],
                              cmp_pe_k[li], cmp_w1_k[li], cmp_w2_k[li], cmp_pe_v[li], cmp_w1_v[li], cmp_w2_v[li],
                              w_br_a[li], w_br_b[li], w_br_c[li], w_o_mix[li], ln1_g[li], ln1_b[li])
        x = cross_attention_block_pallas(x, mem, w_xq[li], w_xk[li], w_xv[li], w_xo[li], ln2_g[li], ln2_b[li])
        x, x_bf = moe_block(x.reshape(-1, shape[-1]), li, w_router[li], b_router[li], w_gu, b_gu, w_down, b_down,
                            ln3_g[li], ln3_b[li])
    return x.reshape(shape)
```

```python
import functools

import jax
import jax.numpy as jnp
import numpy as np
from jax import lax
from jax.experimental import pallas as pl
from jax.experimental.pallas import tpu as pltpu

D_MODEL = 1024
DEPTH = 2
HEAD_DIM = 64
N_BRANCHES = 3
LN_EPS = 1e-5

DIL_GROUPS = ((128, 1), (512, 4), (2048, 16))
DIL_HEADS = 4

GLA_HEADS = 4
GLA_DK = D_MODEL // 2 // GLA_HEADS
GLA_DV = D_MODEL // GLA_HEADS
GLA_RANK = 16
GLA_TAU = 16.0
GLA_CHUNK = 64

NSA_HEADS = 16
NSA_KV_GROUPS = 4
NSA_GROUP_HEADS = NSA_HEADS // NSA_KV_GROUPS
NSA_BRANCHES = 3
NSA_CMP_LEN = 32
NSA_CMP_STRIDE = 16
NSA_CMP_HIDDEN = 2 * HEAD_DIM
NSA_SEL_LEN = 64
NSA_N_SELECT = 16
NSA_WINDOW = 512

XATTN_HEADS = 4

N_EXPERTS = 32
TOP_K = 4
D_FF = D_MODEL
SWIGLU_LIMIT = 7.0
SWIGLU_ALPHA = 1.702

DEEPNORM_ALPHA = (2 * DEPTH) ** 0.25

A_W = len(DIL_GROUPS) * DIL_HEADS * HEAD_DIM
B_K = GLA_HEADS * GLA_DK
B_V = GLA_HEADS * GLA_DV
C_Q = NSA_HEADS * HEAD_DIM
C_KV = NSA_KV_GROUPS * HEAD_DIM
IN_WIDTHS = (A_W, A_W, A_W,
             B_K, B_K, B_V, B_V, GLA_RANK,
             C_Q, C_KV, C_KV, C_KV, C_KV, C_KV, C_KV, NSA_HEADS * NSA_BRANCHES,
             N_BRANCHES * D_MODEL)

V7X_VMEM_LIMIT_BYTES = 48 * 1024 * 1024
V7X_LANES = 128
NEG_BIG = -1e30
BF = jnp.bfloat16


def _cparams(n_axes):
    return pltpu.CompilerParams(dimension_semantics=("arbitrary",) * n_axes,
                                vmem_limit_bytes=V7X_VMEM_LIMIT_BYTES)


def _split_bf16(x):
    hi = x.astype(BF)
    lo = (x - hi.astype(jnp.float32)).astype(BF)
    return hi, lo


def _dot_split(a, b):
    a_hi, a_lo = _split_bf16(a)
    b_hi, b_lo = _split_bf16(b)
    f = functools.partial(jnp.dot, preferred_element_type=jnp.float32)
    return f(a_hi, b_hi) + (f(a_hi, b_lo) + f(a_lo, b_hi))


def _dot_split_lhs(a, b_exact):
    a_hi, a_lo = _split_bf16(a)
    f = functools.partial(jnp.dot, preferred_element_type=jnp.float32)
    return f(a_hi, b_exact) + f(a_lo, b_exact)


def _dot_nt(a, b):
    return lax.dot_general(a, b, (((1,), (1,)), ((), ())), preferred_element_type=jnp.float32)


def _mm_kernel(x_ref, w_ref, b_ref, o_ref):
    acc = jnp.dot(x_ref[...].astype(BF), w_ref[...].astype(BF), preferred_element_type=jnp.float32)
    o_ref[...] = (acc + b_ref[...]).astype(o_ref.dtype)


def _pick_tile(n, candidates):
    for c in candidates:
        if n % c == 0:
            return c
    return n


def _mm(x, w, b=None, out_dtype=jnp.float32):
    m, k = x.shape
    n = w.shape[1]
    if b is None:
        b = jnp.zeros((n,), jnp.float32)
    tm = _pick_tile(m, (1024, 512, 256, 128, 64, 32, 16, 8))
    tn = _pick_tile(n, (1536, 1280, 1152, 1024, 768, 512, 384, 256, 128))
    return pl.pallas_call(
        _mm_kernel,
        out_shape=jax.ShapeDtypeStruct((m, n), out_dtype),
        grid=(n // tn, m // tm),
        in_specs=[
            pl.BlockSpec((tm, k), lambda j, i: (i, 0)),
            pl.BlockSpec((k, tn), lambda j, i: (0, j)),
            pl.BlockSpec((1, tn), lambda j, i: (0, j)),
        ],
        out_specs=pl.BlockSpec((tm, tn), lambda j, i: (i, j)),
        compiler_params=_cparams(2),
        name="dense_proj",
    )(x, w, b.reshape(1, n))


def _mm3(x, w, b=None):
    lead = x.shape[:-1]
    return _mm(x.reshape(-1, x.shape[-1]), w, b).reshape(*lead, w.shape[1])


GQA_TQ = 256
GQA_TK_SELECT = 512
GQA_TK_BAND = 256


def _half_select(x, want_half, have_half):
    if want_half != have_half:
        x = pltpu.roll(x, HEAD_DIM, axis=1)
    return x


def _lane_half_mask(shape, half):
    lane = lax.broadcasted_iota(jnp.int32, shape, 1)
    return (lane >= half * HEAD_DIM) & (lane < (half + 1) * HEAD_DIM)


def _make_gqa_kernel(select, max_dist, n_outer):
    r, tq, tk = NSA_GROUP_HEADS, GQA_TQ, (GQA_TK_SELECT if select else GQA_TK_BAND)

    def kern(q_ref, k_ref, v_ref, *rest):
        if select:
            m_ref, e_ref, o_ref, mx_ref, acc_ref = rest
        else:
            o_ref, mx_ref, acc_ref = rest
        i = pl.program_id(n_outer)
        dist0 = i * tq + lax.broadcasted_iota(jnp.int32, (tk, tq), 1) - lax.broadcasted_iota(jnp.int32, (tk, tq), 0)
        qs = []
        for half in range(2):
            parts = []
            for rr in range(r):
                hh = half * r + rr
                qc = q_ref[:, (hh // 2) * V7X_LANES:(hh // 2 + 1) * V7X_LANES].astype(jnp.float32) * (HEAD_DIM ** -0.5)
                qc = _half_select(qc, half, hh % 2)
                parts.append(jnp.where(_lane_half_mask(qc.shape, half), qc, 0.0).astype(BF))
            qs.append(jnp.concatenate(parts, axis=0))
        mx_ref[...] = jnp.full(mx_ref.shape, NEG_BIG, jnp.float32)
        acc_ref[...] = jnp.zeros(acc_ref.shape, jnp.float32)
        v_lane = lax.broadcasted_iota(jnp.int32, (tk, V7X_LANES), 1)

        def body(j, carry):
            k0 = pl.multiple_of(j * tk, tk)
            kt = k_ref[pl.ds(k0, tk), :].astype(BF)
            vt = v_ref[pl.ds(k0, tk), :].astype(BF)
            dist = dist0 - k0
            in_band = (dist >= 0) if select else ((dist >= 0) & (dist <= max_dist))
            for half in range(2):
                if select:
                    blk = jnp.dot(e_ref[j], m_ref[half], preferred_element_type=jnp.float32)
                    valid = (blk > 0.5) & in_band
                else:
                    valid = in_band
                bias = jnp.where(valid, 0.0, NEG_BIG)
                s = _dot_nt(kt, qs[half]) + jnp.concatenate([bias] * r, axis=1)
                m_old = mx_ref[half]
                m_new = jnp.maximum(m_old, jnp.max(s, axis=0, keepdims=True))
                p = jnp.exp((s - m_new).astype(BF))
                v_aug = jnp.where((v_lane >= half * HEAD_DIM) & (v_lane < (half + 1) * HEAD_DIM), vt, 1.0)
                pv = lax.dot_general(v_aug, p, (((0,), (0,)), ((), ())), preferred_element_type=jnp.float32)
                acc_ref[half] = jnp.exp(m_old - m_new) * acc_ref[half] + pv
                mx_ref[half] = m_new
            return carry

        lo = 0 if select else jnp.maximum(i * tq - max_dist, 0) // tk
        lax.fori_loop(lo, (i * tq + tq - 1) // tk + 1, body, 0)
        outs = []
        for half in range(2):
            acc = acc_ref[half]
            den_row = (1 - half) * HEAD_DIM
            o = (acc / acc[den_row:den_row + 1, :]).T
            for rr in range(r):
                outs.append(_half_select(o[rr * tq:(rr + 1) * tq], (half * r + rr) % 2, half))
        lane = lax.broadcasted_iota(jnp.int32, (tq, V7X_LANES), 1)
        for c in range(r):
            both = jnp.where(lane < HEAD_DIM, outs[2 * c], outs[2 * c + 1])
            o_ref[:, c * V7X_LANES:(c + 1) * V7X_LANES] = both.astype(o_ref.dtype)

    return kern


def _sel_expand_matrix(seq):
    n_blk = seq // NSA_SEL_LEN
    e = (np.arange(seq)[:, None] // NSA_SEL_LEN == np.arange(n_blk)[None, :]).astype(np.float32)
    return e.reshape(seq // GQA_TK_SELECT, GQA_TK_SELECT, n_blk)


def nsa_gqa_attention_pallas(u, q_col, k_col, v_col, selmask=None):
    bsz, seq, _ = u.shape
    select = selmask is not None
    n_pairs = NSA_KV_GROUPS // 2
    qw = 2 * NSA_GROUP_HEADS * HEAD_DIM
    qb, kb, vb = q_col // qw, k_col // V7X_LANES, v_col // V7X_LANES
    rows = NSA_GROUP_HEADS * GQA_TQ
    in_specs = [
        pl.BlockSpec((None, GQA_TQ, qw), lambda b, gp, i: (b, i, qb + gp)),
        pl.BlockSpec((None, seq, V7X_LANES), lambda b, gp, i: (b, 0, kb + gp)),
        pl.BlockSpec((None, seq, V7X_LANES), lambda b, gp, i: (b, 0, vb + gp)),
    ]
    args = [u, u, u]
    if select:
        n_blk = selmask.shape[2]
        in_specs += [pl.BlockSpec((None, 2, n_blk, GQA_TQ), lambda b, gp, i: (b, gp, 0, i)),
                     pl.BlockSpec((seq // GQA_TK_SELECT, GQA_TK_SELECT, n_blk), lambda b, gp, i: (0, 0, 0))]
        args += [selmask, jnp.asarray(_sel_expand_matrix(seq), BF)]
    return pl.pallas_call(
        _make_gqa_kernel(select, NSA_WINDOW - 1, 2),
        out_shape=jax.ShapeDtypeStruct((bsz, seq, NSA_HEADS * HEAD_DIM), BF),
        grid=(bsz, n_pairs, seq // GQA_TQ),
        in_specs=in_specs,
        out_specs=pl.BlockSpec((None, GQA_TQ, qw), lambda b, gp, i: (b, i, gp)),
        scratch_shapes=[pltpu.VMEM((2, 1, rows), jnp.float32),
                        pltpu.VMEM((2, V7X_LANES, rows), jnp.float32)],
        compiler_params=_cparams(3),
        name="nsa_selected_attention" if select else "nsa_window_attention",
    )(*args)


DIL_TILE = 128


def _make_dil_kernel(dil, max_dist, seq):
    t = DIL_TILE
    n_sub = seq // dil // t
    n_kv = min(-(-max_dist // t) + 1, n_sub)

    def rows(first, size):
        return pl.ds(first, size, stride=dil) if dil > 1 else pl.ds(first, size)

    def kern(q_ref, k_ref, v_ref, o_ref, lse_ref):
        kw = n_kv * t
        lane = lax.broadcasted_iota(jnp.int32, (t, V7X_LANES), 1)
        for rr in range(dil):
            for i in range(n_sub):
                k_tile0 = max(i + 1 - n_kv, 0)
                dist = ((i - k_tile0) * t + lax.broadcasted_iota(jnp.int32, (t, kw), 0)
                        - lax.broadcasted_iota(jnp.int32, (t, kw), 1))
                bias = jnp.where((dist >= 0) & (dist <= max_dist), 0.0, NEG_BIG)
                q_rows = rows(rr + i * t * dil, t)
                k_rows = rows(rr + k_tile0 * t * dil, kw)
                qc = q_ref[q_rows, :] * (HEAD_DIM ** -0.5)
                kt = k_ref[k_rows, :].astype(BF)
                vt = v_ref[k_rows, :].astype(BF)
                os_, ls_ = [], []
                for half in range(2):
                    qh = jnp.where(_lane_half_mask(qc.shape, half), qc, 0.0).astype(BF)
                    s = _dot_nt(qh, kt) + bias
                    m = jnp.max(s, axis=-1, keepdims=True)
                    p = jnp.exp(s - m)
                    den = jnp.sum(p, axis=-1, keepdims=True)
                    os_.append(jnp.dot(p.astype(BF), vt, preferred_element_type=jnp.float32) / den)
                    ls_.append(jnp.broadcast_to(m + jnp.log(den), (t, V7X_LANES)))
                o_ref[q_rows, :] = jnp.where(lane < HEAD_DIM, os_[0], os_[1])
                lse_ref[q_rows, :] = jnp.where(lane < HEAD_DIM, ls_[0], ls_[1])

    return kern


def dilated_attention_pallas(u, q_col, k_col, v_col):
    bsz, seq, _ = u.shape
    gw = DIL_HEADS * HEAD_DIM
    outs, lses = [], []
    sds = jax.ShapeDtypeStruct((bsz, seq, gw), jnp.float32)
    out_spec = pl.BlockSpec((None, seq, V7X_LANES), lambda b, c: (b, 0, c))
    for gi, (window, dil) in enumerate(DIL_GROUPS):

        def cmap(col):
            blk = (col + gi * gw) // V7X_LANES
            return lambda b, c: (b, 0, blk + c)

        o, lse = pl.pallas_call(
            _make_dil_kernel(dil, window // dil, seq),
            out_shape=(sds, sds),
            grid=(bsz, gw // V7X_LANES),
            in_specs=[pl.BlockSpec((None, seq, V7X_LANES), cmap(c)) for c in (q_col, k_col, v_col)],
            out_specs=(out_spec, out_spec),
            compiler_params=_cparams(2),
            name="dilated_attention",
        )(u, u, u)
        outs.append(o)
        lses.append(lse)
    return outs, lses


GLA_SUB = 16


def _gla_kernel(q_ref, k_ref, v_ref, r_ref, lr_ref, wa_ref, ba_ref, g_ref, o_ref, state_ref):
    c, dk, dv, sub = GLA_CHUNK, GLA_DK, GLA_DV, GLA_SUB

    @pl.when(pl.program_id(1) == 0)
    def _():
        state_ref[...] = jnp.zeros(state_ref.shape, jnp.float32)

    x = _dot_split(lr_ref[...], wa_ref[...]) + ba_ref[...]
    log_a = (jnp.minimum(x, 0.0) - jnp.log1p(jnp.exp(-jnp.abs(x)))) * (1.0 / GLA_TAU)
    tri = (lax.broadcasted_iota(jnp.int32, (c, c), 0) >= lax.broadcasted_iota(jnp.int32, (c, c), 1))
    la_hi, la_lo = _split_bf16(log_a)
    tri = tri.astype(BF)
    b_all = (jnp.dot(tri, la_hi, preferred_element_type=jnp.float32)
             + jnp.dot(tri, la_lo, preferred_element_type=jnp.float32))
    col = lax.broadcasted_iota(jnp.int32, (sub, c), 1)
    row = lax.broadcasted_iota(jnp.int32, (sub, c), 0)
    for h in range(GLA_HEADS):
        q = q_ref[:, h * dk:(h + 1) * dk] * (dk ** -0.5)
        k = k_ref[:, h * dk:(h + 1) * dk]
        v = v_ref[:, h * dv:(h + 1) * dv]
        b = b_all[:, h * dk:(h + 1) * dk]
        a_rows = []
        for blk in range(c // sub):
            lo = blk * sub
            q_i, b_i, k_i = q[lo:lo + sub], b[lo:lo + sub], k[lo:lo + sub]
            if blk > 0:
                b_prev = b[lo - 1:lo]
                q_t = q_i * jnp.exp(b_i - b_prev)
                k_t = k * jnp.exp(jnp.minimum(b_prev - b, 0.0))
                acc = jnp.where(col < lo, _dot_nt(q_t.astype(BF), k_t.astype(BF)), 0.0)
            else:
                acc = jnp.zeros((sub, c), jnp.float32)
            for j in range(sub):
                t = q_i * k_i[j:j + 1] * jnp.exp(jnp.minimum(b_i - b_i[j:j + 1], 0.0))
                cs = jnp.sum(t, axis=-1, keepdims=True)
                acc = acc + jnp.where((col == lo + j) & (row >= j), cs, 0.0)
            a_rows.append(acc)
        attn = jnp.concatenate(a_rows, axis=0)
        st = state_ref[h]
        o = jnp.dot(attn.astype(BF), v.astype(BF), preferred_element_type=jnp.float32)
        o = o + _dot_nt((q * jnp.exp(b)).astype(BF), st.astype(BF))
        b_last = b[c - 1:c]
        k_d = k * jnp.exp(b_last - b)
        upd = lax.dot_general(v.astype(BF), k_d.astype(BF), (((0,), (0,)), ((), ())),
                              preferred_element_type=jnp.float32)
        state_ref[h] = st * jnp.exp(b_last) + upd
        o = o * lax.rsqrt(jnp.mean(o * o, axis=-1, keepdims=True) + LN_EPS)
        r = r_ref[:, h * dv:(h + 1) * dv]
        o_ref[:, h * dv:(h + 1) * dv] = o * g_ref[...] * (r / (1.0 + jnp.exp(-r)))


def gla_mixer_pallas(u_gla, u_small, w_alpha2, b_alpha, norm_g):
    bsz, seq, _ = u_gla.shape
    c = GLA_CHUNK

    def tok(width, blk):
        return pl.BlockSpec((None, c, width), lambda b, i: (b, i, blk))

    def whole(shape):
        return pl.BlockSpec(shape, lambda b, i: (0,) * len(shape))

    wa = jnp.pad(w_alpha2, ((0, V7X_LANES - GLA_RANK), (0, 0)))
    return pl.pallas_call(
        _gla_kernel,
        out_shape=jax.ShapeDtypeStruct((bsz, seq, B_V), jnp.float32),
        grid=(bsz, seq // c),
        in_specs=[tok(B_K, 0), tok(B_K, 1), tok(B_V, 1), tok(B_V, 2), tok(V7X_LANES, 0),
                  whole((V7X_LANES, B_K)), whole((1, B_K)), whole((1, GLA_DV))],
        out_specs=tok(B_V, 0),
        scratch_shapes=[pltpu.VMEM((GLA_HEADS, GLA_DV, GLA_DK), jnp.float32)],
        compiler_params=_cparams(2),
        name="gla_mixer",
    )(u_gla, u_gla, u_gla, u_gla, u_small, wa, b_alpha.reshape(1, -1), norm_g.reshape(1, -1))


CMP_HALF = NSA_CMP_STRIDE * HEAD_DIM


def _gelu_tanh(x):
    return 0.5 * x * (1.0 + jnp.tanh(np.sqrt(2.0 / np.pi) * (x + 0.044715 * (x * x * x))))


def _compress_kernel(hk_ref, hv_ref, pek_ref, w1k_ref, w2k_ref, pev_ref, w1v_ref, w2v_ref, kc_ref, vc_ref):
    for h_ref, pe_ref, w1_ref, w2_ref, o_ref in ((hk_ref, pek_ref, w1k_ref, w2k_ref, kc_ref),
                                                 (hv_ref, pev_ref, w1v_ref, w2v_ref, vc_ref)):
        w1a = w1_ref[0].astype(BF)
        w1b = w1_ref[1].astype(BF)
        pe = pe_ref[...].astype(BF)
        pe_term = (jnp.dot(pe[:, :CMP_HALF], w1a, preferred_element_type=jnp.float32)
                   + jnp.dot(pe[:, CMP_HALF:], w1b, preferred_element_type=jnp.float32))
        w2 = w2_ref[...].astype(BF)
        for g in range(NSA_KV_GROUPS):
            hb = h_ref[g].astype(BF)
            y1 = jnp.dot(hb, w1a, preferred_element_type=jnp.float32)
            y2 = jnp.dot(hb, w1b, preferred_element_type=jnp.float32)
            n_half = y2.shape[0]
            pre = y1 + pltpu.roll(y2, n_half - 1, axis=0) + pe_term
            out = jnp.dot(_gelu_tanh(pre).astype(BF), w2, preferred_element_type=jnp.float32)
            o_ref[g] = jnp.concatenate([out, out], axis=1)


def nsa_compress_pallas(u, k_col, v_col, pe_k, w1_k, w2_k, pe_v, w1_v, w2_v):
    bsz, seq, _ = u.shape
    n_half = seq // NSA_CMP_STRIDE

    def halves(col):
        t = u[..., col:col + C_KV].reshape(bsz, n_half, NSA_CMP_STRIDE, NSA_KV_GROUPS, HEAD_DIM)
        return t.transpose(0, 3, 1, 2, 4).reshape(bsz, NSA_KV_GROUPS, n_half, CMP_HALF)

    def whole(shape):
        return pl.BlockSpec(shape, lambda b: (0,) * len(shape))

    hspec = pl.BlockSpec((None, NSA_KV_GROUPS, n_half, CMP_HALF), lambda b: (b, 0, 0, 0))
    ospec = pl.BlockSpec((None, NSA_KV_GROUPS, n_half, V7X_LANES), lambda b: (b, 0, 0, 0))
    wspecs = [whole((1, 2 * CMP_HALF)), whole((2, CMP_HALF, NSA_CMP_HIDDEN)), whole((NSA_CMP_HIDDEN, HEAD_DIM))]
    sds = jax.ShapeDtypeStruct((bsz, NSA_KV_GROUPS, n_half, V7X_LANES), jnp.float32)
    return pl.pallas_call(
        _compress_kernel,
        out_shape=(sds, sds),
        grid=(bsz,),
        in_specs=[hspec, hspec] + wspecs + wspecs,
        out_specs=(ospec, ospec),
        compiler_params=_cparams(1),
        name="nsa_compress",
    )(halves(k_col), halves(v_col),
      pe_k.reshape(1, -1), w1_k.reshape(2, CMP_HALF, NSA_CMP_HIDDEN), w2_k,
      pe_v.reshape(1, -1), w1_v.reshape(2, CMP_HALF, NSA_CMP_HIDDEN), w2_v)


CMP_TQ = 1024


def _overlap_matrix(n_cmp_pad, n_blk):
    c0 = np.arange(n_cmp_pad)[:, None] * NSA_CMP_STRIDE
    s0 = np.arange(n_blk)[None, :] * NSA_SEL_LEN
    return ((c0 < s0 + NSA_SEL_LEN) & (c0 + NSA_CMP_LEN > s0)).astype(np.float32)


def _cmp_attn_kernel(q_ref, kc_ref, vc_ref, ov_ref, o_ref, m_ref):
    i = pl.program_id(2)
    tq = CMP_TQ
    n_c = kc_ref.shape[0]
    n_blk = ov_ref.shape[0]
    kc = kc_ref[...].astype(BF)
    vc = vc_ref[...].astype(BF)
    pos = i * tq + lax.broadcasted_iota(jnp.int32, (tq, n_c), 0)
    c_end = lax.broadcasted_iota(jnp.int32, (tq, n_c), 1) * NSA_CMP_STRIDE + (NSA_CMP_LEN - 1)
    cvalid = c_end <= pos
    lane = lax.broadcasted_iota(jnp.int32, (tq, V7X_LANES), 1)
    p_sum = jnp.zeros((tq, n_c), jnp.float32)
    outs = []
    for rr in range(NSA_GROUP_HEADS):
        qc = q_ref[:, (rr // 2) * V7X_LANES:(rr // 2 + 1) * V7X_LANES].astype(jnp.float32) * (HEAD_DIM ** -0.5)
        qh = jnp.where(_lane_half_mask(qc.shape, rr % 2), qc, 0.0).astype(BF)
        s = jnp.where(cvalid, _dot_nt(qh, kc), NEG_BIG)
        m = jnp.max(s, axis=-1, keepdims=True)
        p = jnp.where(cvalid, jnp.exp(s - m), 0.0)
        p = p / jnp.maximum(jnp.sum(p, axis=-1, keepdims=True), 1e-30)
        p_sum = p_sum + p
        outs.append(jnp.dot(p.astype(BF), vc, preferred_element_type=jnp.float32))
    for c in range(NSA_GROUP_HEADS // 2):
        both = jnp.where(lane < HEAD_DIM, outs[2 * c], outs[2 * c + 1])
        o_ref[:, c * V7X_LANES:(c + 1) * V7X_LANES] = both.astype(o_ref.dtype)
    ps_hi, ps_lo = _split_bf16(p_sum)
    imp = _dot_nt(ov_ref[...], ps_hi) + _dot_nt(ov_ref[...], ps_lo)
    j = lax.broadcasted_iota(jnp.int32, (n_blk, tq), 0)
    q_blk = (i * tq + lax.broadcasted_iota(jnp.int32, (n_blk, tq), 1)) // NSA_SEL_LEN
    forced = (j == 0) | (j == q_blk) | (j == q_blk - 1)
    imp = jnp.where(forced, jnp.inf, jnp.where(j > q_blk, -jnp.inf, imp))
    rank = jnp.zeros((n_blk, tq), jnp.float32)
    for jj in range(n_blk):
        other = imp[jj:jj + 1, :]
        beats = (other > imp) | ((other == imp) & (jj < j))
        rank = rank + jnp.where(beats, 1.0, 0.0)
    m_ref[...] = jnp.where(rank < float(min(NSA_N_SELECT, n_blk)), 1.0, 0.0).astype(m_ref.dtype)


def nsa_cmp_attention_pallas(u, q_col, kc, vc):
    bsz, seq, _ = u.shape
    n_c = kc.shape[2]
    n_blk = seq // NSA_SEL_LEN
    gw = NSA_GROUP_HEADS * HEAD_DIM
    qb = q_col // gw
    cspec = pl.BlockSpec((None, None, n_c, V7X_LANES), lambda b, g, i: (b, g, 0, 0))
    return pl.pallas_call(
        _cmp_attn_kernel,
        out_shape=(jax.ShapeDtypeStruct((bsz, seq, NSA_HEADS * HEAD_DIM), BF),
                   jax.ShapeDtypeStruct((bsz, NSA_KV_GROUPS, n_blk, seq), BF)),
        grid=(bsz, NSA_KV_GROUPS, seq // CMP_TQ),
        in_specs=[pl.BlockSpec((None, CMP_TQ, gw), lambda b, g, i: (b, i, qb + g)),
                  cspec, cspec,
                  pl.BlockSpec((n_blk, n_c), lambda b, g, i: (0, 0))],
        out_specs=(pl.BlockSpec((None, CMP_TQ, gw), lambda b, g, i: (b, i, g)),
                   pl.BlockSpec((None, None, n_blk, CMP_TQ), lambda b, g, i: (b, g, 0, i))),
        compiler_params=_cparams(3),
        name="nsa_cmp_attention",
    )(u, kc, vc, jnp.asarray(_overlap_matrix(n_c, n_blk).T, BF))


MERGE_TM = 256


def _layer_norm_rows(x, g, b):
    mu = jnp.mean(x, axis=-1, keepdims=True)
    xc = x - mu
    var = jnp.mean(xc * xc, axis=-1, keepdims=True)
    return xc * lax.rsqrt(var + LN_EPS) * g + b


def _gate_expand_matrices():
    e = np.zeros((NSA_BRANCHES, V7X_LANES, NSA_HEADS * HEAD_DIM), np.float32)
    for br in range(NSA_BRANCHES):
        for h in range(NSA_HEADS):
            e[br, GLA_RANK + NSA_BRANCHES * h + br, h * HEAD_DIM:(h + 1) * HEAD_DIM] = 1.0
    return e


def _sigmoid(x):
    return 1.0 / (1.0 + jnp.exp(-x))


def _merge_kernel(x_ref, o0_ref, o1_ref, o2_ref, l0_ref, l1_ref, l2_ref, yb_ref, oc_ref, os_ref, ow_ref,
                  us_ref, mg_ref, ex_ref, wa_ref, wb_ref, wc_ref, wo_ref, g_ref, b_ref, out_ref):
    l0, l1, l2 = l0_ref[...], l1_ref[...], l2_ref[...]
    lm = jnp.maximum(jnp.maximum(l0, l1), l2)
    e0, e1, e2 = jnp.exp(l0 - lm), jnp.exp(l1 - lm), jnp.exp(l2 - lm)
    y_a = (e0 * o0_ref[...] + e1 * o1_ref[...] + e2 * o2_ref[...]) / (e0 + e1 + e2)
    sg = _sigmoid(us_ref[...])
    f32 = jnp.float32
    y_c = (_dot_split_lhs(sg, ex_ref[0]) * oc_ref[...].astype(f32) + _dot_split_lhs(sg, ex_ref[1]) * os_ref[...].astype(f32)
           + _dot_split_lhs(sg, ex_ref[2]) * ow_ref[...].astype(f32))
    d = D_MODEL
    f = functools.partial(jnp.dot, preferred_element_type=jnp.float32)
    merged = (_sigmoid(mg_ref[:, :d].astype(f32)) * f(y_a.astype(BF), wa_ref[...])
              + _sigmoid(mg_ref[:, d:2 * d].astype(f32)) * f(yb_ref[...].astype(BF), wb_ref[...])
              + _sigmoid(mg_ref[:, 2 * d:].astype(f32)) * f(y_c.astype(BF), wc_ref[...]))
    mix = f(merged.astype(BF), wo_ref[...])
    out_ref[...] = _layer_norm_rows(DEEPNORM_ALPHA * x_ref[...] + mix, g_ref[...], b_ref[...])


def mixer_merge_pallas(x, dil_o, dil_lse, y_b, o_cmp, o_sel, o_win, u_small, m_g, w_br_a, w_br_b, w_br_c, w_o_mix, ln_g, ln_b):
    n, d = x.shape
    tm = MERGE_TM
    aw = DIL_HEADS * HEAD_DIM

    def tok(width):
        return pl.BlockSpec((tm, width), lambda i: (i, 0))

    def whole(shape):
        return pl.BlockSpec(shape, lambda i: (0,) * len(shape))

    ex = jnp.asarray(_gate_expand_matrices(), BF)
    return pl.pallas_call(
        _merge_kernel,
        out_shape=jax.ShapeDtypeStruct((n, d), jnp.float32),
        grid=(n // tm,),
        in_specs=[tok(d)] + [tok(aw)] * 6 + [tok(B_V)] + [tok(C_Q)] * 3 + [tok(V7X_LANES), tok(N_BRANCHES * d),
                  whole(ex.shape), whole((aw, d)), whole((B_V, d)), whole((C_Q, d)), whole((d, d)),
                  whole((1, d)), whole((1, d))],
        out_specs=tok(d),
        compiler_params=_cparams(1),
        name="mixer_merge",
    )(x, *dil_o, *dil_lse, y_b, o_cmp, o_sel, o_win, u_small, m_g, ex,
      w_br_a.astype(BF), w_br_b.astype(BF), w_br_c.astype(BF), w_o_mix.astype(BF),
      ln_g.reshape(1, d), ln_b.reshape(1, d))


XATTN_TM = 256


def _xattn_kernel(x_ref, k_ref, v_ref, wq_ref, wo_ref, g_ref, b_ref, o_ref):
    x = x_ref[...]
    d = x.shape[-1]
    hd = d // XATTN_HEADS
    q = jnp.dot(x.astype(BF), wq_ref[...], preferred_element_type=jnp.float32)
    outs = []
    for h in range(XATTN_HEADS):
        sl = slice(h * hd, (h + 1) * hd)
        s = _dot_nt((q[:, sl] * (hd ** -0.5)).astype(BF), k_ref[:, sl].astype(BF))
        p = jnp.exp(s - jnp.max(s, axis=-1, keepdims=True))
        den = jnp.sum(p, axis=-1, keepdims=True)
        outs.append(jnp.dot(p.astype(BF), v_ref[:, sl].astype(BF), preferred_element_type=jnp.float32) / den)
    o = jnp.concatenate(outs, axis=1)
    xa = jnp.dot(o.astype(BF), wo_ref[...], preferred_element_type=jnp.float32)
    o_ref[...] = _layer_norm_rows(DEEPNORM_ALPHA * x + xa, g_ref[...], b_ref[...])


def cross_attention_block_pallas(x, mem, w_xq, w_xk, w_xv, w_xo, ln_g, ln_b):
    bsz, seq, d = x.shape
    mem_len = mem.shape[1]
    mem2 = mem.reshape(-1, d)
    k = _mm(mem2, w_xk)
    v = _mm(mem2, w_xv)
    n_t = seq // XATTN_TM

    def whole(shape):
        return pl.BlockSpec(shape, lambda b, i: (0,) * len(shape))

    out = pl.pallas_call(
        _xattn_kernel,
        out_shape=jax.ShapeDtypeStruct((bsz * seq, d), jnp.float32),
        grid=(bsz, n_t),
        in_specs=[pl.BlockSpec((XATTN_TM, d), lambda b, i: (b * n_t + i, 0)),
                  pl.BlockSpec((mem_len, d), lambda b, i: (b, 0)),
                  pl.BlockSpec((mem_len, d), lambda b, i: (b, 0)),
                  whole((d, d)), whole((d, d)), whole((1, d)), whole((1, d))],
        out_specs=pl.BlockSpec((XATTN_TM, d), lambda b, i: (b * n_t + i, 0)),
        compiler_params=_cparams(2),
        name="cross_attention_block",
    )(x.reshape(-1, d), k, v, w_xq.astype(BF), w_xo.astype(BF), ln_g.reshape(1, d), ln_b.reshape(1, d))
    return out.reshape(bsz, seq, d)


def _ln_kernel(x_ref, g_ref, b_ref, o_ref, obf_ref):
    y = _layer_norm_rows(x_ref[...], g_ref[...], b_ref[...])
    o_ref[...] = y
    obf_ref[...] = y.astype(BF)


def layer_norm_pallas(x, g, b):
    n, d = x.shape
    tm = _pick_tile(n, (1024, 512, 256, 128, 64, 32, 16, 8))
    spec = pl.BlockSpec((tm, d), lambda i: (i, 0))
    return pl.pallas_call(
        _ln_kernel,
        out_shape=(jax.ShapeDtypeStruct((n, d), jnp.float32), jax.ShapeDtypeStruct((n, d), BF)),
        grid=(n // tm,),
        in_specs=[spec, pl.BlockSpec((1, d), lambda i: (0, 0)), pl.BlockSpec((1, d), lambda i: (0, 0))],
        out_specs=(spec, spec),
        compiler_params=_cparams(1),
        name="layer_norm",
    )(x, g.reshape(1, d), b.reshape(1, d))


MOE_TB = 512


def _moe_ffn_kernel(layer, be_ref, nxt_ref, nused_ref, x_ref, wgu_hbm, bgu_ref, wd_hbm, bd_ref, o_ref,
                    wgu_st, wd_st, wgu_bf, wd_bf, sem):
    i = pl.program_id(0)
    e = be_ref[i]
    changed = jnp.logical_or(i == 0, e != be_ref[jnp.maximum(i - 1, 0)])

    def fetch(expert):
        return (pltpu.make_async_copy(wgu_hbm.at[layer, expert], wgu_st, sem.at[0]),
                pltpu.make_async_copy(wd_hbm.at[layer, expert], wd_st, sem.at[1]))

    @pl.when(i == 0)
    def _():
        for cp in fetch(e):
            cp.start()

    @pl.when(changed)
    def _():
        for cp in fetch(e):
            cp.wait()
        wgu_bf[...] = wgu_st[...].astype(BF)
        wd_bf[...] = wd_st[...].astype(BF)

        @pl.when(nxt_ref[i] != e)
        def _():
            for cp in fetch(nxt_ref[i]):
                cp.start()

    @pl.when(i < nused_ref[0])
    def _():
        gu = jnp.dot(x_ref[...].astype(BF), wgu_bf[...], preferred_element_type=jnp.float32) + bgu_ref[...]
        gate = jnp.minimum(gu[:, :D_FF], SWIGLU_LIMIT)
        up = jnp.clip(gu[:, D_FF:], -SWIGLU_LIMIT, SWIGLU_LIMIT)
        act = (up + 1.0) * gate * (1.0 / (1.0 + jnp.exp(-SWIGLU_ALPHA * gate)))
        o_ref[...] = jnp.dot(act.astype(BF), wd_bf[...], preferred_element_type=jnp.float32) + bd_ref[...]

    @pl.when(i >= nused_ref[0])
    def _():
        o_ref[...] = jnp.zeros(o_ref.shape, o_ref.dtype)


def moe_expert_ffn_pallas(buf, block_e, next_e, n_used, layer, w_gu, b_gu, w_down, b_down):
    n_rows, d = buf.shape
    depth, n_e = w_gu.shape[:2]
    grid_spec = pltpu.PrefetchScalarGridSpec(
        num_scalar_prefetch=3,
        grid=(n_rows // MOE_TB,),
        in_specs=[
            pl.BlockSpec((MOE_TB, d), lambda i, be, nx, nu: (jnp.minimum(i, nu[0] - 1), 0)),
            pl.BlockSpec(memory_space=pl.ANY),
            pl.BlockSpec((None, None, 1, 2 * D_FF), lambda i, be, nx, nu: (layer, be[i], 0, 0)),
            pl.BlockSpec(memory_space=pl.ANY),
            pl.BlockSpec((None, None, 1, d), lambda i, be, nx, nu: (layer, be[i], 0, 0)),
        ],
        out_specs=pl.BlockSpec((MOE_TB, d), lambda i, be, nx, nu: (i, 0)),
        scratch_shapes=[pltpu.VMEM((d, 2 * D_FF), jnp.float32), pltpu.VMEM((D_FF, d), jnp.float32),
                        pltpu.VMEM((d, 2 * D_FF), BF), pltpu.VMEM((D_FF, d), BF),
                        pltpu.SemaphoreType.DMA((2,))],
    )
    return pl.pallas_call(
        functools.partial(_moe_ffn_kernel, layer),
        out_shape=jax.ShapeDtypeStruct((n_rows, d), jnp.float32),
        grid_spec=grid_spec,
        compiler_params=_cparams(1),
        name="moe_expert_ffn",
    )(block_e, next_e, n_used, buf, w_gu, b_gu.reshape(depth, n_e, 1, -1), w_down, b_down.reshape(depth, n_e, 1, -1))


MOE_ROUTE_TM = 256
MOE_RUN_ALIGN = 8


def _router_kernel(x_ref, wr_ref, br_ref, route_ref, counts_ref, tile_n_ref, tile_c_ref, carry_ref):
    tm = MOE_ROUTE_TM

    @pl.when(pl.program_id(0) == 0)
    def _():
        carry_ref[...] = jnp.zeros(carry_ref.shape, jnp.float32)

    logits = _dot_split(x_ref[...], wr_ref[...]) + br_ref[...]
    lane = lax.broadcasted_iota(jnp.int32, (tm, N_EXPERTS), 1).astype(jnp.float32)
    onehots, vals, idxs = [], [], []
    rest = logits
    for _ in range(TOP_K):
        m = jnp.max(rest, axis=-1, keepdims=True)
        idx = jnp.min(jnp.where(rest == m, lane, float(N_EXPERTS)), axis=-1, keepdims=True)
        hit = lane == idx
        onehots.append(hit)
        vals.append(m)
        idxs.append(idx)
        rest = jnp.where(hit, -jnp.inf, rest)
    exps = [jnp.exp(v - vals[0]) for v in vals]
    den = exps[0] + exps[1] + exps[2] + exps[3]
    cnt = jnp.zeros((tm, N_EXPERTS), jnp.float32)
    for hit in onehots:
        cnt = cnt + jnp.where(hit, 1.0, 0.0)
    earlier = (lax.broadcasted_iota(jnp.int32, (tm, tm), 1) < lax.broadcasted_iota(jnp.int32, (tm, tm), 0))
    local_before = jnp.dot(earlier.astype(BF), cnt.astype(BF), preferred_element_type=jnp.float32)
    carry = carry_ref[...]
    tile_n = jnp.sum(cnt, axis=0, keepdims=True)
    tile_n = jnp.floor((tile_n + (MOE_RUN_ALIGN - 1.0)) * (1.0 / MOE_RUN_ALIGN)) * MOE_RUN_ALIGN
    lower = (lax.broadcasted_iota(jnp.int32, (N_EXPERTS, N_EXPERTS), 0)
             < lax.broadcasted_iota(jnp.int32, (N_EXPERTS, N_EXPERTS), 1))
    tile_off = jnp.dot(jnp.broadcast_to(tile_n, (8, N_EXPERTS)).astype(BF), lower.astype(BF),
                       preferred_element_type=jnp.float32)[:1]
    lane_out = lax.broadcasted_iota(jnp.int32, (tm, V7X_LANES), 1)
    out = jnp.zeros((tm, V7X_LANES), jnp.float32)
    for k in range(TOP_K):
        pos = jnp.sum(jnp.where(onehots[k], local_before + carry, 0.0), axis=-1, keepdims=True)
        rank = jnp.sum(jnp.where(onehots[k], local_before + tile_off, 0.0), axis=-1, keepdims=True)
        out = jnp.where(lane_out == k, idxs[k], out)
        out = jnp.where(lane_out == TOP_K + k, exps[k] / den, out)
        out = jnp.where(lane_out == 2 * TOP_K + k, pos, out)
        out = jnp.where(lane_out == 3 * TOP_K + k, rank, out)
    route_ref[...] = out
    tile_n_ref[...] = tile_n
    tile_c_ref[...] = carry
    carry_ref[...] = carry + tile_n
    counts_ref[...] = carry + tile_n


def moe_router_pallas(x, w_router, b_router):
    n, d = x.shape
    tm = MOE_ROUTE_TM
    return pl.pallas_call(
        _router_kernel,
        out_shape=(jax.ShapeDtypeStruct((n, V7X_LANES), jnp.float32),
                   jax.ShapeDtypeStruct((1, N_EXPERTS), jnp.float32),
                   jax.ShapeDtypeStruct((n // tm, 1, N_EXPERTS), jnp.float32),
                   jax.ShapeDtypeStruct((n // tm, 1, N_EXPERTS), jnp.float32)),
        grid=(n // tm,),
        in_specs=[pl.BlockSpec((tm, d), lambda i: (i, 0)),
                  pl.BlockSpec((d, N_EXPERTS), lambda i: (0, 0)),
                  pl.BlockSpec((1, N_EXPERTS), lambda i: (0, 0))],
        out_specs=(pl.BlockSpec((tm, V7X_LANES), lambda i: (i, 0)),
                   pl.BlockSpec((1, N_EXPERTS), lambda i: (0, 0)),
                   pl.BlockSpec((None, 1, N_EXPERTS), lambda i: (i, 0, 0)),
                   pl.BlockSpec((None, 1, N_EXPERTS), lambda i: (i, 0, 0))),
        scratch_shapes=[pltpu.VMEM((1, N_EXPERTS), jnp.float32)],
        compiler_params=_cparams(1),
        name="moe_router",
    )(x, w_router, b_router.reshape(1, -1))


MOE_RUN_BITS = tuple(range(MOE_ROUTE_TM.bit_length() - 1, MOE_RUN_ALIGN.bit_length() - 2, -1))
MOE_SORT_ROWS = TOP_K * MOE_ROUTE_TM + N_EXPERTS * MOE_RUN_ALIGN
MOE_RUNS_PER_TILE = N_EXPERTS + 1


def _for_each_run_chunk(tile, n_ref, sorted_off_ref, buf_off_ref, make_copy):
    for e in range(MOE_RUNS_PER_TILE):
        n = n_ref[tile * MOE_RUNS_PER_TILE + e]
        s_row = sorted_off_ref[tile * MOE_RUNS_PER_TILE + e]
        b_row = buf_off_ref[tile * MOE_RUNS_PER_TILE + e]
        for bit in MOE_RUN_BITS:
            size = 1 << bit
            hit = (n & size) != 0

            @pl.when(hit)
            def _(s_row=s_row, b_row=b_row, size=size):
                make_copy(pl.multiple_of(s_row, MOE_RUN_ALIGN), pl.multiple_of(b_row, MOE_RUN_ALIGN), size).start()

            step = jnp.where(hit, size, 0)
            s_row = s_row + step
            b_row = b_row + step


def _rank_onehot(route_ref, k):
    tm = MOE_ROUTE_TM
    ranks = lax.broadcasted_iota(jnp.int32, (tm, MOE_SORT_ROWS), 1).astype(jnp.float32)
    return ranks == route_ref[:, 3 * TOP_K + k:3 * TOP_K + k + 1]


def _dispatch_kernel(zrow_ref, n_ref, soff_ref, boff_ref, x_ref, route_ref, buf_ref, zero_ref, sorted_ref, sem, zsem):
    tm = MOE_ROUTE_TM
    tile = pl.program_id(0)

    @pl.when(pl.program_id(0) == 0)
    def _():
        zero_ref[...] = jnp.zeros(zero_ref.shape, jnp.float32)
        for j in range(zrow_ref.shape[0]):
            @pl.when(zrow_ref[j] >= 0)
            def _():
                row0 = pl.multiple_of(zrow_ref[j], MOE_TB)
                pltpu.make_async_copy(zero_ref, buf_ref.at[pl.ds(row0, MOE_TB)], zsem).start()
        for j in range(zrow_ref.shape[0]):
            @pl.when(zrow_ref[j] >= 0)
            def _():
                pltpu.make_async_copy(zero_ref, buf_ref.at[pl.ds(0, MOE_TB)], zsem).wait()

    slot = tile % 2
    last = pl.num_programs(0) - 1

    def drain(s):
        pltpu.make_async_copy(sorted_ref.at[s], buf_ref.at[pl.ds(0, MOE_SORT_ROWS)], sem.at[s]).wait()

    @pl.when(tile >= 2)
    def _():
        drain(slot)

    perm = _rank_onehot(route_ref, 0)
    for k in range(1, TOP_K):
        perm = perm | _rank_onehot(route_ref, k)
    sorted_ref[slot] = lax.dot_general(jnp.where(perm, 1.0, 0.0).astype(BF), x_ref[...].astype(BF), (((0,), (0,)), ((), ())),
                                       preferred_element_type=jnp.float32)
    _for_each_run_chunk(tile, n_ref, soff_ref, boff_ref, lambda s_row, b_row, size: pltpu.make_async_copy(
        sorted_ref.at[slot, pl.ds(s_row, size)], buf_ref.at[pl.ds(b_row, size)], sem.at[slot]))

    @pl.when(tile == last)
    def _():
        @pl.when(tile >= 1)
        def _():
            drain(1 - slot)
        drain(slot)


def _run_tables(tile_n, tile_c, pstart, trash_row):
    n_t = tile_n.reshape(-1, N_EXPERTS).astype(jnp.int32)
    c_t = tile_c.reshape(-1, N_EXPERTS).astype(jnp.int32)
    ends = jnp.cumsum(n_t, axis=1)
    total = ends[:, -1:]
    n_all = jnp.concatenate([n_t, MOE_SORT_ROWS - total], axis=1)
    sorted_off = jnp.concatenate([ends - n_t, total], axis=1)
    parity = (jnp.arange(n_t.shape[0], dtype=jnp.int32) % 2)[:, None]
    buf_off = jnp.concatenate([pstart[None, :] + c_t, trash_row - parity * MOE_TB], axis=1)
    return n_all.reshape(-1), sorted_off.reshape(-1), buf_off.reshape(-1)


def moe_dispatch_pallas(x, route, runs, zero_rows, n_rows):
    n, d = x.shape
    tm = MOE_ROUTE_TM
    grid_spec = pltpu.PrefetchScalarGridSpec(
        num_scalar_prefetch=4,
        grid=(n // tm,),
        in_specs=[pl.BlockSpec((tm, d), lambda i, *_: (i, 0)),
                  pl.BlockSpec((tm, V7X_LANES), lambda i, *_: (i, 0))],
        out_specs=pl.BlockSpec(memory_space=pl.ANY),
        scratch_shapes=[pltpu.VMEM((MOE_TB, d), jnp.float32), pltpu.VMEM((2, MOE_SORT_ROWS, d), jnp.float32),
                        pltpu.SemaphoreType.DMA((2,)), pltpu.SemaphoreType.DMA],
    )
    return pl.pallas_call(
        _dispatch_kernel,
        out_shape=jax.ShapeDtypeStruct((n_rows, d), jnp.float32),
        grid_spec=grid_spec,
        compiler_params=_cparams(1),
        name="moe_dispatch",
    )(zero_rows, *runs, x, route)


def _combine_kernel(n_ref, soff_ref, boff_ref, x_ref, route_ref, obuf_ref, g_ref, b_ref, o_ref, obf_ref, sorted_ref, sem):
    tm = MOE_ROUTE_TM
    tile = pl.program_id(0)
    slot = tile % 2

    def gather(t, s):
        _for_each_run_chunk(t, n_ref, soff_ref, boff_ref, lambda s_row, b_row, size: pltpu.make_async_copy(
            obuf_ref.at[pl.ds(b_row, size)], sorted_ref.at[s, pl.ds(s_row, size)], sem.at[s]))

    @pl.when(tile == 0)
    def _():
        gather(0, 0)

    @pl.when(tile + 1 < pl.num_programs(0))
    def _():
        gather(tile + 1, 1 - slot)

    gperm = jnp.zeros((tm, MOE_SORT_ROWS), jnp.float32)
    for k in range(TOP_K):
        gperm = jnp.where(_rank_onehot(route_ref, k), route_ref[:, TOP_K + k:TOP_K + k + 1], gperm)
    pltpu.make_async_copy(obuf_ref.at[pl.ds(0, MOE_SORT_ROWS)], sorted_ref.at[slot], sem.at[slot]).wait()
    ff = _dot_split(gperm, sorted_ref[slot])
    y = _layer_norm_rows(DEEPNORM_ALPHA * x_ref[...] + ff, g_ref[...], b_ref[...])
    o_ref[...] = y
    obf_ref[...] = y.astype(BF)


def moe_combine_pallas(x, route, runs, out_buf, ln_g, ln_b):
    n, d = x.shape
    tm = MOE_ROUTE_TM
    spec = pl.BlockSpec((tm, d), lambda i, *_: (i, 0))
    grid_spec = pltpu.PrefetchScalarGridSpec(
        num_scalar_prefetch=3,
        grid=(n // tm,),
        in_specs=[spec,
                  pl.BlockSpec((tm, V7X_LANES), lambda i, *_: (i, 0)),
                  pl.BlockSpec(memory_space=pl.ANY),
                  pl.BlockSpec((1, d), lambda i, *_: (0, 0)), pl.BlockSpec((1, d), lambda i, *_: (0, 0))],
        out_specs=(spec, spec),
        scratch_shapes=[pltpu.VMEM((2, MOE_SORT_ROWS, d), jnp.float32), pltpu.SemaphoreType.DMA((2,))],
    )
    return pl.pallas_call(
        _combine_kernel,
        out_shape=(jax.ShapeDtypeStruct((n, d), jnp.float32), jax.ShapeDtypeStruct((n, d), BF)),
        grid_spec=grid_spec,
        compiler_params=_cparams(1),
        name="moe_combine",
    )(*runs, x, route, out_buf, ln_g.reshape(1, d), ln_b.reshape(1, d))


def token_mixer_block(h, h_bf, w_in, b_in, w_alpha2, b_alpha, gla_norm_g, cmp_pe_k, cmp_w1_k, cmp_w2_k,
                      cmp_pe_v, cmp_w1_v, cmp_w2_v, w_br_a, w_br_b, w_br_c, w_o_mix, ln_g, ln_b):
    bsz, seq, d = h.shape
    offs = np.concatenate([[0], np.cumsum(IN_WIDTHS)]).tolist()
    h2 = h.reshape(-1, d)

    def proj(lo, hi, out_dtype=jnp.float32):
        return _mm(h_bf, w_in[:, offs[lo]:offs[hi]], b_in[offs[lo]:offs[hi]], out_dtype).reshape(bsz, seq, -1)

    u_dil = proj(0, 3)
    u_gla = proj(3, 7)
    w_small = jnp.concatenate([w_in[:, offs[7]:offs[8]], w_in[:, offs[15]:offs[16]]], axis=1)
    b_small = jnp.concatenate([b_in[offs[7]:offs[8]], b_in[offs[15]:offs[16]]])
    n_small = w_small.shape[1]
    w_small = jnp.pad(w_small, ((0, 0), (0, V7X_LANES - n_small)))
    b_small = jnp.pad(b_small, (0, V7X_LANES - n_small))
    u_small = _mm(h_bf, w_small, b_small)
    u_c = proj(8, 15, BF)
    m_g = _mm(h_bf, w_in[:, offs[16]:offs[17]], b_in[offs[16]:offs[17]], BF)
    dil_o, dil_lse = dilated_attention_pallas(u_dil, 0, A_W, 2 * A_W)
    y_b = gla_mixer_pallas(u_gla, u_small.reshape(bsz, seq, V7X_LANES), w_alpha2, b_alpha, gla_norm_g)
    kc, vc = nsa_compress_pallas(u_c, C_Q, C_Q + C_KV, cmp_pe_k, cmp_w1_k, cmp_w2_k, cmp_pe_v, cmp_w1_v, cmp_w2_v)
    o_cmp, selmask = nsa_cmp_attention_pallas(u_c, 0, kc, vc)
    o_sel = nsa_gqa_attention_pallas(u_c, 0, C_Q + 2 * C_KV, C_Q + 3 * C_KV, selmask)
    o_win = nsa_gqa_attention_pallas(u_c, 0, C_Q + 4 * C_KV, C_Q + 5 * C_KV)

    def flat(t):
        return t.reshape(bsz * seq, -1)

    out = mixer_merge_pallas(h2, [flat(t) for t in dil_o], [flat(t) for t in dil_lse], flat(y_b), flat(o_cmp),
                             flat(o_sel), flat(o_win), u_small, m_g, w_br_a, w_br_b, w_br_c, w_o_mix, ln_g, ln_b)
    return out.reshape(bsz, seq, d)


def moe_block(x, layer, w_router, b_router, w_gu, b_gu, w_down, b_down, ln_g, ln_b):
    n, d = x.shape
    route, counts, tile_n, tile_c = moe_router_pallas(x, w_router, b_router)
    ids = jnp.arange(N_EXPERTS, dtype=jnp.int32)
    counts = counts.reshape(-1).astype(jnp.int32)
    padded = (counts + MOE_TB - 1) // MOE_TB * MOE_TB
    pend = jnp.cumsum(padded)
    pstart = pend - padded
    n_rows = n * TOP_K + (n // MOE_ROUTE_TM) * N_EXPERTS * MOE_RUN_ALIGN + (N_EXPERTS + 2) * MOE_TB
    runs = _run_tables(tile_n, tile_c, pstart, n_rows - MOE_TB)
    n_blocks = n_rows // MOE_TB
    n_used = pend[-1:] // MOE_TB
    blk = jnp.minimum(jnp.arange(n_blocks, dtype=jnp.int32), n_used - 1) * MOE_TB
    block_e = jnp.minimum(jnp.sum((pend[None, :] <= blk[:, None]).astype(jnp.int32), axis=1), N_EXPERTS - 1)
    later_nonempty = (ids[None, :] > ids[:, None]) & (counts[None, :] > 0)
    next_of = jnp.min(jnp.where(later_nonempty, ids[None, :], N_EXPERTS), axis=1)
    next_of = jnp.where(next_of < N_EXPERTS, next_of, ids)
    next_e = jnp.sum(jnp.where(block_e[:, None] == ids, next_of, 0), axis=1)
    tail = pend[-1] + jnp.arange((n_rows - n * TOP_K) // MOE_TB, dtype=jnp.int32) * MOE_TB
    zero_rows = jnp.concatenate([jnp.where(counts > 0, pend - MOE_TB, -1), jnp.where(tail < n_rows, tail, -1)])
    buf = moe_dispatch_pallas(x, route, runs, zero_rows, n_rows)
    out = moe_expert_ffn_pallas(buf, block_e, next_e, n_used, layer, w_gu, b_gu, w_down, b_down)
    return moe_combine_pallas(x, route, runs, out, ln_g, ln_b)


def kernel(x, mem, ln0_g, ln0_b, w_in, b_in, w_alpha2, b_alpha, gla_norm_g, cmp_pe_k, cmp_w1_k, cmp_w2_k, cmp_pe_v, cmp_w1_v, cmp_w2_v, w_br_a, w_br_b, w_br_c, w_o_mix, ln1_g, ln1_b, w_xq, w_xk, w_xv, w_xo, ln2_g, ln2_b, w_router, b_router, w_gu, b_gu, w_down, b_down, ln3_g, ln3_b):
    shape = x.shape
    x, x_bf = layer_norm_pallas(x.reshape(-1, shape[-1]), ln0_g, ln0_b)
    for li in range(DEPTH):
        x = token_mixer_block(x.reshape(shape), x_bf, w_in[li], b_in[li], w_alpha2[li], b_alpha[li], gla_norm_g[li],
                              cmp_pe_k[li], cmp_w1_k[li], cmp_w2_k[li], cmp_pe_v[li], cmp_w1_v[li], cmp_w2_v[li],
                              w_br_a[li], w_br_b[li], w_br_c[li], w_o_mix[li], ln1_g[li], ln1_b[li])
        x = cross_attention_block_pallas(x, mem, w_xq[li], w_xk[li], w_xv[li], w_xo[li], ln2_g[li], ln2_b[li])
        x, x_bf = moe_block(x.reshape(-1, shape[-1]), li, w_router[li], b_router[li], w_gu, b_gu, w_down, b_down,
                            ln3_g[li], ln3_b[li])
    return x.reshape(shape)
```
